```python
import jax, jax.numpy as jnp
from jax import lax
import numpy as np

D_MODEL = 1024
BATCH = 2
SEQ = 8192
DEPTH = 2

MEM_LEN = 256
MIX_WIDTH = D_MODEL
HG_WIDTH = MIX_WIDTH // 2
HG_HEADS = 4
HG_DK = HG_WIDTH // HG_HEADS
HG_DV = HG_WIDTH // HG_HEADS
HG_CHUNK = 64
ATT_WIDTH = MIX_WIDTH - HG_WIDTH
ATT_HEADS = 8
ATT_DH = ATT_WIDTH // ATT_HEADS
DILATED_PATTERNS = ((128, 1), (512, 4), (2048, 16))
ROPE_THETA = 10000.0
X_HEADS = 4
X_DH = D_MODEL // X_HEADS
N_GROUPS = 4
EXPERTS_PER_GROUP = 4
N_EXPERTS = N_GROUPS * EXPERTS_PER_GROUP
TOP_K_IN_GROUP = 2
EXPERT_FF = 512
NORM_EPS = 1e-6
IN_PROJ_WIDTH = 4 * HG_WIDTH + 3 * ATT_WIDTH
IN_SPLITS = (HG_WIDTH, 2 * HG_WIDTH, 3 * HG_WIDTH, 4 * HG_WIDTH,
             4 * HG_WIDTH + ATT_WIDTH, 4 * HG_WIDTH + 2 * ATT_WIDTH)

kernel_name = 'hybrid_hgrn2_dilated_attn_hmoe'

F32 = jnp.float32


def rms_norm(x, g):
    xf = x.astype(F32)
    y = xf * lax.rsqrt(jnp.mean(xf * xf, axis=-1, keepdims=True) + NORM_EPS)
    return (y * g.astype(F32)).astype(x.dtype)


def rope(t, positions):
    dh = t.shape[-1]
    inv = ROPE_THETA ** (-jnp.arange(0, dh, 2, dtype=F32) / dh)
    ang = positions.astype(F32)[:, None, :, None] * inv
    cos, sin = jnp.cos(ang), jnp.sin(ang)
    t1, t2 = jnp.split(t, 2, axis=-1)
    return jnp.concatenate([t1 * cos - t2 * sin, t2 * cos + t1 * sin], axis=-1)


def chunked_gated_recurrence(q, k, v, logf):
    B, H, S, dk = q.shape
    dv = v.shape[-1]
    nc = S // HG_CHUNK

    def chunks(t):
        return t.reshape(B, H, nc, HG_CHUNK, t.shape[-1]).transpose(2, 0, 1, 3, 4)

    causal = jnp.tril(jnp.ones((HG_CHUNK, HG_CHUNK), dtype=bool))

    def step(state, xs):
        qc, kc, vc, gc = xs
        b = jnp.cumsum(gc, axis=-2)
        diff = b[:, :, :, None, :] - b[:, :, None, :, :]
        decay = jnp.exp(jnp.where(causal[:, :, None], diff, -jnp.inf))
        scores = jnp.einsum('bhtk,bhsk,bhtsk->bhts', qc, kc, decay)
        o = (jnp.einsum('bhts,bhsv->bhtv', scores, vc)
             + jnp.einsum('bhtk,bhkv->bhtv', qc * jnp.exp(b), state))
        b_last = b[:, :, -1:, :]
        state = (jnp.exp(b_last)[:, :, 0, :, None] * state
                 + jnp.einsum('bhsk,bhsv->bhkv', kc * jnp.exp(b_last - b), vc))
        return state, o

    state0 = jnp.zeros((B, H, dk, dv), F32)
    _, outs = lax.scan(step, state0, (chunks(q), chunks(k), chunks(v), chunks(logf)))
    return outs.transpose(1, 2, 0, 3, 4).reshape(B, H, S, dv)


def hgrn2_mixer(q, f_logit, i, g, lb, out_gain):
    B, S, _ = q.shape

    def heads(t):
        return t.reshape(B, S, HG_HEADS, -1).transpose(0, 2, 1, 3).astype(F32)

    qh, zf, ih = heads(q), heads(f_logit), heads(i)
    lbh = lb.astype(F32).reshape(HG_HEADS, 1, HG_DK)
    f = lbh + (1.0 - lbh) * jax.nn.sigmoid(zf)
    o = chunked_gated_recurrence(qh, 1.0 - f, ih, jnp.log(f))
    o = o * lax.rsqrt(jnp.mean(o * o, axis=-1, keepdims=True) + NORM_EPS)
    o = o * out_gain.astype(F32).reshape(HG_HEADS, 1, HG_DV)
    o = o.transpose(0, 2, 1, 3).reshape(B, S, HG_WIDTH)
    return (o * jax.nn.silu(g.astype(F32))).astype(q.dtype)


def dilated_branch(q, k, v, dilation, n_back):
    B, H, S, dh = q.shape
    L = S // dilation
    blk = n_back
    nb = -(-L // blk)
    pad = nb * blk - L

    def to_blocks(t):
        t = t.reshape(B, H, L, dilation, dh).transpose(0, 1, 3, 2, 4)
        t = jnp.pad(t, ((0, 0), (0, 0), (0, 0), (0, pad), (0, 0)))
        return t.reshape(B, H, dilation, nb, blk, dh)

    def with_prev(t):
        prev = jnp.pad(t, ((0, 0), (0, 0), (0, 0), (1, 0), (0, 0), (0, 0)))[:, :, :, :-1]
        return jnp.concatenate([prev, t], axis=-2)

    qb = to_blocks(q)
    kk, vv = with_prev(to_blocks(k)), with_prev(to_blocks(v))
    s = jnp.einsum('bhrnqd,bhrnkd->bhrnqk', qb, kk)
    qi = jnp.arange(blk)[:, None]
    kj = jnp.arange(2 * blk)[None, :]
    dist = qi + blk - kj
    band = (dist >= 0) & (dist <= n_back)
    valid = band[None] & ((jnp.arange(nb)[:, None, None] > 0) | (kj[None] >= blk))
    s = jnp.where(valid, s, -jnp.inf)
    m = jnp.max(s, axis=-1)
    p = jnp.exp(s - m[..., None])
    den = jnp.sum(p, axis=-1)
    o = jnp.einsum('bhrnqk,bhrnkd->bhrnqd', p, vv) / den[..., None]

    def from_blocks(t):
        c = t.shape[-1]
        t = t.reshape(B, H, dilation, nb * blk, c)[:, :, :, :L]
        return t.transpose(0, 1, 3, 2, 4).reshape(B, H, S, c)

    return from_blocks(o), from_blocks(m[..., None])[..., 0], from_blocks(den[..., None])[..., 0]


def dilated_attention_mixer(q, k, v, positions):
    B, S, _ = q.shape

    def heads(t):
        return t.reshape(B, S, ATT_HEADS, ATT_DH).transpose(0, 2, 1, 3).astype(F32)

    qh = rope(heads(q), positions) * (ATT_DH ** -0.5)
    kh = rope(heads(k), positions)
    vh = heads(v)
    outs, ms, dens = [], [], []
    for window, dilation in DILATED_PATTERNS:
        o, m, d = dilated_branch(qh, kh, vh, dilation, window // dilation)
        outs.append(o)
        ms.append(m)
        dens.append(d)
    outs, ms, dens = jnp.stack(outs), jnp.stack(ms), jnp.stack(dens)
    wts = dens * jnp.exp(ms - jnp.max(ms, axis=0, keepdims=True))
    wts = wts / jnp.sum(wts, axis=0, keepdims=True)
    o = jnp.sum(wts[..., None] * outs, axis=0)
    return o.transpose(0, 2, 1, 3).reshape(B, S, ATT_WIDTH).astype(q.dtype)


def memory_cross_attention(h, mem_n, wq, wkv, wo):
    B, S, D = h.shape
    q = (h @ wq).reshape(B, S, X_HEADS, X_DH)
    k, v = jnp.split(mem_n @ wkv, 2, axis=-1)
    k = k.reshape(B, MEM_LEN, X_HEADS, X_DH)
    v = v.reshape(B, MEM_LEN, X_HEADS, X_DH)
    s = jnp.einsum('bshd,bmhd->bhsm', q, k).astype(F32) * (X_DH ** -0.5)
    p = jax.nn.softmax(s, axis=-1)
    o = jnp.einsum('bhsm,bmhd->bshd', p, v.astype(F32)).reshape(B, S, D)
    return o.astype(h.dtype) @ wo


def hierarchical_moe(h, w_rg, w_re, w_gate, w_up, w_down):
    B, S, D = h.shape
    t = h.reshape(-1, D)
    g_logits = (t @ w_rg).astype(F32)
    g_prob = jax.nn.softmax(g_logits, axis=-1)
    g_sel = jnp.argmax(g_logits, axis=-1)
    g_onehot = jax.nn.one_hot(g_sel, N_GROUPS, dtype=F32)
    g_w = jnp.sum(g_prob * g_onehot, axis=-1, keepdims=True)
    e_logits = (t @ w_re).astype(F32).reshape(-1, N_GROUPS, EXPERTS_PER_GROUP)
    e_in = jnp.einsum('nge,ng->ne', e_logits, g_onehot)
    top_v, top_i = lax.top_k(e_in, TOP_K_IN_GROUP)
    e_w = jax.nn.softmax(top_v, axis=-1) * g_w
    ids = g_sel[:, None] * EXPERTS_PER_GROUP + top_i
    gates = jnp.einsum('nk,nke->ne', e_w, jax.nn.one_hot(ids, N_EXPERTS, dtype=F32))
    hg = jnp.einsum('nd,edf->nef', t, w_gate)
    hu = jnp.einsum('nd,edf->nef', t, w_up)
    act = (jax.nn.silu(hg.astype(F32)) * hu.astype(F32) * gates[..., None]).astype(t.dtype)
    y = jnp.einsum('nef,efd->nd', act, w_down)
    return y.reshape(B, S, D).astype(h.dtype)


def setup_inputs(seed: int = 0) -> dict:
    key = jax.random.key(seed)
    ks = jax.random.split(key, 24)

    def w(k, shape, fan_in):
        return jax.random.normal(k, shape, F32) * (fan_in ** -0.5)

    def gain(k, shape):
        return 1.0 + 0.02 * jax.random.normal(k, shape, F32)

    return {
        'x': jax.random.normal(ks[0], (BATCH, SEQ, D_MODEL), F32),
        'mem': jax.random.normal(ks[1], (BATCH, MEM_LEN, D_MODEL), F32),
        'positions': (jnp.arange(SEQ, dtype=jnp.int32)[None, :]
                      + jax.random.randint(ks[2], (BATCH, 1), 0, 4096, dtype=jnp.int32)),
        'norm_mix': gain(ks[3], (DEPTH, D_MODEL)),
        'w_in': w(ks[4], (DEPTH, D_MODEL, IN_PROJ_WIDTH), D_MODEL),
        'hg_lower_bounds': 1.0 + 0.1 * jax.random.normal(ks[5], (DEPTH, HG_WIDTH), F32),
        'hg_out_norm': gain(ks[6], (DEPTH, HG_WIDTH)),
        'w_out': w(ks[7], (DEPTH, MIX_WIDTH, D_MODEL), MIX_WIDTH),
        'norm_cross': gain(ks[8], (DEPTH, D_MODEL)),
        'norm_mem': gain(ks[9], (DEPTH, D_MODEL)),
        'wq_x': w(ks[10], (DEPTH, D_MODEL, D_MODEL), D_MODEL),
        'wkv_x': w(ks[11], (DEPTH, D_MODEL, 2 * D_MODEL), D_MODEL),
        'wo_x': w(ks[12], (DEPTH, D_MODEL, D_MODEL), D_MODEL),
        'norm_ffn': gain(ks[13], (DEPTH, D_MODEL)),
        'w_router_group': w(ks[14], (DEPTH, D_MODEL, N_GROUPS), D_MODEL),
        'w_router_expert': w(ks[15], (DEPTH, D_MODEL, N_EXPERTS), D_MODEL),
        'w_gate': w(ks[16], (DEPTH, N_EXPERTS, D_MODEL, EXPERT_FF), D_MODEL),
        'w_up': w(ks[17], (DEPTH, N_EXPERTS, D_MODEL, EXPERT_FF), D_MODEL),
        'w_down': w(ks[18], (DEPTH, N_EXPERTS, EXPERT_FF, D_MODEL), EXPERT_FF),
        'norm_final': gain(ks[19], (D_MODEL,)),
    }


def reference(x, mem, positions, norm_mix, w_in, hg_lower_bounds, hg_out_norm, w_out,
              norm_cross, norm_mem, wq_x, wkv_x, wo_x, norm_ffn, w_router_group,
              w_router_expert, w_gate, w_up, w_down, norm_final):
    lb_sm = jax.nn.softmax(hg_lower_bounds.astype(F32), axis=0)
    lbs = jnp.cumsum(lb_sm, axis=0) - lb_sm[0:1]
    for l in range(DEPTH):
        h = rms_norm(x, norm_mix[l])
        proj = h @ w_in[l]
        hq, hf, hi, hgate, aq, ak, av = jnp.split(proj, IN_SPLITS, axis=-1)
        y_hg = hgrn2_mixer(hq, hf, hi, hgate, lbs[l], hg_out_norm[l])
        y_att = dilated_attention_mixer(aq, ak, av, positions)
        x = x + jnp.concatenate([y_hg, y_att], axis=-1) @ w_out[l]
        x = x + memory_cross_attention(rms_norm(x, norm_cross[l]), rms_norm(mem, norm_mem[l]),
                                       wq_x[l], wkv_x[l], wo_x[l])
        x = x + hierarchical_moe(rms_norm(x, norm_ffn[l]), w_router_group[l], w_router_expert[l],
                                 w_gate[l], w_up[l], w_down[l])
    return rms_norm(x, norm_final)
```

```python
import functools

import numpy as np
import jax
import jax.numpy as jnp
from jax import lax
from jax.experimental import pallas as pl
from jax.experimental.pallas import tpu as pltpu

F32 = jnp.float32
BF16 = jnp.bfloat16
I32 = jnp.int32

NORM_EPS = 1e-6
ROPE_THETA = 10000.0
HG_HEADS = 4
HG_DK = 128
ATT_DH = 64
ATT_BLK = 128
ATT_DILATIONS = (1, 4, 16)
ATT_SUPER = ATT_BLK * 16
X_HEADS = 4
N_GROUPS = 4
EXPERTS_PER_GROUP = 4
N_EXPERTS = 16
N_PAIRS = 6
N_BUCKETS = N_GROUPS * N_PAIRS
EXPERT_FF = 512
LANES = 128
NEG = -1e30

HG_T = 128
TOK_TILE = 512
X_TILE = 256
MOE_TILE = 256
VMEM_LIMIT = 56 * 1024 * 1024


def _dot(a, b):
    return jnp.dot(a, b, preferred_element_type=F32)


def _dot_nt(a, b):
    return lax.dot_general(a, b, (((1,), (1,)), ((), ())), preferred_element_type=F32)


def _dot_tn(a, b):
    return lax.dot_general(a, b, (((0,), (0,)), ((), ())), preferred_element_type=F32)


def _rms(x, g):
    return x * lax.rsqrt(jnp.mean(x * x, axis=-1, keepdims=True) + NORM_EPS) * g


def _params(n_axes):
    return pltpu.CompilerParams(dimension_semantics=("arbitrary",) * n_axes,
                                vmem_limit_bytes=VMEM_LIMIT)


def _rope_kernel(pos_ref, inv_ref, cos_ref, sin_ref):
    ang = pos_ref[...].astype(F32) * inv_ref[...]
    lane = lax.broadcasted_iota(I32, ang.shape, 1)
    s = jnp.sin(ang)
    cos_ref[...] = jnp.cos(ang)
    sin_ref[...] = jnp.where((lane % ATT_DH) < ATT_DH // 2, -s, s)


def rope_tables(positions):
    n = positions.size
    inv = ROPE_THETA ** (-jnp.arange(0, ATT_DH, 2, dtype=F32) / ATT_DH)
    inv = jnp.tile(inv, LANES // (ATT_DH // 2)).reshape(1, LANES)
    tm = TOK_TILE
    return pl.pallas_call(
        _rope_kernel,
        grid=(n // tm,),
        in_specs=[pl.BlockSpec((tm, 1), lambda i: (i, 0)),
                  pl.BlockSpec((1, LANES), lambda i: (0, 0))],
        out_specs=[pl.BlockSpec((tm, LANES), lambda i: (i, 0))] * 2,
        out_shape=[jax.ShapeDtypeStruct((n, LANES), F32)] * 2,
        compiler_params=_params(1),
        name="rope_tables",
    )(positions.reshape(n, 1), inv)


def _inproj_kernel(x_ref, g_ref, w_ref, cos_ref, sin_ref, hp_ref, q_ref, k_ref, v_ref):
    h = _rms(x_ref[...], g_ref[...]).astype(BF16)
    hgw = hp_ref.shape[1]
    aw = q_ref.shape[1]
    for c in range(hgw // aw):
        hp_ref[:, c * aw:(c + 1) * aw] = _dot(h, w_ref[:, c * aw:(c + 1) * aw])
    cos = cos_ref[...]
    sin = sin_ref[...]
    lane = lax.broadcasted_iota(I32, cos.shape, 1)
    first = (lane % ATT_DH) < ATT_DH // 2

    def rope(t, scale):
        for p in range(aw // LANES):
            tp = t[:, p * LANES:(p + 1) * LANES]
            rot = jnp.where(first, pltpu.roll(tp, LANES - ATT_DH // 2, axis=1),
                            pltpu.roll(tp, ATT_DH // 2, axis=1))
            yield (tp * cos + rot * sin) * scale

    aq = _dot(h, w_ref[:, hgw:hgw + aw])
    for p, blk in enumerate(rope(aq, ATT_DH ** -0.5)):
        q_ref[:, p * LANES:(p + 1) * LANES] = blk
    ak = _dot(h, w_ref[:, hgw + aw:hgw + 2 * aw])
    for p, blk in enumerate(rope(ak, 1.0)):
        k_ref[:, p * LANES:(p + 1) * LANES] = blk
    v_ref[...] = _dot(h, w_ref[:, hgw + 2 * aw:hgw + 3 * aw])


def in_projection(x2d, gain, w_in, cos, sin, hg_width, att_width):
    n, d = x2d.shape
    tm = TOK_TILE
    row = lambda i: (i, 0)
    fix = lambda i: (0, 0)
    return pl.pallas_call(
        _inproj_kernel,
        grid=(n // tm,),
        in_specs=[pl.BlockSpec((tm, d), row), pl.BlockSpec((1, d), fix),
                  pl.BlockSpec(w_in.shape, fix),
                  pl.BlockSpec((tm, LANES), row), pl.BlockSpec((tm, LANES), row)],
        out_specs=[pl.BlockSpec((tm, 4 * hg_width), row)] + [pl.BlockSpec((tm, att_width), row)] * 3,
        out_shape=[jax.ShapeDtypeStruct((n, 4 * hg_width), F32)]
        + [jax.ShapeDtypeStruct((n, att_width), F32)] * 3,
        compiler_params=_params(1),
        name="in_projection",
    )(x2d, gain.reshape(1, d), w_in, cos, sin)


def _level_table(t):
    ti = np.arange(t)[:, None]
    si = np.arange(t)[None, :]
    x = np.maximum(ti ^ si, 1)
    lvl = np.floor(np.log2(x)).astype(np.int32)
    diag = int(np.log2(t))
    return np.where(si < ti, lvl, np.where(si == ti, diag, -1)).astype(np.int32)


def _hgrn_kernel(q_ref, f_ref, i_ref, gt_ref, lb_ref, gain_ref, lvl_ref, y_ref, st_ref):
    t_rows = q_ref.shape[0]
    n_lev = t_rows.bit_length() - 1

    @pl.when(pl.program_id(1) == 0)
    def _():
        st_ref[...] = jnp.zeros_like(st_ref)

    row = lax.broadcasted_iota(I32, (t_rows, HG_DK), 0)
    lvl = lvl_ref[...]
    for h in range(HG_HEADS):
        sl = slice(h * HG_DK, (h + 1) * HG_DK)
        lb = lb_ref[:, sl]
        f = lb + (1.0 - lb) * jax.nn.sigmoid(f_ref[:, sl])
        g = jnp.log(f)
        kk = 1.0 - f
        q = q_ref[:, sl]
        v16 = i_ref[:, sl].astype(BF16)
        b = g
        s = 1
        while s < t_rows:
            b = b + jnp.where(row >= s, pltpu.roll(b, s, axis=0), 0.0)
            s *= 2
        scores = jnp.where(lvl == n_lev, _dot_nt(q.astype(BF16), kk.astype(BF16)), 0.0)
        first = b - g
        last = b
        for j in range(n_lev):
            if j > 0:
                half = 1 << (j - 1)
                bit = ((row >> (j - 1)) & 1) == 1
                first = jnp.where(bit, pltpu.roll(first, half, axis=0), first)
                last = jnp.where(bit, last, pltpu.roll(last, t_rows - half, axis=0))
            qj = (q * jnp.exp(b - first)).astype(BF16)
            kj = (kk * jnp.exp(last - b)).astype(BF16)
            scores = jnp.where(lvl == j, _dot_nt(qj, kj), scores)
        st = st_ref[h]
        o = _dot(scores.astype(BF16), v16) + _dot_nt((q * jnp.exp(b)).astype(BF16), st.astype(BF16))
        b_last = b[t_rows - 1:t_rows, :]
        kdec = (kk * jnp.exp(b_last - b)).astype(BF16)
        st_ref[h] = jnp.exp(b_last) * st + _dot_tn(v16, kdec)
        o = o * lax.rsqrt(jnp.mean(o * o, axis=-1, keepdims=True) + NORM_EPS) * gain_ref[:, sl]
        y_ref[:, sl] = o * jax.nn.silu(gt_ref[:, sl])


def hgrn_mixer(hproj, lb, out_gain, batch, seq):
    n, w4 = hproj.shape
    w = w4 // 4
    t = HG_T
    nb = seq // t
    col = lambda c: (lambda b, i: (b * nb + i, c))
    fix = lambda b, i: (0, 0)
    lvl = jnp.asarray(_level_table(t))
    return pl.pallas_call(
        _hgrn_kernel,
        grid=(batch, nb),
        in_specs=[pl.BlockSpec((t, w), col(0)), pl.BlockSpec((t, w), col(1)),
                  pl.BlockSpec((t, w), col(2)), pl.BlockSpec((t, w), col(3)),
                  pl.BlockSpec((1, w), fix), pl.BlockSpec((1, w), fix),
                  pl.BlockSpec((t, t), fix)],
        out_specs=pl.BlockSpec((t, w), col(0)),
        out_shape=jax.ShapeDtypeStruct((n, w), F32),
        scratch_shapes=[pltpu.VMEM((HG_HEADS, HG_DK, HG_DK), F32)],
        compiler_params=_params(2),
        name="hgrn_mixer",
    )(hproj, hproj, hproj, hproj, lb.reshape(1, w), out_gain.reshape(1, w), lvl)


def _att_masks():
    blk = ATT_BLK
    rows = np.arange(blk)
    nat = rows
    seg = blk // 4
    d4 = 4 * (rows % seg) + rows // seg
    out = []
    for pos in (nat, d4, nat):
        kpos = np.concatenate([pos, blk + pos])
        dist = pos[:, None] + blk - kpos[None, :]
        out.append(((dist >= 0) & (dist <= blk)).astype(np.float32))
    return np.stack(out)


def _att_kernel(q_ref, k_ref, v_ref, mask_ref, o_ref,
                qp, kp, vp, kn, vn, acc_p, m_p, l_p, acc_n, m_n, l_n):
    blk = ATT_BLK
    ncls = ATT_SUPER // blk
    sb = pl.program_id(2)
    lane = lax.broadcasted_iota(I32, (blk, LANES), 1)
    head0 = lane < ATT_DH
    col = lax.broadcasted_iota(I32, (blk, 2 * blk), 1)

    @pl.when(sb == 0)
    def _():
        zero = jnp.zeros((blk, LANES), BF16)
        for r in range(ncls):
            kp[r * 2 * blk:r * 2 * blk + blk, :] = zero
            vp[r * 2 * blk:r * 2 * blk + blk, :] = zero
        kn[0:blk, :] = zero
        vn[0:blk, :] = zero

    for r in range(ncls):
        qp[r * blk:(r + 1) * blk, :] = q_ref[pl.ds(r, blk, stride=ncls), :].astype(BF16)
        kp[r * 2 * blk + blk:(r + 1) * 2 * blk, :] = k_ref[pl.ds(r, blk, stride=ncls), :].astype(BF16)
        vp[r * 2 * blk + blk:(r + 1) * 2 * blk, :] = v_ref[pl.ds(r, blk, stride=ncls), :].astype(BF16)
    kn[blk:, :] = k_ref[...].astype(BF16)
    vn[blk:, :] = v_ref[...].astype(BF16)

    def block_attention(qb, kb, vb, valid):
        outs = []
        for hh in range(2):
            hm = head0 if hh == 0 else jnp.logical_not(head0)
            s = _dot_nt(jnp.where(hm, qb, jnp.zeros_like(qb)), kb)
            s = jnp.where(valid, s, NEG)
            m = jnp.max(s, axis=-1, keepdims=True)
            p = jnp.exp(s - m)
            l = jnp.sum(p, axis=-1, keepdims=True)
            outs.append((_dot(p.astype(BF16), vb), m, l))
        (o0, m0, l0), (o1, m1, l1) = outs
        return (jnp.where(head0, o0, o1), jnp.where(head0, m0, m1), jnp.where(head0, l0, l1))

    def first_ok(is_first):
        return jnp.logical_or(col >= blk, jnp.logical_not(is_first))

    band16 = mask_ref[2] > 0.5
    valid16 = jnp.logical_and(band16, first_ok(sb == 0))

    def body16(r, _):
        qs = pl.multiple_of(r * blk, blk)
        ks = pl.multiple_of(r * 2 * blk, 2 * blk)
        o, m, l = block_attention(qp[pl.ds(qs, blk), :], kp[pl.ds(ks, 2 * blk), :],
                                  vp[pl.ds(ks, 2 * blk), :], valid16)
        acc_p[pl.ds(qs, blk), :] = o
        m_p[pl.ds(qs, blk), :] = m
        l_p[pl.ds(qs, blk), :] = l
        return 0

    lax.fori_loop(0, ncls, body16, 0)

    band4 = mask_ref[1] > 0.5
    seg = blk // 4

    def body4(idx, _):
        c4 = idx // 4
        n = idx % 4
        valid = jnp.logical_and(band4, first_ok(jnp.logical_and(sb == 0, n == 0)))
        qoff = [pl.multiple_of((c4 + 4 * c) * blk + seg * n, seg) for c in range(4)]
        koff = [pl.multiple_of((c4 + 4 * c) * 2 * blk + blk + seg * (n - 1), seg) for c in range(4)]
        qb = jnp.concatenate([qp[pl.ds(off, seg), :] for off in qoff], axis=0)
        kb = jnp.concatenate([kp[pl.ds(off, seg), :] for off in koff]
                             + [kp[pl.ds(off + seg, seg), :] for off in koff], axis=0)
        vb = jnp.concatenate([vp[pl.ds(off, seg), :] for off in koff]
                             + [vp[pl.ds(off + seg, seg), :] for off in koff], axis=0)
        o, m, l = block_attention(qb, kb, vb, valid)
        for c, off in enumerate(qoff):
            rs = slice(c * seg, (c + 1) * seg)
            m_old = m_p[pl.ds(off, seg), :]
            m_new = jnp.maximum(m_old, m[rs])
            a_old = jnp.exp(m_old - m_new)
            a_new = jnp.exp(m[rs] - m_new)
            acc_p[pl.ds(off, seg), :] = acc_p[pl.ds(off, seg), :] * a_old + o[rs] * a_new
            l_p[pl.ds(off, seg), :] = l_p[pl.ds(off, seg), :] * a_old + l[rs] * a_new
            m_p[pl.ds(off, seg), :] = m_new
        return 0

    lax.fori_loop(0, ncls, body4, 0)

    band1 = mask_ref[0] > 0.5

    def body1(n, _):
        valid = jnp.logical_and(band1, first_ok(jnp.logical_and(sb == 0, n == 0)))
        qs = pl.multiple_of(n * blk, blk)
        o, m, l = block_attention(q_ref[pl.ds(qs, blk), :].astype(BF16), kn[pl.ds(qs, 2 * blk), :],
                                  vn[pl.ds(qs, 2 * blk), :], valid)
        acc_n[pl.ds(qs, blk), :] = o
        m_n[pl.ds(qs, blk), :] = m
        l_n[pl.ds(qs, blk), :] = l
        return 0

    lax.fori_loop(0, ncls, body1, 0)

    def combine(i, _):
        nat = pl.ds(pl.multiple_of(i * ncls, ncls), ncls)
        per = pl.ds(i, ncls, stride=blk)
        m1, m2 = m_n[nat, :], m_p[per, :]
        mm = jnp.maximum(m1, m2)
        a1 = jnp.exp(m1 - mm)
        a2 = jnp.exp(m2 - mm)
        den = l_n[nat, :] * a1 + l_p[per, :] * a2
        o_ref[nat, :] = (acc_n[nat, :] * a1 + acc_p[per, :] * a2) / den
        return 0

    lax.fori_loop(0, blk, combine, 0)

    for r in range(ncls):
        kp[r * 2 * blk:r * 2 * blk + blk, :] = kp[r * 2 * blk + blk:(r + 1) * 2 * blk, :]
        vp[r * 2 * blk:r * 2 * blk + blk, :] = vp[r * 2 * blk + blk:(r + 1) * 2 * blk, :]
    kn[0:blk, :] = kn[ATT_SUPER:ATT_SUPER + blk, :]
    vn[0:blk, :] = vn[ATT_SUPER:ATT_SUPER + blk, :]


def dilated_attention(aq, ak, av, batch, seq):
    n, w = aq.shape
    nsb = seq // ATT_SUPER
    npair = w // LANES
    blkmap = lambda b, hp, sb: (b * nsb + sb, hp)
    spec = pl.BlockSpec((ATT_SUPER, LANES), blkmap)
    masks = jnp.asarray(_att_masks())
    return pl.pallas_call(
        _att_kernel,
        grid=(batch, npair, nsb),
        in_specs=[spec, spec, spec, pl.BlockSpec(masks.shape, lambda b, hp, sb: (0, 0, 0))],
        out_specs=spec,
        out_shape=jax.ShapeDtypeStruct((n, w), F32),
        scratch_shapes=[pltpu.VMEM((ATT_SUPER, LANES), BF16),
                        pltpu.VMEM((2 * ATT_SUPER, LANES), BF16),
                        pltpu.VMEM((2 * ATT_SUPER, LANES), BF16),
                        pltpu.VMEM((ATT_SUPER + ATT_BLK, LANES), BF16),
                        pltpu.VMEM((ATT_SUPER + ATT_BLK, LANES), BF16)]
        + [pltpu.VMEM((ATT_SUPER, LANES), F32)] * 6,
        compiler_params=_params(3),
        name="dilated_attention",
    )(aq, ak, av, masks)


def _memkv_kernel(mem_ref, g_ref, w_ref, k_ref, v_ref):
    h = _rms(mem_ref[...], g_ref[...]).astype(BF16)
    d = k_ref.shape[1]
    k_ref[...] = _dot(h, w_ref[:, :d]).astype(BF16)
    v_ref[...] = _dot(h, w_ref[:, d:]).astype(BF16)


def memory_kv(mem2d, gain, wkv, batch):
    n, d = mem2d.shape
    m = n // batch
    row = lambda b: (b, 0)
    fix = lambda b: (0, 0)
    return pl.pallas_call(
        _memkv_kernel,
        grid=(batch,),
        in_specs=[pl.BlockSpec((m, d), row), pl.BlockSpec((1, d), fix), pl.BlockSpec(wkv.shape, fix)],
        out_specs=[pl.BlockSpec((m, d), row)] * 2,
        out_shape=[jax.ShapeDtypeStruct((n, d), BF16)] * 2,
        compiler_params=_params(1),
        name="memory_kv",
    )(mem2d, gain.reshape(1, d), wkv)


def _cross_kernel(x_ref, yh_ref, ya_ref, wout_ref, gx_ref, wq_ref, kx_ref, vx_ref, wo_ref,
                  gf_ref, wr_ref, xe_ref, bkt_ref, cnt_ref):
    d = x_ref.shape[1]
    hw = yh_ref.shape[1]
    x = (x_ref[...] + _dot(yh_ref[...].astype(BF16), wout_ref[:hw, :])
         + _dot(ya_ref[...].astype(BF16), wout_ref[hw:, :]))
    h = _rms(x, gx_ref[...]).astype(BF16)
    q = _dot(h, wq_ref[...])
    dh = d // X_HEADS
    heads = []
    for hh in range(X_HEADS):
        sl = slice(hh * dh, (hh + 1) * dh)
        s = _dot_nt(q[:, sl].astype(BF16), kx_ref[:, sl]) * (dh ** -0.5)
        s = s - jnp.max(s, axis=-1, keepdims=True)
        p = jnp.exp(s)
        p = p / jnp.sum(p, axis=-1, keepdims=True)
        heads.append(_dot(p.astype(BF16), vx_ref[:, sl]).astype(BF16))
    x = x + _dot(jnp.concatenate(heads, axis=1), wo_ref[...])
    xe_ref[:, :d] = x

    hf = _rms(x, gf_ref[...])
    logits = jnp.dot(hf, wr_ref[...], preferred_element_type=F32, precision=lax.Precision.HIGHEST)
    lane = lax.broadcasted_iota(I32, logits.shape, 1)
    big = 1 << 20

    def first_max(vals):
        top = jnp.max(vals, axis=-1, keepdims=True)
        return top, jnp.min(jnp.where(vals == top, lane, big), axis=-1, keepdims=True)

    is_g = lane < N_GROUPS
    g_top, g_sel = first_max(jnp.where(is_g, logits, NEG))
    g_w = 1.0 / jnp.sum(jnp.where(is_g, jnp.exp(logits - g_top), 0.0), axis=-1, keepdims=True)
    e_lo = N_GROUPS + EXPERTS_PER_GROUP * g_sel
    in_grp = jnp.logical_and(lane >= e_lo, lane < e_lo + EXPERTS_PER_GROUP)
    e_log = jnp.where(in_grp, logits, NEG)
    v1, i1 = first_max(e_log)
    v2, i2 = first_max(jnp.where(lane == i1, NEG, e_log))
    t = jnp.exp(v2 - v1)
    w1 = g_w / (1.0 + t)
    w2 = g_w * t / (1.0 + t)
    j1 = i1 - e_lo
    j2 = i2 - e_lo
    lo = jnp.minimum(j1, j2)
    hi = jnp.maximum(j1, j2)
    pair = jnp.where(lo == 0, hi - 1, jnp.where(lo == 1, hi + 1, N_PAIRS - 1))
    bucket = g_sel * N_PAIRS + pair
    w_lo = jnp.where(j1 < j2, w1, w2)
    w_hi = jnp.where(j1 < j2, w2, w1)
    xe_ref[:, d:] = jnp.where(lane == 0, w_lo, jnp.where(lane == 1, w_hi, 0.0))
    bkt_ref[...] = bucket
    cnt_ref[0] = jnp.sum((lane == bucket).astype(F32), axis=0, keepdims=True)


def cross_block(x2d, y_hg, y_att, w_out, g_cross, wq, kx, vx, wo, g_ffn, w_router, batch):
    n, d = x2d.shape
    tm = X_TILE
    nt = n // tm
    per_batch = nt // batch
    m = kx.shape[0] // batch
    row = lambda i: (i, 0)
    fix = lambda i: (0, 0)
    mem = lambda i: (i // per_batch, 0)
    return pl.pallas_call(
        _cross_kernel,
        grid=(nt,),
        in_specs=[pl.BlockSpec((tm, d), row), pl.BlockSpec((tm, y_hg.shape[1]), row),
                  pl.BlockSpec((tm, y_att.shape[1]), row), pl.BlockSpec(w_out.shape, fix),
                  pl.BlockSpec((1, d), fix), pl.BlockSpec(wq.shape, fix),
                  pl.BlockSpec((m, d), mem), pl.BlockSpec((m, d), mem), pl.BlockSpec(wo.shape, fix),
                  pl.BlockSpec((1, d), fix), pl.BlockSpec(w_router.shape, fix)],
        out_specs=[pl.BlockSpec((tm, d + LANES), row), pl.BlockSpec((tm, 1), row),
                   pl.BlockSpec((1, 1, LANES), lambda i: (i, 0, 0))],
        out_shape=[jax.ShapeDtypeStruct((n, d + LANES), F32), jax.ShapeDtypeStruct((n, 1), I32),
                   jax.ShapeDtypeStruct((nt, 1, LANES), F32)],
        compiler_params=_params(1),
        name="cross_block",
    )(x2d, y_hg, y_att, w_out, g_cross.reshape(1, d), wq, kx, vx, wo, g_ffn.reshape(1, d), w_router)


def _position_kernel(bkt_ref, base_ref, tri_ref, pos_ref):
    lane = lax.broadcasted_iota(I32, (bkt_ref.shape[0], LANES), 1)
    onehot = (lane == bkt_ref[...]).astype(F32)
    before = _dot(tri_ref[...], onehot.astype(BF16))
    pos = jnp.sum(onehot * (before + base_ref[0]), axis=-1, keepdims=True)
    pos_ref[...] = pos.astype(I32)


def sorted_positions(bucket, base):
    n = bucket.shape[0]
    nt = base.shape[0]
    tm = n // nt
    tri = jnp.asarray(np.tril(np.ones((tm, tm), np.float32), -1), BF16)
    return pl.pallas_call(
        _position_kernel,
        grid=(nt,),
        in_specs=[pl.BlockSpec((tm, 1), lambda i: (i, 0)),
                  pl.BlockSpec((1, 1, LANES), lambda i: (i, 0, 0)),
                  pl.BlockSpec((tm, tm), lambda i: (0, 0))],
        out_specs=pl.BlockSpec((tm, 1), lambda i: (i, 0)),
        out_shape=jax.ShapeDtypeStruct((n, 1), I32),
        compiler_params=_params(1),
        name="sorted_positions",
    )(bucket, base, tri)


def _moe_kernel(pos_ref, ea_ref, eb_ref, cnt_ref,
                x_hbm, g_ref, wga_ref, wgb_ref, wua_ref, wub_ref, wda_ref, wdb_ref, gfin_ref,
                out_hbm, src_ref, xbuf, obuf, gsem, ssem, *, final_norm):
    j = pl.program_id(0)
    nt = pl.num_programs(0)
    rows = xbuf.shape[1]
    d = obuf.shape[2]
    slot = j % 2

    def gather_copy(tile, sl, r):
        tok = src_ref[tile * rows + r]
        return pltpu.make_async_copy(x_hbm.at[pl.ds(tok, 1)], xbuf.at[sl, pl.ds(r, 1)], gsem.at[sl])

    def scatter_copy(tile, sl, r):
        tok = src_ref[tile * rows + r]
        return pltpu.make_async_copy(obuf.at[sl, pl.ds(r, 1)], out_hbm.at[pl.ds(tok, 1)], ssem.at[sl])

    def for_rows(tile, fn):
        def body(r, _):
            fn(r)
            return 0
        lax.fori_loop(0, cnt_ref[tile], body, 0)

    @pl.when(j == 0)
    def _():
        def fill(t, _):
            src_ref[pos_ref[t]] = t
            return 0
        lax.fori_loop(0, pos_ref.shape[0], fill, 0)
        xbuf[...] = jnp.zeros_like(xbuf)
        for_rows(0, lambda r: gather_copy(0, 0, r).start())

    @pl.when(j + 1 < nt)
    def _():
        for_rows(j + 1, lambda r: gather_copy(j + 1, 1 - slot, r).start())

    for_rows(j, lambda r: gather_copy(j, slot, r).wait())

    @pl.when(j >= 2)
    def _():
        for_rows(j - 2, lambda r: scatter_copy(j - 2, slot, r).wait())

    @pl.when(cnt_ref[j] > 0)
    def _():
        xe = xbuf[slot]
        x = xe[:, :d]
        h = _rms(x, g_ref[...]).astype(BF16)
        lane = lax.broadcasted_iota(I32, (rows, LANES), 1)
        wts = xe[:, d:]
        w_a = jnp.sum(jnp.where(lane == 0, wts, 0.0), axis=-1, keepdims=True)
        w_b = jnp.sum(jnp.where(lane == 1, wts, 0.0), axis=-1, keepdims=True)
        act_a = (jax.nn.silu(_dot(h, wga_ref[...])) * _dot(h, wua_ref[...]) * w_a).astype(BF16)
        act_b = (jax.nn.silu(_dot(h, wgb_ref[...])) * _dot(h, wub_ref[...]) * w_b).astype(BF16)
        y = x + _dot(act_a, wda_ref[...]) + _dot(act_b, wdb_ref[...])
        if final_norm:
            y = _rms(y, gfin_ref[...])
        obuf[slot] = y
        for_rows(j, lambda r: scatter_copy(j, slot, r).start())

    @pl.when(j == nt - 1)
    def _():
        @pl.when(j >= 1)
        def _():
            for_rows(j - 1, lambda r: scatter_copy(j - 1, 1 - slot, r).wait())
        for_rows(j, lambda r: scatter_copy(j, slot, r).wait())


def moe_ffn(x_ext, pos, tile_a, tile_b, tile_cnt, g_ffn, w_gate, w_up, w_down, g_final, final_norm):
    n, de = x_ext.shape
    d = de - LANES
    rows = MOE_TILE
    nt = tile_cnt.shape[0]
    ff = w_gate.shape[2]
    fix = lambda j, *_: (0, 0)
    exp_a = lambda j, pos, ea, eb, cnt: (ea[j], 0, 0)
    exp_b = lambda j, pos, ea, eb, cnt: (eb[j], 0, 0)
    up_spec = lambda im: pl.BlockSpec((None, d, ff), im)
    down_spec = lambda im: pl.BlockSpec((None, ff, d), im)
    grid_spec = pltpu.PrefetchScalarGridSpec(
        num_scalar_prefetch=4,
        grid=(nt,),
        in_specs=[pl.BlockSpec(memory_space=pl.ANY), pl.BlockSpec((1, d), fix),
                  up_spec(exp_a), up_spec(exp_b), up_spec(exp_a), up_spec(exp_b),
                  down_spec(exp_a), down_spec(exp_b), pl.BlockSpec((1, d), fix)],
        out_specs=pl.BlockSpec(memory_space=pl.ANY),
        scratch_shapes=[pltpu.SMEM((nt * rows,), I32),
                        pltpu.VMEM((2, rows, de), F32), pltpu.VMEM((2, rows, d), F32),
                        pltpu.SemaphoreType.DMA((2,)), pltpu.SemaphoreType.DMA((2,))],
    )
    return pl.pallas_call(
        functools.partial(_moe_kernel, final_norm=final_norm),
        grid_spec=grid_spec,
        out_shape=jax.ShapeDtypeStruct((n, d), F32),
        compiler_params=_params(1),
        name="moe_ffn",
    )(pos, tile_a, tile_b, tile_cnt, x_ext, g_ffn.reshape(1, d),
      w_gate, w_gate, w_up, w_up, w_down, w_down, g_final.reshape(1, d))


_PAIR_LO = np.array([0, 0, 0, 1, 1, 2], np.int32)
_PAIR_HI = np.array([1, 2, 3, 2, 3, 3], np.int32)


def _tile_plan(counts, n_tokens):
    rows = MOE_TILE
    counts = counts.astype(I32)
    total = jnp.sum(counts, axis=0)
    padded = (total + rows - 1) // rows * rows
    start = jnp.cumsum(padded) - padded
    base = start[None, :] + jnp.cumsum(counts, axis=0) - counts
    nt = n_tokens // rows + N_BUCKETS
    first_row = jnp.arange(nt, dtype=I32) * rows
    end = (start + padded)[:N_BUCKETS]
    bucket = jnp.sum((first_row[:, None] >= end[None, :]).astype(I32), axis=1)
    used = bucket < N_BUCKETS
    last_used = jnp.max(jnp.where(used, bucket, 0))
    b_eff = jnp.where(used, bucket, last_used)
    cnt = jnp.where(used, jnp.clip(total[b_eff] - (first_row - start[b_eff]), 0, rows), 0)
    grp = b_eff // N_PAIRS
    pair = b_eff % N_PAIRS
    exp_a = grp * EXPERTS_PER_GROUP + jnp.asarray(_PAIR_LO)[pair]
    exp_b = grp * EXPERTS_PER_GROUP + jnp.asarray(_PAIR_HI)[pair]
    return base.astype(F32), exp_a.astype(I32), exp_b.astype(I32), cnt.astype(I32)


def kernel(x, mem, positions, norm_mix, w_in, hg_lower_bounds, hg_out_norm, w_out, norm_cross, norm_mem,
           wq_x, wkv_x, wo_x, norm_ffn, w_router_group, w_router_expert, w_gate, w_up, w_down, norm_final):
    batch, seq, d = x.shape
    depth = w_in.shape[0]
    n = batch * seq
    hg_width = hg_lower_bounds.shape[1]
    att_width = (w_in.shape[2] - 4 * hg_width) // 3
    assert hg_width == HG_HEADS * HG_DK and seq % ATT_SUPER == 0 and n % TOK_TILE == 0

    lb_sm = jax.nn.softmax(hg_lower_bounds.astype(F32), axis=0)
    lbs = jnp.cumsum(lb_sm, axis=0) - lb_sm[0:1]
    cos, sin = rope_tables(positions)
    xs = x.reshape(n, d)
    mem2d = mem.reshape(-1, d)
    w_router = jnp.concatenate([w_router_group, w_router_expert], axis=-1)
    w_router = jnp.pad(w_router, ((0, 0), (0, 0), (0, LANES - w_router.shape[-1])))

    for l in range(depth):
        hproj, aq, ak, av = in_projection(xs, norm_mix[l], w_in[l].astype(BF16), cos, sin, hg_width, att_width)
        y_hg = hgrn_mixer(hproj, lbs[l], hg_out_norm[l], batch, seq)
        y_att = dilated_attention(aq, ak, av, batch, seq)
        kx, vx = memory_kv(mem2d, norm_mem[l], wkv_x[l].astype(BF16), batch)
        x_ext, bucket, counts = cross_block(xs, y_hg, y_att, w_out[l].astype(BF16), norm_cross[l],
                                            wq_x[l].astype(BF16), kx, vx, wo_x[l].astype(BF16),
                                            norm_ffn[l], w_router[l], batch)
        base, exp_a, exp_b, cnt = _tile_plan(counts.reshape(counts.shape[0], LANES), n)
        pos = sorted_positions(bucket, base.reshape(base.shape[0], 1, LANES))
        xs = moe_ffn(x_ext, pos.reshape(n), exp_a, exp_b, cnt, norm_ffn[l],
                     w_gate[l].astype(BF16), w_up[l].astype(BF16), w_down[l].astype(BF16),
                     norm_final, final_norm=(l == depth - 1))
    return xs.reshape(batch, seq, d)
```

```python
import functools

import numpy as np
import jax
import jax.numpy as jnp
from jax import lax
from jax.experimental import pallas as pl
from jax.experimental.pallas import tpu as pltpu

F32 = jnp.float32
BF16 = jnp.bfloat16
I32 = jnp.int32

NORM_EPS = 1e-6
ROPE_THETA = 10000.0
HG_HEADS = 4
HG_DK = 128
ATT_DH = 64
ATT_BLK = 128
ATT_DILATIONS = (1, 4, 16)
ATT_SUPER = ATT_BLK * 16
X_HEADS = 4
N_GROUPS = 4
EXPERTS_PER_GROUP = 4
N_EXPERTS = 16
N_PAIRS = 6
N_BUCKETS = N_GROUPS * N_PAIRS
EXPERT_FF = 512
LANES = 128
SUBLANES = 8
NEG = -1e30

HG_T = 128
TOK_TILE = 512
X_TILE = 256
MOE_TILE = 256
ROW_GROUP = 4
VMEM_LIMIT = 56 * 1024 * 1024


def _dot(a, b):
    return jnp.dot(a, b, preferred_element_type=F32)


def _dot_nt(a, b):
    return lax.dot_general(a, b, (((1,), (1,)), ((), ())), preferred_element_type=F32)


def _dot_tn(a, b):
    return lax.dot_general(a, b, (((0,), (0,)), ((), ())), preferred_element_type=F32)


def _rms(x, g):
    return x * lax.rsqrt(jnp.mean(x * x, axis=-1, keepdims=True) + NORM_EPS) * g


def _split_bf16(a):
    hi = a.astype(BF16)
    return hi, (a - hi.astype(F32)).astype(BF16)


def _params(n_axes):
    return pltpu.CompilerParams(dimension_semantics=("arbitrary",) * n_axes,
                                vmem_limit_bytes=VMEM_LIMIT)


def _rope_kernel(pos_ref, inv_ref, cos_ref, sin_ref):
    ang = pos_ref[...].astype(F32) * inv_ref[...]
    lane = lax.broadcasted_iota(I32, ang.shape, 1)
    s = jnp.sin(ang)
    cos_ref[...] = jnp.cos(ang)
    sin_ref[...] = jnp.where((lane % ATT_DH) < ATT_DH // 2, -s, s)


def rope_tables(positions):
    n = positions.size
    inv = ROPE_THETA ** (-jnp.arange(0, ATT_DH, 2, dtype=F32) / ATT_DH)
    inv = jnp.tile(inv, LANES // (ATT_DH // 2)).reshape(1, LANES)
    tm = TOK_TILE
    return pl.pallas_call(
        _rope_kernel,
        grid=(n // tm,),
        in_specs=[pl.BlockSpec((tm, 1), lambda i: (i, 0)),
                  pl.BlockSpec((1, LANES), lambda i: (0, 0))],
        out_specs=[pl.BlockSpec((tm, LANES), lambda i: (i, 0))] * 2,
        out_shape=[jax.ShapeDtypeStruct((n, LANES), F32)] * 2,
        compiler_params=_params(1),
        name="rope_tables",
    )(positions.reshape(n, 1), inv)


def _inproj_kernel(x_ref, g_ref, w_ref, cos_ref, sin_ref, hp_ref, q_ref, k_ref, v_ref):
    h = _rms(x_ref[...], g_ref[...]).astype(BF16)
    hgw = hp_ref.shape[1]
    aw = q_ref.shape[1]
    for c in range(hgw // aw):
        hp_ref[:, c * aw:(c + 1) * aw] = _dot(h, w_ref[:, c * aw:(c + 1) * aw])
    cos = cos_ref[...]
    sin = sin_ref[...]
    lane = lax.broadcasted_iota(I32, cos.shape, 1)
    first = (lane % ATT_DH) < ATT_DH // 2

    def rope(t, scale):
        for p in range(aw // LANES):
            tp = t[:, p * LANES:(p + 1) * LANES]
            rot = jnp.where(first, pltpu.roll(tp, LANES - ATT_DH // 2, axis=1),
                            pltpu.roll(tp, ATT_DH // 2, axis=1))
            yield (tp * cos + rot * sin) * scale

    aq = _dot(h, w_ref[:, hgw:hgw + aw])
    for p, blk in enumerate(rope(aq, ATT_DH ** -0.5)):
        q_ref[:, p * LANES:(p + 1) * LANES] = blk
    ak = _dot(h, w_ref[:, hgw + aw:hgw + 2 * aw])
    for p, blk in enumerate(rope(ak, 1.0)):
        k_ref[:, p * LANES:(p + 1) * LANES] = blk
    v_ref[...] = _dot(h, w_ref[:, hgw + 2 * aw:hgw + 3 * aw])


def in_projection(x2d, gain, w_in, cos, sin, hg_width, att_width):
    n, d = x2d.shape
    tm = TOK_TILE
    row = lambda i: (i, 0)
    fix = lambda i: (0, 0)
    return pl.pallas_call(
        _inproj_kernel,
        grid=(n // tm,),
        in_specs=[pl.BlockSpec((tm, d), row), pl.BlockSpec((1, d), fix),
                  pl.BlockSpec(w_in.shape, fix),
                  pl.BlockSpec((tm, LANES), row), pl.BlockSpec((tm, LANES), row)],
        out_specs=[pl.BlockSpec((tm, 4 * hg_width), row)] + [pl.BlockSpec((tm, att_width), row)] * 3,
        out_shape=[jax.ShapeDtypeStruct((n, 4 * hg_width), F32)]
        + [jax.ShapeDtypeStruct((n, att_width), F32)] * 3,
        compiler_params=_params(1),
        name="in_projection",
    )(x2d, gain.reshape(1, d), w_in, cos, sin)


def _level_table(t):
    ti = np.arange(t)[:, None]
    si = np.arange(t)[None, :]
    x = np.maximum(ti ^ si, 1)
    lvl = np.floor(np.log2(x)).astype(np.int32)
    diag = int(np.log2(t))
    return np.where(si < ti, lvl, np.where(si == ti, diag, -1)).astype(np.int32)


def _hgrn_kernel(q_ref, f_ref, i_ref, gt_ref, lb_ref, gain_ref, lvl_ref, y_ref, st_ref):
    t_rows = q_ref.shape[0]
    n_lev = t_rows.bit_length() - 1

    @pl.when(pl.program_id(1) == 0)
    def _():
        st_ref[...] = jnp.zeros_like(st_ref)

    row = lax.broadcasted_iota(I32, (t_rows, HG_DK), 0)
    lvl = lvl_ref[...]
    for h in range(HG_HEADS):
        sl = slice(h * HG_DK, (h + 1) * HG_DK)
        lb = lb_ref[:, sl]
        f = lb + (1.0 - lb) * jax.nn.sigmoid(f_ref[:, sl])
        g = jnp.log(f)
        kk = 1.0 - f
        q = q_ref[:, sl]
        v16 = i_ref[:, sl].astype(BF16)
        b = g
        s = 1
        while s < t_rows:
            b = b + jnp.where(row >= s, pltpu.roll(b, s, axis=0), 0.0)
            s *= 2
        scores = jnp.where(lvl == n_lev, _dot_nt(q.astype(BF16), kk.astype(BF16)), 0.0)
        first = b - g
        last = b
        for j in range(n_lev):
            if j > 0:
                half = 1 << (j - 1)
                bit = ((row >> (j - 1)) & 1) == 1
                first = jnp.where(bit, pltpu.roll(first, half, axis=0), first)
                last = jnp.where(bit, last, pltpu.roll(last, t_rows - half, axis=0))
            qj = (q * jnp.exp(b - first)).astype(BF16)
            kj = (kk * jnp.exp(last - b)).astype(BF16)
            scores = jnp.where(lvl == j, _dot_nt(qj, kj), scores)
        st = st_ref[h]
        o = _dot(scores.astype(BF16), v16) + _dot_nt((q * jnp.exp(b)).astype(BF16), st.astype(BF16))
        b_last = b[t_rows - 1:t_rows, :]
        kdec = (kk * jnp.exp(b_last - b)).astype(BF16)
        st_ref[h] = jnp.exp(b_last) * st + _dot_tn(v16, kdec)
        o = o * lax.rsqrt(jnp.mean(o * o, axis=-1, keepdims=True) + NORM_EPS) * gain_ref[:, sl]
        y_ref[:, sl] = o * jax.nn.silu(gt_ref[:, sl])


def hgrn_mixer(hproj, lb, out_gain, batch, seq):
    n, w4 = hproj.shape
    w = w4 // 4
    t = HG_T
    nb = seq // t
    col = lambda c: (lambda b, i: (b * nb + i, c))
    fix = lambda b, i: (0, 0)
    lvl = jnp.asarray(_level_table(t))
    return pl.pallas_call(
        _hgrn_kernel,
        grid=(batch, nb),
        in_specs=[pl.BlockSpec((t, w), col(0)), pl.BlockSpec((t, w), col(1)),
                  pl.BlockSpec((t, w), col(2)), pl.BlockSpec((t, w), col(3)),
                  pl.BlockSpec((1, w), fix), pl.BlockSpec((1, w), fix),
                  pl.BlockSpec((t, t), fix)],
        out_specs=pl.BlockSpec((t, w), col(0)),
        out_shape=jax.ShapeDtypeStruct((n, w), F32),
        scratch_shapes=[pltpu.VMEM((HG_HEADS, HG_DK, HG_DK), F32)],
        compiler_params=_params(2),
        name="hgrn_mixer",
    )(hproj, hproj, hproj, hproj, lb.reshape(1, w), out_gain.reshape(1, w), lvl)


def _att_masks():
    blk = ATT_BLK
    rows = np.arange(blk)
    nat = rows
    seg = blk // 4
    d4 = 4 * (rows % seg) + rows // seg
    out = []
    for pos in (nat, d4, nat):
        kpos = np.concatenate([pos, blk + pos])
        dist = pos[:, None] + blk - kpos[None, :]
        out.append(((dist >= 0) & (dist <= blk)).astype(np.float32))
    return np.stack(out)


def _att_kernel(q_ref, k_ref, v_ref, mask_ref, o_ref,
                qp, kp, vp, kn, vn, acc_p, m_p, l_p, acc_n, m_n, l_n):
    blk = ATT_BLK
    ncls = ATT_SUPER // blk
    sb = pl.program_id(2)
    lane = lax.broadcasted_iota(I32, (blk, LANES), 1)
    head0 = lane < ATT_DH
    col = lax.broadcasted_iota(I32, (blk, 2 * blk), 1)

    @pl.when(sb == 0)
    def _():
        zero = jnp.zeros((blk, LANES), BF16)
        for r in range(ncls):
            kp[r * 2 * blk:r * 2 * blk + blk, :] = zero
            vp[r * 2 * blk:r * 2 * blk + blk, :] = zero
        kn[0:blk, :] = zero
        vn[0:blk, :] = zero

    for r in range(ncls):
        qp[r * blk:(r + 1) * blk, :] = q_ref[pl.ds(r, blk, stride=ncls), :].astype(BF16)
        kp[r * 2 * blk + blk:(r + 1) * 2 * blk, :] = k_ref[pl.ds(r, blk, stride=ncls), :].astype(BF16)
        vp[r * 2 * blk + blk:(r + 1) * 2 * blk, :] = v_ref[pl.ds(r, blk, stride=ncls), :].astype(BF16)
    kn[blk:, :] = k_ref[...].astype(BF16)
    vn[blk:, :] = v_ref[...].astype(BF16)

    def block_attention(qb, kb, vb, valid):
        outs = []
        for hh in range(2):
            hm = head0 if hh == 0 else jnp.logical_not(head0)
            s = _dot_nt(jnp.where(hm, qb, jnp.zeros_like(qb)), kb)
            s = jnp.where(valid, s, NEG)
            m = jnp.max(s, axis=-1, keepdims=True)
            p = jnp.exp(s - m)
            l = jnp.sum(p, axis=-1, keepdims=True)
            outs.append((_dot(p.astype(BF16), vb), m, l))
        (o0, m0, l0), (o1, m1, l1) = outs
        return (jnp.where(head0, o0, o1), jnp.where(head0, m0, m1), jnp.where(head0, l0, l1))

    def first_ok(is_first):
        return jnp.logical_or(col >= blk, jnp.logical_not(is_first))

    band16 = mask_ref[2] > 0.5
    valid16 = jnp.logical_and(band16, first_ok(sb == 0))

    def body16(r, _):
        qs = pl.multiple_of(r * blk, blk)
        ks = pl.multiple_of(r * 2 * blk, 2 * blk)
        o, m, l = block_attention(qp[pl.ds(qs, blk), :], kp[pl.ds(ks, 2 * blk), :],
                                  vp[pl.ds(ks, 2 * blk), :], valid16)
        acc_p[pl.ds(qs, blk), :] = o
        m_p[pl.ds(qs, blk), :] = m
        l_p[pl.ds(qs, blk), :] = l
        return 0

    lax.fori_loop(0, ncls, body16, 0, unroll=8)

    band4 = mask_ref[1] > 0.5
    seg = blk // 4

    def body4(idx, _):
        c4 = idx // 4
        n = idx % 4
        valid = jnp.logical_and(band4, first_ok(jnp.logical_and(sb == 0, n == 0)))
        qoff = [pl.multiple_of((c4 + 4 * c) * blk + seg * n, seg) for c in range(4)]
        koff = [pl.multiple_of((c4 + 4 * c) * 2 * blk + blk + seg * (n - 1), seg) for c in range(4)]
        qb = jnp.concatenate([qp[pl.ds(off, seg), :] for off in qoff], axis=0)
        kb = jnp.concatenate([kp[pl.ds(off, seg), :] for off in koff]
                             + [kp[pl.ds(off + seg, seg), :] for off in koff], axis=0)
        vb = jnp.concatenate([vp[pl.ds(off, seg), :] for off in koff]
                             + [vp[pl.ds(off + seg, seg), :] for off in koff], axis=0)
        o, m, l = block_attention(qb, kb, vb, valid)
        for c, off in enumerate(qoff):
            rs = slice(c * seg, (c + 1) * seg)
            m_old = m_p[pl.ds(off, seg), :]
            m_new = jnp.maximum(m_old, m[rs])
            a_old = jnp.exp(m_old - m_new)
            a_new = jnp.exp(m[rs] - m_new)
            acc_p[pl.ds(off, seg), :] = acc_p[pl.ds(off, seg), :] * a_old + o[rs] * a_new
            l_p[pl.ds(off, seg), :] = l_p[pl.ds(off, seg), :] * a_old + l[rs] * a_new
            m_p[pl.ds(off, seg), :] = m_new
        return 0

    lax.fori_loop(0, ncls, body4, 0, unroll=8)

    band1 = mask_ref[0] > 0.5

    def body1(n, _):
        valid = jnp.logical_and(band1, first_ok(jnp.logical_and(sb == 0, n == 0)))
        qs = pl.multiple_of(n * blk, blk)
        o, m, l = block_attention(q_ref[pl.ds(qs, blk), :].astype(BF16), kn[pl.ds(qs, 2 * blk), :],
                                  vn[pl.ds(qs, 2 * blk), :], valid)
        acc_n[pl.ds(qs, blk), :] = o
        m_n[pl.ds(qs, blk), :] = m
        l_n[pl.ds(qs, blk), :] = l
        return 0

    lax.fori_loop(0, ncls, body1, 0, unroll=8)

    def combine(i, _):
        nat = pl.ds(pl.multiple_of(i * ncls, ncls), ncls)
        per = pl.ds(i, ncls, stride=blk)
        m1, m2 = m_n[nat, :], m_p[per, :]
        mm = jnp.maximum(m1, m2)
        a1 = jnp.exp(m1 - mm)
        a2 = jnp.exp(m2 - mm)
        den = l_n[nat, :] * a1 + l_p[per, :] * a2
        o_ref[nat, :] = (acc_n[nat, :] * a1 + acc_p[per, :] * a2) / den
        return 0

    lax.fori_loop(0, blk, combine, 0, unroll=8)

    for r in range(ncls):
        kp[r * 2 * blk:r * 2 * blk + blk, :] = kp[r * 2 * blk + blk:(r + 1) * 2 * blk, :]
        vp[r * 2 * blk:r * 2 * blk + blk, :] = vp[r * 2 * blk + blk:(r + 1) * 2 * blk, :]
    kn[0:blk, :] = kn[ATT_SUPER:ATT_SUPER + blk, :]
    vn[0:blk, :] = vn[ATT_SUPER:ATT_SUPER + blk, :]


def dilated_attention(aq, ak, av, batch, seq):
    n, w = aq.shape
    nsb = seq // ATT_SUPER
    npair = w // LANES
    blkmap = lambda b, hp, sb: (b * nsb + sb, hp)
    spec = pl.BlockSpec((ATT_SUPER, LANES), blkmap)
    masks = jnp.asarray(_att_masks())
    return pl.pallas_call(
        _att_kernel,
        grid=(batch, npair, nsb),
        in_specs=[spec, spec, spec, pl.BlockSpec(masks.shape, lambda b, hp, sb: (0, 0, 0))],
        out_specs=spec,
        out_shape=jax.ShapeDtypeStruct((n, w), F32),
        scratch_shapes=[pltpu.VMEM((ATT_SUPER, LANES), BF16),
                        pltpu.VMEM((2 * ATT_SUPER, LANES), BF16),
                        pltpu.VMEM((2 * ATT_SUPER, LANES), BF16),
                        pltpu.VMEM((ATT_SUPER + ATT_BLK, LANES), BF16),
                        pltpu.VMEM((ATT_SUPER + ATT_BLK, LANES), BF16)]
        + [pltpu.VMEM((ATT_SUPER, LANES), F32)] * 6,
        compiler_params=_params(3),
        name="dilated_attention",
    )(aq, ak, av, masks)


def _memkv_kernel(mem_ref, g_ref, w_ref, k_ref, v_ref):
    h = _rms(mem_ref[...], g_ref[...]).astype(BF16)
    d = k_ref.shape[1]
    k_ref[...] = _dot(h, w_ref[:, :d]).astype(BF16)
    v_ref[...] = _dot(h, w_ref[:, d:]).astype(BF16)


def memory_kv(mem2d, gain, wkv, batch):
    n, d = mem2d.shape
    m = n // batch
    row = lambda b: (b, 0)
    fix = lambda b: (0, 0)
    return pl.pallas_call(
        _memkv_kernel,
        grid=(batch,),
        in_specs=[pl.BlockSpec((m, d), row), pl.BlockSpec((1, d), fix), pl.BlockSpec(wkv.shape, fix)],
        out_specs=[pl.BlockSpec((m, d), row)] * 2,
        out_shape=[jax.ShapeDtypeStruct((n, d), BF16)] * 2,
        compiler_params=_params(1),
        name="memory_kv",
    )(mem2d, gain.reshape(1, d), wkv)


def _cross_kernel(x_ref, yh_ref, ya_ref, wout_ref, gx_ref, wq_ref, kx_ref, vx_ref, wo_ref,
                  gf_ref, wr_ref, xe_ref, bkt_ref, cnt_ref):
    d = x_ref.shape[1]
    hw = yh_ref.shape[1]
    x = (x_ref[...] + _dot(yh_ref[...].astype(BF16), wout_ref[:hw, :])
         + _dot(ya_ref[...].astype(BF16), wout_ref[hw:, :]))
    h = _rms(x, gx_ref[...]).astype(BF16)
    q = _dot(h, wq_ref[...])
    dh = d // X_HEADS
    heads = []
    for hh in range(X_HEADS):
        sl = slice(hh * dh, (hh + 1) * dh)
        s = _dot_nt(q[:, sl].astype(BF16), kx_ref[:, sl]) * (dh ** -0.5)
        s = s - jnp.max(s, axis=-1, keepdims=True)
        p = jnp.exp(s)
        p = p / jnp.sum(p, axis=-1, keepdims=True)
        heads.append(_dot(p.astype(BF16), vx_ref[:, sl]).astype(BF16))
    x = x + _dot(jnp.concatenate(heads, axis=1), wo_ref[...])
    xe_ref[:, :d] = x

    hf = _rms(x, gf_ref[...])
    h_hi, h_lo = _split_bf16(hf)
    w_hi, w_lo = _split_bf16(wr_ref[...])
    both = _dot(h_hi, jnp.concatenate([w_hi, w_lo], axis=1))
    logits = both[:, :LANES] + both[:, LANES:] + _dot(h_lo, w_hi)
    lane = lax.broadcasted_iota(I32, logits.shape, 1)
    big = 1 << 20

    def first_max(vals):
        top = jnp.max(vals, axis=-1, keepdims=True)
        return top, jnp.min(jnp.where(vals == top, lane, big), axis=-1, keepdims=True)

    is_g = lane < N_GROUPS
    g_top, g_sel = first_max(jnp.where(is_g, logits, NEG))
    g_w = 1.0 / jnp.sum(jnp.where(is_g, jnp.exp(logits - g_top), 0.0), axis=-1, keepdims=True)
    e_lo = N_GROUPS + EXPERTS_PER_GROUP * g_sel
    in_grp = jnp.logical_and(lane >= e_lo, lane < e_lo + EXPERTS_PER_GROUP)
    e_log = jnp.where(in_grp, logits, NEG)
    v1, i1 = first_max(e_log)
    v2, i2 = first_max(jnp.where(lane == i1, NEG, e_log))
    t = jnp.exp(v2 - v1)
    w1 = g_w / (1.0 + t)
    w2 = g_w * t / (1.0 + t)
    j1 = i1 - e_lo
    j2 = i2 - e_lo
    lo = jnp.minimum(j1, j2)
    hi = jnp.maximum(j1, j2)
    pair = jnp.where(lo == 0, hi - 1, jnp.where(lo == 1, hi + 1, N_PAIRS - 1))
    bucket = g_sel * N_PAIRS + pair
    w_lo = jnp.where(j1 < j2, w1, w2)
    w_hi = jnp.where(j1 < j2, w2, w1)
    xe_ref[:, d:] = jnp.where(lane == 0, w_lo, jnp.where(lane == 1, w_hi, 0.0))
    bkt_ref[...] = bucket
    cnt_ref[0] = jnp.sum((lane == bucket).astype(F32), axis=0, keepdims=True)


def cross_block(x2d, y_hg, y_att, w_out, g_cross, wq, kx, vx, wo, g_ffn, w_router, batch):
    n, d = x2d.shape
    tm = X_TILE
    nt = n // tm
    per_batch = nt // batch
    m = kx.shape[0] // batch
    row = lambda i: (i, 0)
    fix = lambda i: (0, 0)
    mem = lambda i: (i // per_batch, 0)
    return pl.pallas_call(
        _cross_kernel,
        grid=(nt,),
        in_specs=[pl.BlockSpec((tm, d), row), pl.BlockSpec((tm, y_hg.shape[1]), row),
                  pl.BlockSpec((tm, y_att.shape[1]), row), pl.BlockSpec(w_out.shape, fix),
                  pl.BlockSpec((1, d), fix), pl.BlockSpec(wq.shape, fix),
                  pl.BlockSpec((m, d), mem), pl.BlockSpec((m, d), mem), pl.BlockSpec(wo.shape, fix),
                  pl.BlockSpec((1, d), fix), pl.BlockSpec(w_router.shape, fix)],
        out_specs=[pl.BlockSpec((tm, d + LANES), row), pl.BlockSpec((tm, 1), row),
                   pl.BlockSpec((1, 1, LANES), lambda i: (i, 0, 0))],
        out_shape=[jax.ShapeDtypeStruct((n, d + LANES), F32), jax.ShapeDtypeStruct((n, 1), I32),
                   jax.ShapeDtypeStruct((nt, 1, LANES), F32)],
        compiler_params=_params(1),
        name="cross_block",
    )(x2d, y_hg, y_att, w_out, g_cross.reshape(1, d), wq, kx, vx, wo, g_ffn.reshape(1, d), w_router)


def _position_kernel(bkt_ref, base_ref, tri_ref, pos_ref):
    lane = lax.broadcasted_iota(I32, (bkt_ref.shape[0], LANES), 1)
    onehot = (lane == bkt_ref[...]).astype(F32)
    before = _dot(tri_ref[...], onehot.astype(BF16))
    pos = jnp.sum(onehot * (before + base_ref[0]), axis=-1, keepdims=True)
    pos_ref[...] = pos.astype(I32)


def sorted_positions(bucket, base):
    n = bucket.shape[0]
    nt = base.shape[0]
    tm = n // nt
    tri = jnp.asarray(np.tril(np.ones((tm, tm), np.float32), -1), BF16)
    return pl.pallas_call(
        _position_kernel,
        grid=(nt,),
        in_specs=[pl.BlockSpec((tm, 1), lambda i: (i, 0)),
                  pl.BlockSpec((1, 1, LANES), lambda i: (i, 0, 0)),
                  pl.BlockSpec((tm, tm), lambda i: (0, 0))],
        out_specs=pl.BlockSpec((tm, 1), lambda i: (i, 0)),
        out_shape=jax.ShapeDtypeStruct((n, 1), I32),
        compiler_params=_params(1),
        name="sorted_positions",
    )(bucket, base, tri)


def _moe_kernel(pos_ref, ea_ref, eb_ref, cnt_ref,
                x_hbm, g_ref, wga_ref, wgb_ref, wua_ref, wub_ref, wda_ref, wdb_ref, gfin_ref,
                out_hbm, src_ref, xbuf, obuf, gsem, ssem, *, final_norm):
    j = pl.program_id(0)
    nt = pl.num_programs(0)
    rows = xbuf.shape[1]
    d = obuf.shape[2]
    slot = j % 2

    def gather_copy(tile, sl, r):
        tok = src_ref[tile * rows + r]
        return pltpu.make_async_copy(x_hbm.at[pl.ds(tok, 1)], xbuf.at[sl, pl.ds(r, 1)], gsem.at[sl])

    def scatter_copy(tile, sl, r):
        tok = src_ref[tile * rows + r]
        return pltpu.make_async_copy(obuf.at[sl, pl.ds(r, 1)], out_hbm.at[pl.ds(tok, 1)], ssem.at[sl])

    def for_rows(tile, fn):
        n = cnt_ref[tile]

        def group(i, _):
            for u in range(ROW_GROUP):
                fn(i * ROW_GROUP + u)
            return 0

        def single(r, _):
            fn(r)
            return 0

        lax.fori_loop(0, n // ROW_GROUP, group, 0)
        lax.fori_loop(n // ROW_GROUP * ROW_GROUP, n, single, 0)

    def wait_rows(tile, bulk_copy, row_copy):
        n = cnt_ref[tile]
        n8 = pl.multiple_of(n // SUBLANES * SUBLANES, SUBLANES)

        @pl.when(n8 > 0)
        def _():
            bulk_copy(n8).wait()

        def single(r, _):
            row_copy(r).wait()
            return 0

        lax.fori_loop(n8, n, single, 0)

    def wait_gather(tile, sl):
        wait_rows(tile,
                  lambda m: pltpu.make_async_copy(x_hbm.at[pl.ds(0, m)], xbuf.at[sl, pl.ds(0, m)], gsem.at[sl]),
                  lambda r: gather_copy(tile, sl, r))

    def wait_scatter(tile, sl):
        wait_rows(tile,
                  lambda m: pltpu.make_async_copy(obuf.at[sl, pl.ds(0, m)], out_hbm.at[pl.ds(0, m)], ssem.at[sl]),
                  lambda r: scatter_copy(tile, sl, r))

    @pl.when(j == 0)
    def _():
        def fill(t, _):
            src_ref[pos_ref[t]] = t
            return 0
        lax.fori_loop(0, pos_ref.shape[0], fill, 0, unroll=8)
        xbuf[...] = jnp.zeros_like(xbuf)
        for_rows(0, lambda r: gather_copy(0, 0, r).start())

    @pl.when(j + 1 < nt)
    def _():
        for_rows(j + 1, lambda r: gather_copy(j + 1, 1 - slot, r).start())

    wait_gather(j, slot)

    @pl.when(j >= 2)
    def _():
        wait_scatter(j - 2, slot)

    @pl.when(cnt_ref[j] > 0)
    def _():
        xe = xbuf[slot]
        x = xe[:, :d]
        h = _rms(x, g_ref[...]).astype(BF16)
        lane = lax.broadcasted_iota(I32, (rows, LANES), 1)
        wts = xe[:, d:]
        w_a = jnp.sum(jnp.where(lane == 0, wts, 0.0), axis=-1, keepdims=True)
        w_b = jnp.sum(jnp.where(lane == 1, wts, 0.0), axis=-1, keepdims=True)
        act_a = (jax.nn.silu(_dot(h, wga_ref[...])) * _dot(h, wua_ref[...]) * w_a).astype(BF16)
        act_b = (jax.nn.silu(_dot(h, wgb_ref[...])) * _dot(h, wub_ref[...]) * w_b).astype(BF16)
        y = x + _dot(act_a, wda_ref[...]) + _dot(act_b, wdb_ref[...])
        if final_norm:
            y = _rms(y, gfin_ref[...])
        obuf[slot] = y
        for_rows(j, lambda r: scatter_copy(j, slot, r).start())

    @pl.when(j == nt - 1)
    def _():
        @pl.when(j >= 1)
        def _():
            wait_scatter(j - 1, 1 - slot)
        wait_scatter(j, slot)


def moe_ffn(x_ext, pos, tile_a, tile_b, tile_cnt, g_ffn, w_gate, w_up, w_down, g_final, final_norm):
    n, de = x_ext.shape
    d = de - LANES
    rows = MOE_TILE
    nt = tile_cnt.shape[0]
    ff = w_gate.shape[2]
    fix = lambda j, *_: (0, 0)
    exp_a = lambda j, pos, ea, eb, cnt: (ea[j], 0, 0)
    exp_b = lambda j, pos, ea, eb, cnt: (eb[j], 0, 0)
    up_spec = lambda im: pl.BlockSpec((None, d, ff), im)
    down_spec = lambda im: pl.BlockSpec((None, ff, d), im)
    grid_spec = pltpu.PrefetchScalarGridSpec(
        num_scalar_prefetch=4,
        grid=(nt,),
        in_specs=[pl.BlockSpec(memory_space=pl.ANY), pl.BlockSpec((1, d), fix),
                  up_spec(exp_a), up_spec(exp_b), up_spec(exp_a), up_spec(exp_b),
                  down_spec(exp_a), down_spec(exp_b), pl.BlockSpec((1, d), fix)],
        out_specs=pl.BlockSpec(memory_space=pl.ANY),
        scratch_shapes=[pltpu.SMEM((nt * rows,), I32),
                        pltpu.VMEM((2, rows, de), F32), pltpu.VMEM((2, rows, d), F32),
                        pltpu.SemaphoreType.DMA((2,)), pltpu.SemaphoreType.DMA((2,))],
    )
    return pl.pallas_call(
        functools.partial(_moe_kernel, final_norm=final_norm),
        grid_spec=grid_spec,
        out_shape=jax.ShapeDtypeStruct((n, d), F32),
        compiler_params=_params(1),
        name="moe_ffn",
    )(pos, tile_a, tile_b, tile_cnt, x_ext, g_ffn.reshape(1, d),
      w_gate, w_gate, w_up, w_up, w_down, w_down, g_final.reshape(1, d))


_PAIR_LO = np.array([0, 0, 0, 1, 1, 2], np.int32)
_PAIR_HI = np.array([1, 2, 3, 2, 3, 3], np.int32)


def _tile_plan(counts, n_tokens):
    rows = MOE_TILE
    counts = counts.astype(I32)
    total = jnp.sum(counts, axis=0)
    padded = (total + rows - 1) // rows * rows
    start = jnp.cumsum(padded) - padded
    base = start[None, :] + jnp.cumsum(counts, axis=0) - counts
    nt = n_tokens // rows + N_BUCKETS
    first_row = jnp.arange(nt, dtype=I32) * rows
    end = (start + padded)[:N_BUCKETS]
    bucket = jnp.sum((first_row[:, None] >= end[None, :]).astype(I32), axis=1)
    used = bucket < N_BUCKETS
    last_used = jnp.max(jnp.where(used, bucket, 0))
    b_eff = jnp.where(used, bucket, last_used)
    cnt = jnp.where(used, jnp.clip(total[b_eff] - (first_row - start[b_eff]), 0, rows), 0)
    grp = b_eff // N_PAIRS
    pair = b_eff % N_PAIRS
    exp_a = grp * EXPERTS_PER_GROUP + jnp.asarray(_PAIR_LO)[pair]
    exp_b = grp * EXPERTS_PER_GROUP + jnp.asarray(_PAIR_HI)[pair]
    return base.astype(F32), exp_a.astype(I32), exp_b.astype(I32), cnt.astype(I32)


def kernel(x, mem, positions, norm_mix, w_in, hg_lower_bounds, hg_out_norm, w_out, norm_cross, norm_mem,
           wq_x, wkv_x, wo_x, norm_ffn, w_router_group, w_router_expert, w_gate, w_up, w_down, norm_final):
    batch, seq, d = x.shape
    depth = w_in.shape[0]
    n = batch * seq
    hg_width = hg_lower_bounds.shape[1]
    att_width = (w_in.shape[2] - 4 * hg_width) // 3
    assert hg_width == HG_HEADS * HG_DK and seq % ATT_SUPER == 0 and n % TOK_TILE == 0

    lb_sm = jax.nn.softmax(hg_lower_bounds.astype(F32), axis=0)
    lbs = jnp.cumsum(lb_sm, axis=0) - lb_sm[0:1]
    cos, sin = rope_tables(positions)
    xs = x.reshape(n, d)
    mem2d = mem.reshape(-1, d)
    w_router = jnp.concatenate([w_router_group, w_router_expert], axis=-1)
    w_router = jnp.pad(w_router, ((0, 0), (0, 0), (0, LANES - w_router.shape[-1])))

    for l in range(depth):
        hproj, aq, ak, av = in_projection(xs, norm_mix[l], w_in[l].astype(BF16), cos, sin, hg_width, att_width)
        y_hg = hgrn_mixer(hproj, lbs[l], hg_out_norm[l], batch, seq)
        y_att = dilated_attention(aq, ak, av, batch, seq)
        kx, vx = memory_kv(mem2d, norm_mem[l], wkv_x[l].astype(BF16), batch)
        x_ext, bucket, counts = cross_block(xs, y_hg, y_att, w_out[l].astype(BF16), norm_cross[l],
                                            wq_x[l].astype(BF16), kx, vx, wo_x[l].astype(BF16),
                                            norm_ffn[l], w_router[l], batch)
        base, exp_a, exp_b, cnt = _tile_plan(counts.reshape(counts.shape[0], LANES), n)
        pos = sorted_positions(bucket, base.reshape(base.shape[0], 1, LANES))
        xs = moe_ffn(x_ext, pos.reshape(n), exp_a, exp_b, cnt, norm_ffn[l],
                     w_gate[l].astype(BF16), w_up[l].astype(BF16), w_down[l].astype(BF16),
                     norm_final, final_norm=(l == depth - 1))
    return xs.reshape(batch, seq, d)
```

```python
import functools

import numpy as np
import jax
import jax.numpy as jnp
from jax import lax
from jax.experimental import pallas as pl
from jax.experimental.pallas import tpu as pltpu

F32 = jnp.float32
BF16 = jnp.bfloat16
I32 = jnp.int32

NORM_EPS = 1e-6
ROPE_THETA = 10000.0
HG_HEADS = 4
HG_DK = 128
ATT_DH = 64
ATT_BLK = 128
ATT_DILATIONS = (1, 4, 16)
ATT_CLASSES = 16
ATT_SUPER = ATT_BLK * ATT_CLASSES
X_HEADS = 4
N_GROUPS = 4
EXPERTS_PER_GROUP = 4
N_EXPERTS = 16
N_PAIRS = 6
N_BUCKETS = N_GROUPS * N_PAIRS
EXPERT_FF = 512
LANES = 128
SUBLANES = 8
NEG = -1e30
LOG2_E = 1.4426950408889634

HG_T = 128
TOK_TILE = 512
X_TILE = 512
POS_TILE = 1024
MOE_TILE = 256
VMEM_LIMIT = 56 * 1024 * 1024


def _dot(a, b):
    return jnp.dot(a, b, preferred_element_type=F32)


def _dot_nt(a, b):
    return lax.dot_general(a, b, (((1,), (1,)), ((), ())), preferred_element_type=F32)


def _dot_tn(a, b):
    return lax.dot_general(a, b, (((0,), (0,)), ((), ())), preferred_element_type=F32)


def _rms(x, g):
    return x * lax.rsqrt(jnp.mean(x * x, axis=-1, keepdims=True) + NORM_EPS) * g


def _split_bf16(a):
    hi = a.astype(BF16)
    return hi, (a - hi.astype(F32)).astype(BF16)


def _params(n_axes):
    return pltpu.CompilerParams(dimension_semantics=("arbitrary",) * n_axes,
                                vmem_limit_bytes=VMEM_LIMIT)


def _rope_kernel(pos_ref, inv_ref, cos_ref, sin_ref):
    ang = pos_ref[...].astype(F32) * inv_ref[...]
    lane = lax.broadcasted_iota(I32, ang.shape, 1)
    s = jnp.sin(ang)
    cos_ref[...] = jnp.cos(ang)
    sin_ref[...] = jnp.where((lane % ATT_DH) < ATT_DH // 2, -s, s)


def rope_tables(positions):
    n = positions.size
    inv = ROPE_THETA ** (-jnp.arange(0, ATT_DH, 2, dtype=F32) / ATT_DH)
    inv = jnp.tile(inv, LANES // (ATT_DH // 2)).reshape(1, LANES)
    tm = TOK_TILE
    return pl.pallas_call(
        _rope_kernel,
        grid=(n // tm,),
        in_specs=[pl.BlockSpec((tm, 1), lambda i: (i, 0)),
                  pl.BlockSpec((1, LANES), lambda i: (0, 0))],
        out_specs=[pl.BlockSpec((tm, LANES), lambda i: (i, 0))] * 2,
        out_shape=[jax.ShapeDtypeStruct((n, LANES), F32)] * 2,
        compiler_params=_params(1),
        name="rope_tables",
    )(positions.reshape(n, 1), inv)


def _inproj_kernel(x_ref, g_ref, w_ref, cos_ref, sin_ref, hp_ref, q_ref, k_ref, v_ref):
    h = _rms(x_ref[...], g_ref[...]).astype(BF16)
    hgw = hp_ref.shape[1]
    aw = q_ref.shape[1]
    for c in range(hgw // aw):
        hp_ref[:, c * aw:(c + 1) * aw] = _dot(h, w_ref[:, c * aw:(c + 1) * aw])
    cos = cos_ref[...]
    sin = sin_ref[...]
    lane = lax.broadcasted_iota(I32, cos.shape, 1)
    first = (lane % ATT_DH) < ATT_DH // 2

    def rope(t, scale):
        for p in range(aw // LANES):
            tp = t[:, p * LANES:(p + 1) * LANES]
            rot = jnp.where(first, pltpu.roll(tp, LANES - ATT_DH // 2, axis=1),
                            pltpu.roll(tp, ATT_DH // 2, axis=1))
            yield (tp * cos + rot * sin) * scale

    aq = _dot(h, w_ref[:, hgw:hgw + aw])
    for p, blk in enumerate(rope(aq, ATT_DH ** -0.5 * LOG2_E)):
        q_ref[:, p * LANES:(p + 1) * LANES] = blk
    ak = _dot(h, w_ref[:, hgw + aw:hgw + 2 * aw])
    for p, blk in enumerate(rope(ak, 1.0)):
        k_ref[:, p * LANES:(p + 1) * LANES] = blk
    v_ref[...] = _dot(h, w_ref[:, hgw + 2 * aw:hgw + 3 * aw])


def in_projection(x2d, gain, w_in, cos, sin, hg_width, att_width):
    n, d = cos.shape[0], x2d.shape[1]
    tm = TOK_TILE
    row = lambda i: (i, 0)
    fix = lambda i: (0, 0)
    return pl.pallas_call(
        _inproj_kernel,
        grid=(n // tm,),
        in_specs=[pl.BlockSpec((tm, d), row), pl.BlockSpec((1, d), fix),
                  pl.BlockSpec(w_in.shape, fix),
                  pl.BlockSpec((tm, LANES), row), pl.BlockSpec((tm, LANES), row)],
        out_specs=[pl.BlockSpec((tm, 4 * hg_width), row)] + [pl.BlockSpec((tm, att_width), row)] * 3,
        out_shape=[jax.ShapeDtypeStruct((n, 4 * hg_width), F32)]
        + [jax.ShapeDtypeStruct((n, att_width), F32)] * 3,
        compiler_params=_params(1),
        name="in_projection",
    )(x2d, gain.reshape(1, d), w_in, cos, sin)


def _level_table(t):
    ti = np.arange(t)[:, None]
    si = np.arange(t)[None, :]
    x = np.maximum(ti ^ si, 1)
    lvl = np.floor(np.log2(x)).astype(np.int32)
    diag = int(np.log2(t))
    return np.where(si < ti, lvl, np.where(si == ti, diag, -1)).astype(np.int32)


def _hgrn_kernel(q_ref, f_ref, i_ref, gt_ref, lb_ref, gain_ref, lvl_ref, y_ref, st_ref):
    t_rows = q_ref.shape[0]
    n_lev = t_rows.bit_length() - 1

    @pl.when(pl.program_id(1) == 0)
    def _():
        st_ref[...] = jnp.zeros_like(st_ref)

    row = lax.broadcasted_iota(I32, (t_rows, HG_DK), 0)
    lvl = lvl_ref[...]
    tri = jnp.where(lvl >= 0, 1.0, 0.0).astype(BF16)
    lb_all = lb_ref[...]
    f_all = lb_all + (1.0 - lb_all) * jax.nn.sigmoid(f_ref[...])
    g_all = jnp.log(f_all) * LOG2_E
    g_hi = g_all.astype(BF16)
    r1 = g_all - g_hi.astype(F32)
    g_mid = r1.astype(BF16)
    g_lo = (r1 - g_mid.astype(F32)).astype(BF16)
    b_all = _dot(tri, g_hi) + _dot(tri, g_mid) + _dot(tri, g_lo)
    for h in range(HG_HEADS):
        sl = slice(h * HG_DK, (h + 1) * HG_DK)
        g = g_all[:, sl]
        b = b_all[:, sl]
        kk = 1.0 - f_all[:, sl]
        q = q_ref[:, sl]
        v16 = i_ref[:, sl].astype(BF16)
        scores = jnp.where(lvl == n_lev, _dot_nt(q.astype(BF16), kk.astype(BF16)), 0.0)
        first = b - g
        last = b
        for j in range(n_lev):
            if j > 0:
                half = 1 << (j - 1)
                bit = ((row >> (j - 1)) & 1) == 1
                first = jnp.where(bit, pltpu.roll(first, half, axis=0), first)
                last = jnp.where(bit, last, pltpu.roll(last, t_rows - half, axis=0))
            qj = (q * jnp.exp2(b - first)).astype(BF16)
            kj = (kk * jnp.exp2(last - b)).astype(BF16)
            scores = jnp.where(lvl == j, _dot_nt(qj, kj), scores)
        st = st_ref[h]
        o = _dot(scores.astype(BF16), v16) + _dot_nt((q * jnp.exp2(b)).astype(BF16), st.astype(BF16))
        b_last = b[t_rows - 1:t_rows, :]
        kdec = (kk * jnp.exp2(b_last - b)).astype(BF16)
        st_ref[h] = jnp.exp2(b_last) * st + _dot_tn(v16, kdec)
        o = o * lax.rsqrt(jnp.mean(o * o, axis=-1, keepdims=True) + NORM_EPS) * gain_ref[:, sl]
        y_ref[:, sl] = o * jax.nn.silu(gt_ref[:, sl])


def hgrn_mixer(hproj, lb, out_gain, batch, seq):
    n, w4 = hproj.shape
    w = w4 // 4
    t = HG_T
    nb = seq // t
    col = lambda c: (lambda b, i: (b * nb + i, c))
    fix = lambda b, i: (0, 0)
    lvl = jnp.asarray(_level_table(t))
    return pl.pallas_call(
        _hgrn_kernel,
        grid=(batch, nb),
        in_specs=[pl.BlockSpec((t, w), col(0)), pl.BlockSpec((t, w), col(1)),
                  pl.BlockSpec((t, w), col(2)), pl.BlockSpec((t, w), col(3)),
                  pl.BlockSpec((1, w), fix), pl.BlockSpec((1, w), fix),
                  pl.BlockSpec((t, t), fix)],
        out_specs=pl.BlockSpec((t, w), col(0)),
        out_shape=jax.ShapeDtypeStruct((n, w), F32),
        scratch_shapes=[pltpu.VMEM((HG_HEADS, HG_DK, HG_DK), F32)],
        compiler_params=_params(2),
        name="hgrn_mixer",
    )(hproj, hproj, hproj, hproj, lb.reshape(1, w), out_gain.reshape(1, w), lvl)


def _att_masks():
    blk = ATT_BLK
    rows = np.arange(blk)
    out = []
    for dil in ATT_DILATIONS:
        seg = blk * dil // ATT_CLASSES
        pos = (ATT_CLASSES // dil) * (rows % seg) + rows // seg
        kpos = np.concatenate([pos, blk + pos])
        dist = pos[:, None] + blk - kpos[None, :]
        out.append(((dist >= 0) & (dist <= blk)).astype(np.float32))
    return np.stack(out)


def _att_kernel(q_hbm, k_hbm, v_hbm, mask_ref, o_hbm,
                qbuf, kbuf, vbuf, acc, m_s, l_s, obuf, isem, osem):
    blk = ATT_BLK
    ncls = ATT_CLASSES
    npair = pl.num_programs(1)
    nsb = pl.num_programs(2)
    sb = pl.program_id(2)
    step = (pl.program_id(0) * npair + pl.program_id(1)) * nsb + sb
    n_steps = pl.num_programs(0) * npair * nsb
    lane = lax.broadcasted_iota(I32, (blk, LANES), 1)
    head0 = lane < ATT_DH
    col = lax.broadcasted_iota(I32, (blk, 2 * blk), 1)
    cur = step % 3
    prev = (step + 2) % 3

    def load_copies(s):
        row0 = pl.multiple_of(((s // (npair * nsb)) * nsb + s % nsb) * blk, blk)
        lane0 = pl.multiple_of(((s // nsb) % npair) * LANES, LANES)
        for r in range(ncls):
            dst = pl.ds(r * blk, blk)
            for hbm, buf, slot in ((q_hbm, qbuf, s % 2), (k_hbm, kbuf, s % 3), (v_hbm, vbuf, s % 3)):
                yield pltpu.make_async_copy(hbm.at[pl.ds(row0, blk), r, pl.ds(lane0, LANES)],
                                            buf.at[slot, dst, :], isem.at[s % 2])

    def store_copies(s):
        row0 = pl.multiple_of(((s // (npair * nsb)) * nsb + s % nsb) * blk, blk)
        lane0 = pl.multiple_of(((s // nsb) % npair) * LANES, LANES)
        for r in range(ncls):
            yield pltpu.make_async_copy(obuf.at[pl.ds(r * blk, blk), :],
                                        o_hbm.at[pl.ds(row0, blk), r, pl.ds(lane0, LANES)], osem.at[0])

    @pl.when(step == 0)
    def _():
        kbuf[2] = jnp.zeros(kbuf.shape[1:], kbuf.dtype)
        vbuf[2] = jnp.zeros(vbuf.shape[1:], vbuf.dtype)
        for cp in load_copies(step):
            cp.start()

    @pl.when(step + 1 < n_steps)
    def _():
        for cp in load_copies(step + 1):
            cp.start()

    for cp in load_copies(step):
        cp.wait()
    qs = step % 2

    def rows_of(buf, pieces, seg):
        return jnp.concatenate([buf[slot, pl.ds(o, seg), :] for slot, o in pieces], axis=0).astype(BF16)

    def block_attention(qb, kb, vb, valid):
        outs = []
        for hh in range(2):
            hm = head0 if hh == 0 else jnp.logical_not(head0)
            s = _dot_nt(jnp.where(hm, qb, jnp.zeros_like(qb)), kb)
            s = jnp.where(valid, s, NEG)
            m = jnp.max(s, axis=-1, keepdims=True)
            p = jnp.exp2(s - m)
            l = jnp.sum(p, axis=-1, keepdims=True)
            outs.append((_dot(p.astype(BF16), vb), m, l))
        (o0, m0, l0), (o1, m1, l1) = outs
        return (jnp.where(head0, o0, o1), jnp.where(head0, m0, m1), jnp.where(head0, l0, l1))

    def first_ok(is_first):
        return jnp.logical_or(col >= blk, jnp.logical_not(is_first))

    def pattern(p, dil, init):
        band = mask_ref[p] > 0.5
        nseg = ncls // dil
        seg = blk // nseg

        def body(idx, _):
            c = idx // nseg
            n = idx % nseg
            valid = jnp.logical_and(band, first_ok(jnp.logical_and(sb == 0, n == 0)))
            slot_p = jnp.where(n == 0, prev, cur)
            n_p = (n + nseg - 1) % nseg
            q_off = [pl.multiple_of((c + dil * k) * blk + seg * n, seg) for k in range(nseg)]
            p_off = [pl.multiple_of((c + dil * k) * blk + seg * n_p, seg) for k in range(nseg)]
            keys = [(slot_p, o) for o in p_off] + [(cur, o) for o in q_off]
            o, m, l = block_attention(rows_of(qbuf, [(qs, o) for o in q_off], seg),
                                      rows_of(kbuf, keys, seg), rows_of(vbuf, keys, seg), valid)
            for k, off in enumerate(q_off):
                rs = slice(k * seg, (k + 1) * seg)
                dst = pl.ds(off, seg)
                if init:
                    acc[dst, :] = o[rs]
                    m_s[dst, :] = m[rs]
                    l_s[dst, :] = l[rs]
                else:
                    m_old = m_s[dst, :]
                    m_new = jnp.maximum(m_old, m[rs])
                    a_old = jnp.exp2(m_old - m_new)
                    a_new = jnp.exp2(m[rs] - m_new)
                    acc[dst, :] = acc[dst, :] * a_old + o[rs] * a_new
                    l_s[dst, :] = l_s[dst, :] * a_old + l[rs] * a_new
                    m_s[dst, :] = m_new
            return 0

        lax.fori_loop(0, ncls, body, 0, unroll=8)

    pattern(2, ATT_DILATIONS[2], True)
    pattern(1, ATT_DILATIONS[1], False)
    pattern(0, ATT_DILATIONS[0], False)

    @pl.when(step > 0)
    def _():
        for cp in store_copies(step - 1):
            cp.wait()

    obuf[...] = acc[...] / l_s[...]
    for cp in store_copies(step):
        cp.start()

    @pl.when(step == n_steps - 1)
    def _():
        for cp in store_copies(step):
            cp.wait()


def dilated_attention(aq, ak, av, batch, seq):
    n, w = aq.shape
    nsb = seq // ATT_SUPER
    npair = w // LANES
    masks = jnp.asarray(_att_masks())
    view = lambda t: t.reshape(n // ATT_CLASSES, ATT_CLASSES, w)
    any_spec = pl.BlockSpec(memory_space=pl.ANY)
    out = pl.pallas_call(
        _att_kernel,
        grid=(batch, npair, nsb),
        in_specs=[any_spec, any_spec, any_spec, pl.BlockSpec(masks.shape, lambda b, hp, sb: (0, 0, 0))],
        out_specs=any_spec,
        out_shape=jax.ShapeDtypeStruct((n // ATT_CLASSES, ATT_CLASSES, w), F32),
        scratch_shapes=[pltpu.VMEM((2, ATT_SUPER, LANES), F32),
                        pltpu.VMEM((3, ATT_SUPER, LANES), F32),
                        pltpu.VMEM((3, ATT_SUPER, LANES), F32)]
        + [pltpu.VMEM((ATT_SUPER, LANES), F32)] * 4
        + [pltpu.SemaphoreType.DMA((2,)), pltpu.SemaphoreType.DMA((1,))],
        compiler_params=_params(3),
        name="dilated_attention",
    )(view(aq), view(ak), view(av), masks)
    return out.reshape(n, w)


def _memkv_kernel(mem_ref, g_ref, w_ref, k_ref, v_ref):
    h = _rms(mem_ref[...], g_ref[...]).astype(BF16)
    d = k_ref.shape[1]
    k_ref[...] = _dot(h, w_ref[:, :d]).astype(BF16)
    v_ref[...] = _dot(h, w_ref[:, d:]).astype(BF16)


def memory_kv(mem2d, gain, wkv, batch):
    n, d = mem2d.shape
    m = n // batch
    row = lambda b: (b, 0)
    fix = lambda b: (0, 0)
    return pl.pallas_call(
        _memkv_kernel,
        grid=(batch,),
        in_specs=[pl.BlockSpec((m, d), row), pl.BlockSpec((1, d), fix), pl.BlockSpec(wkv.shape, fix)],
        out_specs=[pl.BlockSpec((m, d), row)] * 2,
        out_shape=[jax.ShapeDtypeStruct((n, d), BF16)] * 2,
        compiler_params=_params(1),
        name="memory_kv",
    )(mem2d, gain.reshape(1, d), wkv)


def _cross_kernel(x_ref, yh_ref, ya_ref, wout_ref, gx_ref, wq_ref, kx_ref, vx_ref, wo_ref,
                  gf_ref, wr_ref, xe_ref, bkt_ref, cnt_ref):
    d = x_ref.shape[1]
    hw = yh_ref.shape[1]
    x = (x_ref[...] + _dot(yh_ref[...].astype(BF16), wout_ref[:hw, :])
         + _dot(ya_ref[...].astype(BF16), wout_ref[hw:, :]))
    h = _rms(x, gx_ref[...]).astype(BF16)
    q = _dot(h, wq_ref[...])
    dh = d // X_HEADS
    heads = []
    for hh in range(X_HEADS):
        sl = slice(hh * dh, (hh + 1) * dh)
        s = _dot_nt(q[:, sl].astype(BF16), kx_ref[:, sl]) * (dh ** -0.5)
        s = s - jnp.max(s, axis=-1, keepdims=True)
        p = jnp.exp(s)
        p = p / jnp.sum(p, axis=-1, keepdims=True)
        heads.append(_dot(p.astype(BF16), vx_ref[:, sl]).astype(BF16))
    x = x + _dot(jnp.concatenate(heads, axis=1), wo_ref[...])
    xe_ref[:, :d] = x

    hf = _rms(x, gf_ref[...])
    h_hi, h_lo = _split_bf16(hf)
    w_hi, w_lo = _split_bf16(wr_ref[...])
    both = _dot(h_hi, jnp.concatenate([w_hi, w_lo], axis=1))
    logits = both[:, :LANES] + both[:, LANES:] + _dot(h_lo, w_hi)
    lane = lax.broadcasted_iota(I32, logits.shape, 1)
    big = 1 << 20

    def first_max(vals):
        top = jnp.max(vals, axis=-1, keepdims=True)
        return top, jnp.min(jnp.where(vals == top, lane, big), axis=-1, keepdims=True)

    is_g = lane < N_GROUPS
    g_top, g_sel = first_max(jnp.where(is_g, logits, NEG))
    g_w = 1.0 / jnp.sum(jnp.where(is_g, jnp.exp(logits - g_top), 0.0), axis=-1, keepdims=True)
    e_lo = N_GROUPS + EXPERTS_PER_GROUP * g_sel
    in_grp = jnp.logical_and(lane >= e_lo, lane < e_lo + EXPERTS_PER_GROUP)
    e_log = jnp.where(in_grp, logits, NEG)
    v1, i1 = first_max(e_log)
    v2, i2 = first_max(jnp.where(lane == i1, NEG, e_log))
    t = jnp.exp(v2 - v1)
    w1 = g_w / (1.0 + t)
    w2 = g_w * t / (1.0 + t)
    j1 = i1 - e_lo
    j2 = i2 - e_lo
    lo = jnp.minimum(j1, j2)
    hi = jnp.maximum(j1, j2)
    pair = jnp.where(lo == 0, hi - 1, jnp.where(lo == 1, hi + 1, N_PAIRS - 1))
    bucket = g_sel * N_PAIRS + pair
    w_lo = jnp.where(j1 < j2, w1, w2)
    w_hi = jnp.where(j1 < j2, w2, w1)
    xe_ref[:, d:] = jnp.where(lane == 0, w_lo, jnp.where(lane == 1, w_hi, 0.0))
    bkt_ref[...] = bucket
    cnt_ref[0] = jnp.sum((lane == bucket).astype(F32), axis=0, keepdims=True)


def cross_block(x2d, y_hg, y_att, w_out, g_cross, wq, kx, vx, wo, g_ffn, w_router, batch):
    n, d = y_hg.shape[0], x2d.shape[1]
    tm = X_TILE
    nt = n // tm
    per_batch = nt // batch
    m = kx.shape[0] // batch
    row = lambda i: (i, 0)
    fix = lambda i: (0, 0)
    mem = lambda i: (i // per_batch, 0)
    return pl.pallas_call(
        _cross_kernel,
        grid=(nt,),
        in_specs=[pl.BlockSpec((tm, d), row), pl.BlockSpec((tm, y_hg.shape[1]), row),
                  pl.BlockSpec((tm, y_att.shape[1]), row), pl.BlockSpec(w_out.shape, fix),
                  pl.BlockSpec((1, d), fix), pl.BlockSpec(wq.shape, fix),
                  pl.BlockSpec((m, d), mem), pl.BlockSpec((m, d), mem), pl.BlockSpec(wo.shape, fix),
                  pl.BlockSpec((1, d), fix), pl.BlockSpec(w_router.shape, fix)],
        out_specs=[pl.BlockSpec((tm, d + LANES), row), pl.BlockSpec((tm, 1), row),
                   pl.BlockSpec((1, 1, LANES), lambda i: (i, 0, 0))],
        out_shape=[jax.ShapeDtypeStruct((n, d + LANES), F32), jax.ShapeDtypeStruct((n, 1), I32),
                   jax.ShapeDtypeStruct((nt, 1, LANES), F32)],
        compiler_params=_params(1),
        name="cross_block",
    )(x2d, y_hg, y_att, w_out, g_cross.reshape(1, d), wq, kx, vx, wo, g_ffn.reshape(1, d), w_router)


def _position_kernel(bkt_ref, base_ref, tri_ref, pos_ref):
    lane = lax.broadcasted_iota(I32, (bkt_ref.shape[0], LANES), 1)
    onehot = (lane == bkt_ref[...]).astype(F32)
    before = _dot(tri_ref[...], onehot.astype(BF16))
    pos = jnp.sum(onehot * (before + base_ref[0]), axis=-1, keepdims=True)
    pos_ref[...] = pos.astype(I32)


def sorted_positions(bucket, base):
    n = bucket.shape[0]
    nt = base.shape[0]
    tm = n // nt
    tri = jnp.asarray(np.tril(np.ones((tm, tm), np.float32), -1), BF16)
    return pl.pallas_call(
        _position_kernel,
        grid=(nt,),
        in_specs=[pl.BlockSpec((tm, 1), lambda i: (i, 0)),
                  pl.BlockSpec((1, 1, LANES), lambda i: (i, 0, 0)),
                  pl.BlockSpec((tm, tm), lambda i: (0, 0))],
        out_specs=pl.BlockSpec((tm, 1), lambda i: (i, 0)),
        out_shape=jax.ShapeDtypeStruct((n, 1), I32),
        compiler_params=_params(1),
        name="sorted_positions",
    )(bucket, base, tri)


def _moe_kernel(pos_ref, ea_ref, eb_ref, cnt_ref,
                x_hbm, g_ref, wga_ref, wgb_ref, wua_ref, wub_ref, wda_ref, wdb_ref, gfin_ref,
                out_hbm, src_ref, xbuf, obuf, gsem, ssem, *, final_norm, n_tokens):
    j = pl.program_id(0)
    nt = pl.num_programs(0)
    rows = xbuf.shape[1]
    d = obuf.shape[2]
    slot = j % 2
    other = 1 - slot

    def start_gather(tile, sl):
        for r in range(rows):
            tok = jnp.maximum(src_ref[tile * rows + r], 0)
            pltpu.make_async_copy(x_hbm.at[pl.ds(tok, 1)], xbuf.at[sl, pl.ds(r, 1)], gsem.at[sl]).start()

    def start_scatter(tile, sl, all_spare):
        for r in range(rows):
            tok = src_ref[tile * rows + r]
            spare = jnp.logical_or(tok < 0, all_spare)
            dst = jnp.where(spare, n_tokens + sl * rows + r, tok)
            pltpu.make_async_copy(obuf.at[sl, pl.ds(r, 1)], out_hbm.at[pl.ds(dst, 1)], ssem.at[sl]).start()

    def wait_gather(sl):
        pltpu.make_async_copy(x_hbm.at[pl.ds(0, rows)], xbuf.at[sl], gsem.at[sl]).wait()

    def wait_scatter(sl):
        pltpu.make_async_copy(obuf.at[sl], out_hbm.at[pl.ds(0, rows)], ssem.at[sl]).wait()

    @pl.when(j == 0)
    def _():
        def clear(t, _):
            src_ref[t] = -1
            return 0

        def fill(t, _):
            src_ref[pos_ref[t]] = t
            return 0

        lax.fori_loop(0, src_ref.shape[0], clear, 0, unroll=8)
        lax.fori_loop(0, pos_ref.shape[0], fill, 0, unroll=8)
        obuf[...] = jnp.zeros_like(obuf)
        start_gather(0, 0)
        start_scatter(0, 0, True)

    used = cnt_ref[j] > 0
    last_used = jnp.logical_and(used, jnp.logical_or(j == nt - 1, cnt_ref[jnp.minimum(j + 1, nt - 1)] == 0))

    @pl.when(used)
    def _():
        wait_gather(slot)
        wait_scatter(slot)
        start_gather(jnp.minimum(j + 1, nt - 1), other)
        start_scatter(jnp.maximum(j - 1, 0), other, j == 0)
        xe = xbuf[slot]
        x = xe[:, :d]
        h = _rms(x, g_ref[...]).astype(BF16)
        lane = lax.broadcasted_iota(I32, (rows, LANES), 1)
        wts = xe[:, d:]
        w_a = jnp.sum(jnp.where(lane == 0, wts, 0.0), axis=-1, keepdims=True)
        w_b = jnp.sum(jnp.where(lane == 1, wts, 0.0), axis=-1, keepdims=True)
        act_a = (jax.nn.silu(_dot(h, wga_ref[...])) * _dot(h, wua_ref[...]) * w_a).astype(BF16)
        act_b = (jax.nn.silu(_dot(h, wgb_ref[...])) * _dot(h, wub_ref[...]) * w_b).astype(BF16)
        y = x + _dot(act_a, wda_ref[...]) + _dot(act_b, wdb_ref[...])
        if final_norm:
            y = _rms(y, gfin_ref[...])
        obuf[slot] = y

    @pl.when(last_used)
    def _():
        start_scatter(j, slot, False)
        wait_scatter(other)
        wait_scatter(slot)
        wait_gather(other)


def moe_ffn(x_ext, pos, tile_a, tile_b, tile_cnt, g_ffn, w_gate, w_up, w_down, g_final, final_norm):
    n, de = x_ext.shape
    d = de - LANES
    rows = MOE_TILE
    nt = tile_cnt.shape[0]
    ff = w_gate.shape[2]
    fix = lambda j, *_: (0, 0)
    exp_a = lambda j, pos, ea, eb, cnt: (ea[j], 0, 0)
    exp_b = lambda j, pos, ea, eb, cnt: (eb[j], 0, 0)
    up_spec = lambda im: pl.BlockSpec((None, d, ff), im)
    down_spec = lambda im: pl.BlockSpec((None, ff, d), im)
    grid_spec = pltpu.PrefetchScalarGridSpec(
        num_scalar_prefetch=4,
        grid=(nt,),
        in_specs=[pl.BlockSpec(memory_space=pl.ANY), pl.BlockSpec((1, d), fix),
                  up_spec(exp_a), up_spec(exp_b), up_spec(exp_a), up_spec(exp_b),
                  down_spec(exp_a), down_spec(exp_b), pl.BlockSpec((1, d), fix)],
        out_specs=pl.BlockSpec(memory_space=pl.ANY),
        scratch_shapes=[pltpu.SMEM((nt * rows,), I32),
                        pltpu.VMEM((2, rows, de), F32), pltpu.VMEM((2, rows, d), F32),
                        pltpu.SemaphoreType.DMA((2,)), pltpu.SemaphoreType.DMA((2,))],
    )
    return pl.pallas_call(
        functools.partial(_moe_kernel, final_norm=final_norm, n_tokens=n),
        grid_spec=grid_spec,
        out_shape=jax.ShapeDtypeStruct((n + 2 * rows, d), F32),
        compiler_params=_params(1),
        name="moe_ffn",
    )(pos, tile_a, tile_b, tile_cnt, x_ext, g_ffn.reshape(1, d),
      w_gate, w_gate, w_up, w_up, w_down, w_down, g_final.reshape(1, d))


_PAIR_LO = np.array([0, 0, 0, 1, 1, 2], np.int32)
_PAIR_HI = np.array([1, 2, 3, 2, 3, 3], np.int32)


def _tile_plan(counts, n_tokens):
    rows = MOE_TILE
    counts = counts.astype(I32)
    total = jnp.sum(counts, axis=0)
    padded = (total + rows - 1) // rows * rows
    start = jnp.cumsum(padded) - padded
    base = start[None, :] + jnp.cumsum(counts, axis=0) - counts
    nt = n_tokens // rows + N_BUCKETS
    first_row = jnp.arange(nt, dtype=I32) * rows
    end = (start + padded)[:N_BUCKETS]
    bucket = jnp.sum((first_row[:, None] >= end[None, :]).astype(I32), axis=1)
    used = bucket < N_BUCKETS
    last_used = jnp.max(jnp.where(used, bucket, 0))
    b_eff = jnp.where(used, bucket, last_used)
    cnt = jnp.where(used, jnp.clip(total[b_eff] - (first_row - start[b_eff]), 0, rows), 0)
    grp = b_eff // N_PAIRS
    pair = b_eff % N_PAIRS
    exp_a = grp * EXPERTS_PER_GROUP + jnp.asarray(_PAIR_LO)[pair]
    exp_b = grp * EXPERTS_PER_GROUP + jnp.asarray(_PAIR_HI)[pair]
    return base.astype(F32), exp_a.astype(I32), exp_b.astype(I32), cnt.astype(I32)


def kernel(x, mem, positions, norm_mix, w_in, hg_lower_bounds, hg_out_norm, w_out, norm_cross, norm_mem,
           wq_x, wkv_x, wo_x, norm_ffn, w_router_group, w_router_expert, w_gate, w_up, w_down, norm_final):
    batch, seq, d = x.shape
    depth = w_in.shape[0]
    n = batch * seq
    hg_width = hg_lower_bounds.shape[1]
    att_width = (w_in.shape[2] - 4 * hg_width) // 3
    assert hg_width == HG_HEADS * HG_DK and seq % ATT_SUPER == 0 and n % TOK_TILE == 0

    lb_sm = jax.nn.softmax(hg_lower_bounds.astype(F32), axis=0)
    lbs = jnp.cumsum(lb_sm, axis=0) - lb_sm[0:1]
    cos, sin = rope_tables(positions)
    xs = x.reshape(n, d)
    mem2d = mem.reshape(-1, d)
    w_router = jnp.concatenate([w_router_group, w_router_expert], axis=-1)
    w_router = jnp.pad(w_router, ((0, 0), (0, 0), (0, LANES - w_router.shape[-1])))

    for l in range(depth):
        hproj, aq, ak, av = in_projection(xs, norm_mix[l], w_in[l].astype(BF16), cos, sin, hg_width, att_width)
        y_hg = hgrn_mixer(hproj, lbs[l], hg_out_norm[l], batch, seq)
        y_att = dilated_attention(aq, ak, av, batch, seq)
        kx, vx = memory_kv(mem2d, norm_mem[l], wkv_x[l].astype(BF16), batch)
        x_ext, bucket, counts = cross_block(xs, y_hg, y_att, w_out[l].astype(BF16), norm_cross[l],
                                            wq_x[l].astype(BF16), kx, vx, wo_x[l].astype(BF16),
                                            norm_ffn[l], w_router[l], batch)
        base, exp_a, exp_b, cnt = _tile_plan(counts.reshape(n // POS_TILE, -1, LANES).sum(axis=1), n)
        pos = sorted_positions(bucket, base.reshape(base.shape[0], 1, LANES))
        xs = moe_ffn(x_ext, pos.reshape(n), exp_a, exp_b, cnt, norm_ffn[l],
                     w_gate[l].astype(BF16), w_up[l].astype(BF16), w_down[l].astype(BF16),
                     norm_final, final_norm=(l == depth - 1))
    return xs[:n].reshape(batch, seq, d)
```

```python
import functools

import numpy as np
import jax
import jax.numpy as jnp
from jax import lax
from jax.experimental import pallas as pl
from jax.experimental.pallas import tpu as pltpu

F32 = jnp.float32
BF16 = jnp.bfloat16
I32 = jnp.int32

NORM_EPS = 1e-6
ROPE_THETA = 10000.0
HG_HEADS = 4
HG_DK = 128
ATT_DH = 64
ATT_BLK = 128
ATT_DILATIONS = (1, 4, 16)
ATT_CLASSES = 16
ATT_SUPER = ATT_BLK * ATT_CLASSES
X_HEADS = 4
N_GROUPS = 4
EXPERTS_PER_GROUP = 4
N_EXPERTS = 16
N_PAIRS = 6
N_BUCKETS = N_GROUPS * N_PAIRS
EXPERT_FF = 512
LANES = 128
SUBLANES = 8
NEG = -1e30
LOG2_E = 1.4426950408889634

HG_T = 128
TOK_TILE = 512
X_TILE = 512
POS_TILE = 1024
MOE_TILE = 256
VMEM_LIMIT = 56 * 1024 * 1024


def _dot(a, b):
    return jnp.dot(a, b, preferred_element_type=F32)


def _dot_nt(a, b):
    return lax.dot_general(a, b, (((1,), (1,)), ((), ())), preferred_element_type=F32)


def _dot_tn(a, b):
    return lax.dot_general(a, b, (((0,), (0,)), ((), ())), preferred_element_type=F32)


def _rms(x, g):
    return x * lax.rsqrt(jnp.mean(x * x, axis=-1, keepdims=True) + NORM_EPS) * g


def _split_bf16(a):
    hi = a.astype(BF16)
    return hi, (a - hi.astype(F32)).astype(BF16)


def _params(n_axes):
    return pltpu.CompilerParams(dimension_semantics=("arbitrary",) * n_axes,
                                vmem_limit_bytes=VMEM_LIMIT)


def _rope_kernel(pos_ref, inv_ref, cos_ref, sin_ref):
    ang = pos_ref[...].astype(F32) * inv_ref[...]
    lane = lax.broadcasted_iota(I32, ang.shape, 1)
    s = jnp.sin(ang)
    cos_ref[...] = jnp.cos(ang)
    sin_ref[...] = jnp.where((lane % ATT_DH) < ATT_DH // 2, -s, s)


def rope_tables(positions):
    n = positions.size
    inv = ROPE_THETA ** (-jnp.arange(0, ATT_DH, 2, dtype=F32) / ATT_DH)
    inv = jnp.tile(inv, LANES // (ATT_DH // 2)).reshape(1, LANES)
    tm = TOK_TILE
    return pl.pallas_call(
        _rope_kernel,
        grid=(n // tm,),
        in_specs=[pl.BlockSpec((tm, 1), lambda i: (i, 0)),
                  pl.BlockSpec((1, LANES), lambda i: (0, 0))],
        out_specs=[pl.BlockSpec((tm, LANES), lambda i: (i, 0))] * 2,
        out_shape=[jax.ShapeDtypeStruct((n, LANES), F32)] * 2,
        compiler_params=_params(1),
        name="rope_tables",
    )(positions.reshape(n, 1), inv)


def _inproj_kernel(x_ref, g_ref, w_ref, cos_ref, sin_ref, hp_ref, q_ref, k_ref, v_ref):
    h = _rms(x_ref[...], g_ref[...]).astype(BF16)
    hgw = hp_ref.shape[1]
    aw = q_ref.shape[1]
    for c in range(hgw // aw):
        hp_ref[:, c * aw:(c + 1) * aw] = _dot(h, w_ref[:, c * aw:(c + 1) * aw])
    cos = cos_ref[...]
    sin = sin_ref[...]
    lane = lax.broadcasted_iota(I32, cos.shape, 1)
    first = (lane % ATT_DH) < ATT_DH // 2

    def rope(t, scale):
        for p in range(aw // LANES):
            tp = t[:, p * LANES:(p + 1) * LANES]
            rot = jnp.where(first, pltpu.roll(tp, LANES - ATT_DH // 2, axis=1),
                            pltpu.roll(tp, ATT_DH // 2, axis=1))
            yield (tp * cos + rot * sin) * scale

    aq = _dot(h, w_ref[:, hgw:hgw + aw])
    for p, blk in enumerate(rope(aq, ATT_DH ** -0.5 * LOG2_E)):
        q_ref[:, p * LANES:(p + 1) * LANES] = blk
    ak = _dot(h, w_ref[:, hgw + aw:hgw + 2 * aw])
    for p, blk in enumerate(rope(ak, 1.0)):
        k_ref[:, p * LANES:(p + 1) * LANES] = blk
    v_ref[...] = _dot(h, w_ref[:, hgw + 2 * aw:hgw + 3 * aw])


def in_projection(x2d, gain, w_in, cos, sin, hg_width, att_width):
    n, d = cos.shape[0], x2d.shape[1]
    tm = TOK_TILE
    row = lambda i: (i, 0)
    fix = lambda i: (0, 0)
    return pl.pallas_call(
        _inproj_kernel,
        grid=(n // tm,),
        in_specs=[pl.BlockSpec((tm, d), row), pl.BlockSpec((1, d), fix),
                  pl.BlockSpec(w_in.shape, fix),
                  pl.BlockSpec((tm, LANES), row), pl.BlockSpec((tm, LANES), row)],
        out_specs=[pl.BlockSpec((tm, 4 * hg_width), row)] + [pl.BlockSpec((tm, att_width), row)] * 3,
        out_shape=[jax.ShapeDtypeStruct((n, 4 * hg_width), F32)]
        + [jax.ShapeDtypeStruct((n, att_width), F32)] * 3,
        compiler_params=_params(1),
        name="in_projection",
    )(x2d, gain.reshape(1, d), w_in, cos, sin)


def _level_table(t):
    ti = np.arange(t)[:, None]
    si = np.arange(t)[None, :]
    x = np.maximum(ti ^ si, 1)
    lvl = np.floor(np.log2(x)).astype(np.int32)
    diag = int(np.log2(t))
    return np.where(si < ti, lvl, np.where(si == ti, diag, -1)).astype(np.int32)


def _hgrn_kernel(q_ref, f_ref, i_ref, gt_ref, lb_ref, gain_ref, lvl_ref, y_ref, st_ref):
    t_rows = q_ref.shape[0]
    n_lev = t_rows.bit_length() - 1

    @pl.when(pl.program_id(1) == 0)
    def _():
        st_ref[...] = jnp.zeros_like(st_ref)

    row = lax.broadcasted_iota(I32, (t_rows, HG_DK), 0)
    lvl = lvl_ref[...]
    lb_all = lb_ref[...]
    f_all = lb_all + (1.0 - lb_all) * jax.nn.sigmoid(f_ref[...])
    g_all = jnp.log(f_all) * LOG2_E
    for h in range(HG_HEADS):
        sl = slice(h * HG_DK, (h + 1) * HG_DK)
        g = g_all[:, sl]
        b = g
        s = 1
        while s < t_rows:
            b = b + jnp.where(row >= s, pltpu.roll(b, s, axis=0), 0.0)
            s *= 2
        kk = 1.0 - f_all[:, sl]
        q = q_ref[:, sl]
        v16 = i_ref[:, sl].astype(BF16)
        scores = jnp.where(lvl == n_lev, _dot_nt(q.astype(BF16), kk.astype(BF16)), 0.0)
        first = b - g
        last = b
        for j in range(n_lev):
            if j > 0:
                half = 1 << (j - 1)
                bit = ((row >> (j - 1)) & 1) == 1
                first = jnp.where(bit, pltpu.roll(first, half, axis=0), first)
                last = jnp.where(bit, last, pltpu.roll(last, t_rows - half, axis=0))
            qj = (q * jnp.exp2(b - first)).astype(BF16)
            kj = (kk * jnp.exp2(last - b)).astype(BF16)
            scores = jnp.where(lvl == j, _dot_nt(qj, kj), scores)
        st = st_ref[h]
        o = _dot(scores.astype(BF16), v16) + _dot_nt((q * jnp.exp2(b)).astype(BF16), st.astype(BF16))
        b_last = b[t_rows - 1:t_rows, :]
        kdec = (kk * jnp.exp2(b_last - b)).astype(BF16)
        st_ref[h] = jnp.exp2(b_last) * st + _dot_tn(v16, kdec)
        o = o * lax.rsqrt(jnp.mean(o * o, axis=-1, keepdims=True) + NORM_EPS) * gain_ref[:, sl]
        y_ref[:, sl] = o * jax.nn.silu(gt_ref[:, sl])


def hgrn_mixer(hproj, lb, out_gain, batch, seq):
    n, w4 = hproj.shape
    w = w4 // 4
    t = HG_T
    nb = seq // t
    col = lambda c: (lambda b, i: (b * nb + i, c))
    fix = lambda b, i: (0, 0)
    lvl = jnp.asarray(_level_table(t))
    return pl.pallas_call(
        _hgrn_kernel,
        grid=(batch, nb),
        in_specs=[pl.BlockSpec((t, w), col(0)), pl.BlockSpec((t, w), col(1)),
                  pl.BlockSpec((t, w), col(2)), pl.BlockSpec((t, w), col(3)),
                  pl.BlockSpec((1, w), fix), pl.BlockSpec((1, w), fix),
                  pl.BlockSpec((t, t), fix)],
        out_specs=pl.BlockSpec((t, w), col(0)),
        out_shape=jax.ShapeDtypeStruct((n, w), F32),
        scratch_shapes=[pltpu.VMEM((HG_HEADS, HG_DK, HG_DK), F32)],
        compiler_params=_params(2),
        name="hgrn_mixer",
    )(hproj, hproj, hproj, hproj, lb.reshape(1, w), out_gain.reshape(1, w), lvl)


def _att_masks():
    blk = ATT_BLK
    rows = np.arange(blk)
    out = []
    for dil in ATT_DILATIONS:
        seg = blk * dil // ATT_CLASSES
        pos = (ATT_CLASSES // dil) * (rows % seg) + rows // seg
        kpos = np.concatenate([pos, blk + pos])
        dist = pos[:, None] + blk - kpos[None, :]
        out.append(((dist >= 0) & (dist <= blk)).astype(np.float32))
    return np.stack(out)


def _att_kernel(q_hbm, k_hbm, v_hbm, mask_ref, o_hbm,
                qbuf, kbuf, vbuf, acc, m_s, l_s, obuf, isem, osem):
    blk = ATT_BLK
    ncls = ATT_CLASSES
    npair = pl.num_programs(1)
    nsb = pl.num_programs(2)
    sb = pl.program_id(2)
    step = (pl.program_id(0) * npair + pl.program_id(1)) * nsb + sb
    n_steps = pl.num_programs(0) * npair * nsb
    lane = lax.broadcasted_iota(I32, (blk, LANES), 1)
    head0 = lane < ATT_DH
    col = lax.broadcasted_iota(I32, (blk, 2 * blk), 1)
    cur = step % 3
    prev = (step + 2) % 3

    def load_copies(s):
        row0 = pl.multiple_of(((s // (npair * nsb)) * nsb + s % nsb) * blk, blk)
        lane0 = pl.multiple_of(((s // nsb) % npair) * LANES, LANES)
        for r in range(ncls):
            dst = pl.ds(r * blk, blk)
            for hbm, buf, slot in ((q_hbm, qbuf, s % 2), (k_hbm, kbuf, s % 3), (v_hbm, vbuf, s % 3)):
                yield pltpu.make_async_copy(hbm.at[pl.ds(row0, blk), r, pl.ds(lane0, LANES)],
                                            buf.at[slot, dst, :], isem.at[s % 2])

    def store_copies(s):
        row0 = pl.multiple_of(((s // (npair * nsb)) * nsb + s % nsb) * blk, blk)
        lane0 = pl.multiple_of(((s // nsb) % npair) * LANES, LANES)
        for r in range(ncls):
            yield pltpu.make_async_copy(obuf.at[pl.ds(r * blk, blk), :],
                                        o_hbm.at[pl.ds(row0, blk), r, pl.ds(lane0, LANES)], osem.at[0])

    @pl.when(step == 0)
    def _():
        kbuf[2] = jnp.zeros(kbuf.shape[1:], kbuf.dtype)
        vbuf[2] = jnp.zeros(vbuf.shape[1:], vbuf.dtype)
        for cp in load_copies(step):
            cp.start()

    @pl.when(step + 1 < n_steps)
    def _():
        for cp in load_copies(step + 1):
            cp.start()

    for cp in load_copies(step):
        cp.wait()
    qs = step % 2

    def rows_of(buf, pieces, seg):
        return jnp.concatenate([buf[slot, pl.ds(o, seg), :] for slot, o in pieces], axis=0).astype(BF16)

    def block_attention(qb, kb, vb, valid):
        outs = []
        for hh in range(2):
            hm = head0 if hh == 0 else jnp.logical_not(head0)
            s = _dot_nt(jnp.where(hm, qb, jnp.zeros_like(qb)), kb)
            s = jnp.where(valid, s, NEG)
            m = jnp.max(s, axis=-1, keepdims=True)
            p = jnp.exp2(s - m)
            l = jnp.sum(p, axis=-1, keepdims=True)
            outs.append((_dot(p.astype(BF16), vb), m, l))
        (o0, m0, l0), (o1, m1, l1) = outs
        return (jnp.where(head0, o0, o1), jnp.where(head0, m0, m1), jnp.where(head0, l0, l1))

    def first_ok(is_first):
        return jnp.logical_or(col >= blk, jnp.logical_not(is_first))

    def pattern(p, dil, init):
        band = mask_ref[p] > 0.5
        nseg = ncls // dil
        seg = blk // nseg

        def body(idx, _):
            c = idx // nseg
            n = idx % nseg
            valid = jnp.logical_and(band, first_ok(jnp.logical_and(sb == 0, n == 0)))
            slot_p = jnp.where(n == 0, prev, cur)
            n_p = (n + nseg - 1) % nseg
            q_off = [pl.multiple_of((c + dil * k) * blk + seg * n, seg) for k in range(nseg)]
            p_off = [pl.multiple_of((c + dil * k) * blk + seg * n_p, seg) for k in range(nseg)]
            keys = [(slot_p, o) for o in p_off] + [(cur, o) for o in q_off]
            o, m, l = block_attention(rows_of(qbuf, [(qs, o) for o in q_off], seg),
                                      rows_of(kbuf, keys, seg), rows_of(vbuf, keys, seg), valid)
            for k, off in enumerate(q_off):
                rs = slice(k * seg, (k + 1) * seg)
                dst = pl.ds(off, seg)
                if init:
                    acc[dst, :] = o[rs]
                    m_s[dst, :] = m[rs]
                    l_s[dst, :] = l[rs]
                else:
                    m_old = m_s[dst, :]
                    m_new = jnp.maximum(m_old, m[rs])
                    a_old = jnp.exp2(m_old - m_new)
                    a_new = jnp.exp2(m[rs] - m_new)
                    acc[dst, :] = acc[dst, :] * a_old + o[rs] * a_new
                    l_s[dst, :] = l_s[dst, :] * a_old + l[rs] * a_new
                    m_s[dst, :] = m_new
            return 0

        lax.fori_loop(0, ncls, body, 0, unroll=8)

    pattern(2, ATT_DILATIONS[2], True)
    pattern(1, ATT_DILATIONS[1], False)
    pattern(0, ATT_DILATIONS[0], False)

    @pl.when(step > 0)
    def _():
        for cp in store_copies(step - 1):
            cp.wait()

    obuf[...] = acc[...] / l_s[...]
    for cp in store_copies(step):
        cp.start()

    @pl.when(step == n_steps - 1)
    def _():
        for cp in store_copies(step):
            cp.wait()


def dilated_attention(aq, ak, av, batch, seq):
    n, w = aq.shape
    nsb = seq // ATT_SUPER
    npair = w // LANES
    masks = jnp.asarray(_att_masks())
    view = lambda t: t.reshape(n // ATT_CLASSES, ATT_CLASSES, w)
    any_spec = pl.BlockSpec(memory_space=pl.ANY)
    out = pl.pallas_call(
        _att_kernel,
        grid=(batch, npair, nsb),
        in_specs=[any_spec, any_spec, any_spec, pl.BlockSpec(masks.shape, lambda b, hp, sb: (0, 0, 0))],
        out_specs=any_spec,
        out_shape=jax.ShapeDtypeStruct((n // ATT_CLASSES, ATT_CLASSES, w), F32),
        scratch_shapes=[pltpu.VMEM((2, ATT_SUPER, LANES), F32),
                        pltpu.VMEM((3, ATT_SUPER, LANES), F32),
                        pltpu.VMEM((3, ATT_SUPER, LANES), F32)]
        + [pltpu.VMEM((ATT_SUPER, LANES), F32)] * 4
        + [pltpu.SemaphoreType.DMA((2,)), pltpu.SemaphoreType.DMA((1,))],
        compiler_params=_params(3),
        name="dilated_attention",
    )(view(aq), view(ak), view(av), masks)
    return out.reshape(n, w)


def _memkv_kernel(mem_ref, g_ref, w_ref, k_ref, v_ref):
    h = _rms(mem_ref[...], g_ref[...]).astype(BF16)
    d = k_ref.shape[1]
    k_ref[...] = _dot(h, w_ref[:, :d]).astype(BF16)
    v_ref[...] = _dot(h, w_ref[:, d:]).astype(BF16)


def memory_kv(mem2d, gain, wkv, batch):
    n, d = mem2d.shape
    m = n // batch
    row = lambda b: (b, 0)
    fix = lambda b: (0, 0)
    return pl.pallas_call(
        _memkv_kernel,
        grid=(batch,),
        in_specs=[pl.BlockSpec((m, d), row), pl.BlockSpec((1, d), fix), pl.BlockSpec(wkv.shape, fix)],
        out_specs=[pl.BlockSpec((m, d), row)] * 2,
        out_shape=[jax.ShapeDtypeStruct((n, d), BF16)] * 2,
        compiler_params=_params(1),
        name="memory_kv",
    )(mem2d, gain.reshape(1, d), wkv)


def _cross_kernel(x_ref, yh_ref, ya_ref, wout_ref, gx_ref, wq_ref, kx_ref, vx_ref, wo_ref,
                  gf_ref, wr_ref, xe_ref, bkt_ref, cnt_ref):
    d = x_ref.shape[1]
    hw = yh_ref.shape[1]
    x = (x_ref[...] + _dot(yh_ref[...].astype(BF16), wout_ref[:hw, :])
         + _dot(ya_ref[...].astype(BF16), wout_ref[hw:, :]))
    h = _rms(x, gx_ref[...]).astype(BF16)
    q = _dot(h, wq_ref[...])
    dh = d // X_HEADS
    heads = []
    for hh in range(X_HEADS):
        sl = slice(hh * dh, (hh + 1) * dh)
        s = _dot_nt(q[:, sl].astype(BF16), kx_ref[:, sl]) * (dh ** -0.5)
        s = s - jnp.max(s, axis=-1, keepdims=True)
        p = jnp.exp(s)
        p = p / jnp.sum(p, axis=-1, keepdims=True)
        heads.append(_dot(p.astype(BF16), vx_ref[:, sl]).astype(BF16))
    x = x + _dot(jnp.concatenate(heads, axis=1), wo_ref[...])
    xe_ref[:, :d] = x

    hf = _rms(x, gf_ref[...])
    h_hi, h_lo = _split_bf16(hf)
    w_hi, w_lo = _split_bf16(wr_ref[...])
    both = _dot(h_hi, jnp.concatenate([w_hi, w_lo], axis=1))
    logits = both[:, :LANES] + both[:, LANES:] + _dot(h_lo, w_hi)
    lane = lax.broadcasted_iota(I32, logits.shape, 1)
    big = 1 << 20

    def first_max(vals):
        top = jnp.max(vals, axis=-1, keepdims=True)
        return top, jnp.min(jnp.where(vals == top, lane, big), axis=-1, keepdims=True)

    is_g = lane < N_GROUPS
    g_top, g_sel = first_max(jnp.where(is_g, logits, NEG))
    g_w = 1.0 / jnp.sum(jnp.where(is_g, jnp.exp(logits - g_top), 0.0), axis=-1, keepdims=True)
    e_lo = N_GROUPS + EXPERTS_PER_GROUP * g_sel
    in_grp = jnp.logical_and(lane >= e_lo, lane < e_lo + EXPERTS_PER_GROUP)
    e_log = jnp.where(in_grp, logits, NEG)
    v1, i1 = first_max(e_log)
    v2, i2 = first_max(jnp.where(lane == i1, NEG, e_log))
    t = jnp.exp(v2 - v1)
    w1 = g_w / (1.0 + t)
    w2 = g_w * t / (1.0 + t)
    j1 = i1 - e_lo
    j2 = i2 - e_lo
    lo = jnp.minimum(j1, j2)
    hi = jnp.maximum(j1, j2)
    pair = jnp.where(lo == 0, hi - 1, jnp.where(lo == 1, hi + 1, N_PAIRS - 1))
    bucket = g_sel * N_PAIRS + pair
    w_lo = jnp.where(j1 < j2, w1, w2)
    w_hi = jnp.where(j1 < j2, w2, w1)
    xe_ref[:, d:] = jnp.where(lane == 0, w_lo, jnp.where(lane == 1, w_hi, 0.0))
    bkt_ref[...] = bucket
    cnt_ref[0] = jnp.sum((lane == bucket).astype(F32), axis=0, keepdims=True)


def cross_block(x2d, y_hg, y_att, w_out, g_cross, wq, kx, vx, wo, g_ffn, w_router, batch):
    n, d = y_hg.shape[0], x2d.shape[1]
    tm = X_TILE
    nt = n // tm
    per_batch = nt // batch
    m = kx.shape[0] // batch
    row = lambda i: (i, 0)
    fix = lambda i: (0, 0)
    mem = lambda i: (i // per_batch, 0)
    return pl.pallas_call(
        _cross_kernel,
        grid=(nt,),
        in_specs=[pl.BlockSpec((tm, d), row), pl.BlockSpec((tm, y_hg.shape[1]), row),
                  pl.BlockSpec((tm, y_att.shape[1]), row), pl.BlockSpec(w_out.shape, fix),
                  pl.BlockSpec((1, d), fix), pl.BlockSpec(wq.shape, fix),
                  pl.BlockSpec((m, d), mem), pl.BlockSpec((m, d), mem), pl.BlockSpec(wo.shape, fix),
                  pl.BlockSpec((1, d), fix), pl.BlockSpec(w_router.shape, fix)],
        out_specs=[pl.BlockSpec((tm, d + LANES), row), pl.BlockSpec((tm, 1), row),
                   pl.BlockSpec((1, 1, LANES), lambda i: (i, 0, 0))],
        out_shape=[jax.ShapeDtypeStruct((n, d + LANES), F32), jax.ShapeDtypeStruct((n, 1), I32),
                   jax.ShapeDtypeStruct((nt, 1, LANES), F32)],
        compiler_params=_params(1),
        name="cross_block",
    )(x2d, y_hg, y_att, w_out, g_cross.reshape(1, d), wq, kx, vx, wo, g_ffn.reshape(1, d), w_router)


def _position_kernel(bkt_ref, base_ref, tri_ref, pos_ref):
    lane = lax.broadcasted_iota(I32, (bkt_ref.shape[0], LANES), 1)
    onehot = (lane == bkt_ref[...]).astype(F32)
    before = _dot(tri_ref[...], onehot.astype(BF16))
    pos = jnp.sum(onehot * (before + base_ref[0]), axis=-1, keepdims=True)
    pos_ref[...] = pos.astype(I32)


def sorted_positions(bucket, base):
    n = bucket.shape[0]
    nt = base.shape[0]
    tm = n // nt
    tri = jnp.asarray(np.tril(np.ones((tm, tm), np.float32), -1), BF16)
    return pl.pallas_call(
        _position_kernel,
        grid=(nt,),
        in_specs=[pl.BlockSpec((tm, 1), lambda i: (i, 0)),
                  pl.BlockSpec((1, 1, LANES), lambda i: (i, 0, 0)),
                  pl.BlockSpec((tm, tm), lambda i: (0, 0))],
        out_specs=pl.BlockSpec((tm, 1), lambda i: (i, 0)),
        out_shape=jax.ShapeDtypeStruct((n, 1), I32),
        compiler_params=_params(1),
        name="sorted_positions",
    )(bucket, base, tri)


def _moe_kernel(pos_ref, ea_ref, eb_ref, cnt_ref,
                x_hbm, g_ref, wga_ref, wgb_ref, wua_ref, wub_ref, wda_ref, wdb_ref, gfin_ref,
                out_hbm, gsrc_ref, sdst_ref, xbuf, obuf, gsem, ssem, *, final_norm, n_tokens):
    j = pl.program_id(0)
    nt = pl.num_programs(0)
    rows = xbuf.shape[1]
    d = obuf.shape[2]
    slot = j % 2
    other = 1 - slot

    def start_gather(tile, sl):
        for r in range(rows):
            tok = gsrc_ref[tile * rows + r]
            pltpu.make_async_copy(x_hbm.at[pl.ds(tok, 1)], xbuf.at[sl, pl.ds(r, 1)], gsem.at[sl]).start()

    def start_scatter(tile, sl):
        for r in range(rows):
            dst = sdst_ref[tile * rows + r]
            pltpu.make_async_copy(obuf.at[sl, pl.ds(r, 1)], out_hbm.at[pl.ds(dst, 1)], ssem.at[sl]).start()

    def start_spare_scatter(sl):
        for r in range(rows):
            pltpu.make_async_copy(obuf.at[sl, pl.ds(r, 1)], out_hbm.at[pl.ds(n_tokens + sl * rows + r, 1)],
                                  ssem.at[sl]).start()

    def wait_gather(sl):
        pltpu.make_async_copy(x_hbm.at[pl.ds(0, rows)], xbuf.at[sl], gsem.at[sl]).wait()

    def wait_scatter(sl):
        pltpu.make_async_copy(obuf.at[sl], out_hbm.at[pl.ds(0, rows)], ssem.at[sl]).wait()

    @pl.when(j == 0)
    def _():
        def pad_tile(tile, _):
            @pl.when(jnp.logical_or(cnt_ref[tile] > 0, cnt_ref[jnp.maximum(tile - 1, 0)] > 0))
            def _():
                def pad_row(r, _):
                    gsrc_ref[tile * rows + r] = 0
                    sdst_ref[tile * rows + r] = n_tokens + (tile % 2) * rows + r
                    return 0
                lax.fori_loop(cnt_ref[tile], rows, pad_row, 0)
            return 0

        def fill(t, _):
            gsrc_ref[pos_ref[t]] = t
            sdst_ref[pos_ref[t]] = t
            return 0

        lax.fori_loop(0, nt, pad_tile, 0)
        lax.fori_loop(0, pos_ref.shape[0], fill, 0, unroll=8)
        obuf[...] = jnp.zeros_like(obuf)
        start_gather(0, 0)
        start_spare_scatter(0)
        start_spare_scatter(1)

    used = cnt_ref[j] > 0
    last_used = jnp.logical_and(used, jnp.logical_or(j == nt - 1, cnt_ref[jnp.minimum(j + 1, nt - 1)] == 0))

    for static_slot in range(2):
        @pl.when(jnp.logical_and(used, slot == static_slot))
        def _():
            wait_gather(static_slot)
            wait_scatter(static_slot)
            start_gather(jnp.minimum(j + 1, nt - 1), 1 - static_slot)

            @pl.when(j > 0)
            def _():
                start_scatter(j - 1, 1 - static_slot)

    @pl.when(cnt_ref[j] > 0)
    def _():
        xe = xbuf[slot]
        x = xe[:, :d]
        h = _rms(x, g_ref[...]).astype(BF16)
        lane = lax.broadcasted_iota(I32, (rows, LANES), 1)
        wts = xe[:, d:]
        w_a = jnp.sum(jnp.where(lane == 0, wts, 0.0), axis=-1, keepdims=True)
        w_b = jnp.sum(jnp.where(lane == 1, wts, 0.0), axis=-1, keepdims=True)
        act_a = (jax.nn.silu(_dot(h, wga_ref[...])) * _dot(h, wua_ref[...]) * w_a).astype(BF16)
        act_b = (jax.nn.silu(_dot(h, wgb_ref[...])) * _dot(h, wub_ref[...]) * w_b).astype(BF16)
        y = x + _dot(act_a, wda_ref[...]) + _dot(act_b, wdb_ref[...])
        if final_norm:
            y = _rms(y, gfin_ref[...])
        obuf[slot] = y

    @pl.when(last_used)
    def _():
        start_scatter(j, slot)
        wait_scatter(other)
        wait_scatter(slot)
        wait_gather(other)


def moe_ffn(x_ext, pos, tile_a, tile_b, tile_cnt, g_ffn, w_gate, w_up, w_down, g_final, final_norm):
    n, de = x_ext.shape
    d = de - LANES
    rows = MOE_TILE
    nt = tile_cnt.shape[0]
    ff = w_gate.shape[2]
    fix = lambda j, *_: (0, 0)
    exp_a = lambda j, pos, ea, eb, cnt: (ea[j], 0, 0)
    exp_b = lambda j, pos, ea, eb, cnt: (eb[j], 0, 0)
    up_spec = lambda im: pl.BlockSpec((None, d, ff), im)
    down_spec = lambda im: pl.BlockSpec((None, ff, d), im)
    grid_spec = pltpu.PrefetchScalarGridSpec(
        num_scalar_prefetch=4,
        grid=(nt,),
        in_specs=[pl.BlockSpec(memory_space=pl.ANY), pl.BlockSpec((1, d), fix),
                  up_spec(exp_a), up_spec(exp_b), up_spec(exp_a), up_spec(exp_b),
                  down_spec(exp_a), down_spec(exp_b), pl.BlockSpec((1, d), fix)],
        out_specs=pl.BlockSpec(memory_space=pl.ANY),
        scratch_shapes=[pltpu.SMEM((nt * rows,), I32), pltpu.SMEM((nt * rows,), I32),
                        pltpu.VMEM((2, rows, de), F32), pltpu.VMEM((2, rows, d), F32),
                        pltpu.SemaphoreType.DMA((2,)), pltpu.SemaphoreType.DMA((2,))],
    )
    return pl.pallas_call(
        functools.partial(_moe_kernel, final_norm=final_norm, n_tokens=n),
        grid_spec=grid_spec,
        out_shape=jax.ShapeDtypeStruct((n + 2 * rows, d), F32),
        compiler_params=_params(1),
        name="moe_ffn",
    )(pos, tile_a, tile_b, tile_cnt, x_ext, g_ffn.reshape(1, d),
      w_gate, w_gate, w_up, w_up, w_down, w_down, g_final.reshape(1, d))


_PAIR_LO = np.array([0, 0, 0, 1, 1, 2], np.int32)
_PAIR_HI = np.array([1, 2, 3, 2, 3, 3], np.int32)


def _tile_plan(counts, n_tokens):
    rows = MOE_TILE
    counts = counts.astype(I32)
    total = jnp.sum(counts, axis=0)
    padded = (total + rows - 1) // rows * rows
    start = jnp.cumsum(padded) - padded
    base = start[None, :] + jnp.cumsum(counts, axis=0) - counts
    nt = n_tokens // rows + N_BUCKETS
    first_row = jnp.arange(nt, dtype=I32) * rows
    end = (start + padded)[:N_BUCKETS]
    bucket = jnp.sum((first_row[:, None] >= end[None, :]).astype(I32), axis=1)
    used = bucket < N_BUCKETS
    last_used = jnp.max(jnp.where(used, bucket, 0))
    b_eff = jnp.where(used, bucket, last_used)
    cnt = jnp.where(used, jnp.clip(total[b_eff] - (first_row - start[b_eff]), 0, rows), 0)
    grp = b_eff // N_PAIRS
    pair = b_eff % N_PAIRS
    exp_a = grp * EXPERTS_PER_GROUP + jnp.asarray(_PAIR_LO)[pair]
    exp_b = grp * EXPERTS_PER_GROUP + jnp.asarray(_PAIR_HI)[pair]
    return base.astype(F32), exp_a.astype(I32), exp_b.astype(I32), cnt.astype(I32)


def kernel(x, mem, positions, norm_mix, w_in, hg_lower_bounds, hg_out_norm, w_out, norm_cross, norm_mem,
           wq_x, wkv_x, wo_x, norm_ffn, w_router_group, w_router_expert, w_gate, w_up, w_down, norm_final):
    batch, seq, d = x.shape
    depth = w_in.shape[0]
    n = batch * seq
    hg_width = hg_lower_bounds.shape[1]
    att_width = (w_in.shape[2] - 4 * hg_width) // 3
    assert hg_width == HG_HEADS * HG_DK and seq % ATT_SUPER == 0 and n % TOK_TILE == 0

    lb_sm = jax.nn.softmax(hg_lower_bounds.astype(F32), axis=0)
    lbs = jnp.cumsum(lb_sm, axis=0) - lb_sm[0:1]
    cos, sin = rope_tables(positions)
    xs = x.reshape(n, d)
    mem2d = mem.reshape(-1, d)
    w_router = jnp.concatenate([w_router_group, w_router_expert], axis=-1)
    w_router = jnp.pad(w_router, ((0, 0), (0, 0), (0, LANES - w_router.shape[-1])))
    wg16 = w_gate.astype(BF16).reshape((-1,) + w_gate.shape[2:])
    wu16 = w_up.astype(BF16).reshape((-1,) + w_up.shape[2:])
    wd16 = w_down.astype(BF16).reshape((-1,) + w_down.shape[2:])

    for l in range(depth):
        hproj, aq, ak, av = in_projection(xs, norm_mix[l], w_in[l].astype(BF16), cos, sin, hg_width, att_width)
        y_hg = hgrn_mixer(hproj, lbs[l], hg_out_norm[l], batch, seq)
        y_att = dilated_attention(aq, ak, av, batch, seq)
        kx, vx = memory_kv(mem2d, norm_mem[l], wkv_x[l].astype(BF16), batch)
        x_ext, bucket, counts = cross_block(xs, y_hg, y_att, w_out[l].astype(BF16), norm_cross[l],
                                            wq_x[l].astype(BF16), kx, vx, wo_x[l].astype(BF16),
                                            norm_ffn[l], w_router[l], batch)
        base, exp_a, exp_b, cnt = _tile_plan(counts.reshape(n // POS_TILE, -1, LANES).sum(axis=1), n)
        pos = sorted_positions(bucket, base.reshape(base.shape[0], 1, LANES))
        xs = moe_ffn(x_ext, pos.reshape(n), exp_a + l * N_EXPERTS, exp_b + l * N_EXPERTS, cnt, norm_ffn[l],
                     wg16, wu16, wd16, norm_final, final_norm=(l == depth - 1))
    return xs[:n].reshape(batch, seq, d)
```

```python
import functools

import numpy as np
import jax
import jax.numpy as jnp
from jax import lax
from jax.experimental import pallas as pl
from jax.experimental.pallas import tpu as pltpu

F32 = jnp.float32
BF16 = jnp.bfloat16
I32 = jnp.int32

NORM_EPS = 1e-6
ROPE_THETA = 10000.0
HG_HEADS = 4
HG_DK = 128
ATT_DH = 64
ATT_BLK = 128
ATT_DILATIONS = (1, 4, 16)
ATT_CLASSES = 16
ATT_SUPER = ATT_BLK * ATT_CLASSES
X_HEADS = 4
N_GROUPS = 4
EXPERTS_PER_GROUP = 4
N_EXPERTS = 16
N_PAIRS = 6
N_BUCKETS = N_GROUPS * N_PAIRS
EXPERT_FF = 512
LANES = 128
SUBLANES = 8
NEG = -1e30
LOG2_E = 1.4426950408889634

HG_T = 128
TOK_TILE = 512
X_TILE = 512
POS_TILE = 1024
MOE_TILE = 256
VMEM_LIMIT = 56 * 1024 * 1024


def _dot(a, b):
    return jnp.dot(a, b, preferred_element_type=F32)


def _dot_nt(a, b):
    return lax.dot_general(a, b, (((1,), (1,)), ((), ())), preferred_element_type=F32)


def _dot_tn(a, b):
    return lax.dot_general(a, b, (((0,), (0,)), ((), ())), preferred_element_type=F32)


def _rms(x, g):
    return x * lax.rsqrt(jnp.mean(x * x, axis=-1, keepdims=True) + NORM_EPS) * g


def _split_bf16(a):
    hi = a.astype(BF16)
    return hi, (a - hi.astype(F32)).astype(BF16)


def _params(n_axes):
    return pltpu.CompilerParams(dimension_semantics=("arbitrary",) * n_axes,
                                vmem_limit_bytes=VMEM_LIMIT)


def _rope_kernel(pos_ref, inv_ref, cos_ref, sin_ref):
    ang = pos_ref[...].astype(F32) * inv_ref[...]
    lane = lax.broadcasted_iota(I32, ang.shape, 1)
    s = jnp.sin(ang)
    cos_ref[...] = jnp.cos(ang)
    sin_ref[...] = jnp.where((lane % ATT_DH) < ATT_DH // 2, -s, s)


def rope_tables(positions):
    n = positions.size
    inv = ROPE_THETA ** (-jnp.arange(0, ATT_DH, 2, dtype=F32) / ATT_DH)
    inv = jnp.tile(inv, LANES // (ATT_DH // 2)).reshape(1, LANES)
    tm = TOK_TILE
    return pl.pallas_call(
        _rope_kernel,
        grid=(n // tm,),
        in_specs=[pl.BlockSpec((tm, 1), lambda i: (i, 0)),
                  pl.BlockSpec((1, LANES), lambda i: (0, 0))],
        out_specs=[pl.BlockSpec((tm, LANES), lambda i: (i, 0))] * 2,
        out_shape=[jax.ShapeDtypeStruct((n, LANES), F32)] * 2,
        compiler_params=_params(1),
        name="rope_tables",
    )(positions.reshape(n, 1), inv)


def _inproj_kernel(x_ref, g_ref, w_ref, cos_ref, sin_ref, hp_ref, q_ref, k_ref, v_ref):
    h = _rms(x_ref[...], g_ref[...]).astype(BF16)
    hgw = hp_ref.shape[1]
    aw = q_ref.shape[1]
    for c in range(hgw // aw):
        hp_ref[:, c * aw:(c + 1) * aw] = _dot(h, w_ref[:, c * aw:(c + 1) * aw])
    cos = cos_ref[...]
    sin = sin_ref[...]
    lane = lax.broadcasted_iota(I32, cos.shape, 1)
    first = (lane % ATT_DH) < ATT_DH // 2

    def rope(t, scale):
        for p in range(aw // LANES):
            tp = t[:, p * LANES:(p + 1) * LANES]
            rot = jnp.where(first, pltpu.roll(tp, LANES - ATT_DH // 2, axis=1),
                            pltpu.roll(tp, ATT_DH // 2, axis=1))
            yield (tp * cos + rot * sin) * scale

    aq = _dot(h, w_ref[:, hgw:hgw + aw])
    for p, blk in enumerate(rope(aq, ATT_DH ** -0.5 * LOG2_E)):
        q_ref[:, p * LANES:(p + 1) * LANES] = blk
    ak = _dot(h, w_ref[:, hgw + aw:hgw + 2 * aw])
    for p, blk in enumerate(rope(ak, 1.0)):
        k_ref[:, p * LANES:(p + 1) * LANES] = blk
    v_ref[...] = _dot(h, w_ref[:, hgw + 2 * aw:hgw + 3 * aw])


def in_projection(x2d, gain, w_in, cos, sin, hg_width, att_width):
    n, d = x2d.shape
    tm = TOK_TILE
    row = lambda i: (i, 0)
    fix = lambda i: (0, 0)
    return pl.pallas_call(
        _inproj_kernel,
        grid=(n // tm,),
        in_specs=[pl.BlockSpec((tm, d), row), pl.BlockSpec((1, d), fix),
                  pl.BlockSpec(w_in.shape, fix),
                  pl.BlockSpec((tm, LANES), row), pl.BlockSpec((tm, LANES), row)],
        out_specs=[pl.BlockSpec((tm, 4 * hg_width), row)] + [pl.BlockSpec((tm, att_width), row)] * 3,
        out_shape=[jax.ShapeDtypeStruct((n, 4 * hg_width), F32)]
        + [jax.ShapeDtypeStruct((n, att_width), F32)] * 3,
        compiler_params=_params(1),
        name="in_projection",
    )(x2d, gain.reshape(1, d), w_in, cos, sin)


def _level_table(t):
    ti = np.arange(t)[:, None]
    si = np.arange(t)[None, :]
    x = np.maximum(ti ^ si, 1)
    lvl = np.floor(np.log2(x)).astype(np.int32)
    diag = int(np.log2(t))
    return np.where(si < ti, lvl, np.where(si == ti, diag, -1)).astype(np.int32)


def _hgrn_kernel(q_ref, f_ref, i_ref, gt_ref, lb_ref, gain_ref, lvl_ref, y_ref, st_ref):
    t_rows = q_ref.shape[0]
    n_lev = t_rows.bit_length() - 1

    @pl.when(pl.program_id(1) == 0)
    def _():
        st_ref[...] = jnp.zeros_like(st_ref)

    row = lax.broadcasted_iota(I32, (t_rows, HG_DK), 0)
    lvl = lvl_ref[...]
    lb_all = lb_ref[...]
    f_all = lb_all + (1.0 - lb_all) * jax.nn.sigmoid(f_ref[...])
    g_all = jnp.log(f_all) * LOG2_E
    for h in range(HG_HEADS):
        sl = slice(h * HG_DK, (h + 1) * HG_DK)
        g = g_all[:, sl]
        b = g
        s = 1
        while s < t_rows:
            b = b + jnp.where(row >= s, pltpu.roll(b, s, axis=0), 0.0)
            s *= 2
        kk = 1.0 - f_all[:, sl]
        q = q_ref[:, sl]
        v16 = i_ref[:, sl].astype(BF16)
        scores = jnp.where(lvl == n_lev, _dot_nt(q.astype(BF16), kk.astype(BF16)), 0.0)
        first = b - g
        last = b
        for j in range(n_lev):
            if j > 0:
                half = 1 << (j - 1)
                bit = ((row >> (j - 1)) & 1) == 1
                first = jnp.where(bit, pltpu.roll(first, half, axis=0), first)
                last = jnp.where(bit, last, pltpu.roll(last, t_rows - half, axis=0))
            qj = (q * jnp.exp2(b - first)).astype(BF16)
            kj = (kk * jnp.exp2(last - b)).astype(BF16)
            scores = jnp.where(lvl == j, _dot_nt(qj, kj), scores)
        st = st_ref[h]
        o = _dot(scores.astype(BF16), v16) + _dot_nt((q * jnp.exp2(b)).astype(BF16), st.astype(BF16))
        b_last = b[t_rows - 1:t_rows, :]
        kdec = (kk * jnp.exp2(b_last - b)).astype(BF16)
        st_ref[h] = jnp.exp2(b_last) * st + _dot_tn(v16, kdec)
        o = o * lax.rsqrt(jnp.mean(o * o, axis=-1, keepdims=True) + NORM_EPS) * gain_ref[:, sl]
        y_ref[:, sl] = o * jax.nn.silu(gt_ref[:, sl])


def hgrn_mixer(hproj, lb, out_gain, batch, seq):
    n, w4 = hproj.shape
    w = w4 // 4
    t = HG_T
    nb = seq // t
    col = lambda c: (lambda b, i: (b * nb + i, c))
    fix = lambda b, i: (0, 0)
    lvl = jnp.asarray(_level_table(t))
    return pl.pallas_call(
        _hgrn_kernel,
        grid=(batch, nb),
        in_specs=[pl.BlockSpec((t, w), col(0)), pl.BlockSpec((t, w), col(1)),
                  pl.BlockSpec((t, w), col(2)), pl.BlockSpec((t, w), col(3)),
                  pl.BlockSpec((1, w), fix), pl.BlockSpec((1, w), fix),
                  pl.BlockSpec((t, t), fix)],
        out_specs=pl.BlockSpec((t, w), col(0)),
        out_shape=jax.ShapeDtypeStruct((n, w), F32),
        scratch_shapes=[pltpu.VMEM((HG_HEADS, HG_DK, HG_DK), F32)],
        compiler_params=_params(2),
        name="hgrn_mixer",
    )(hproj, hproj, hproj, hproj, lb.reshape(1, w), out_gain.reshape(1, w), lvl)


def _att_masks():
    blk = ATT_BLK
    rows = np.arange(blk)
    out = []
    for dil in ATT_DILATIONS:
        seg = blk * dil // ATT_CLASSES
        pos = (ATT_CLASSES // dil) * (rows % seg) + rows // seg
        kpos = np.concatenate([pos, blk + pos])
        dist = pos[:, None] + blk - kpos[None, :]
        out.append(((dist >= 0) & (dist <= blk)).astype(np.float32))
    return np.stack(out)


def _att_kernel(q_hbm, k_hbm, v_hbm, mask_ref, o_hbm,
                qbuf, kbuf, vbuf, acc, m_s, l_s, obuf, isem, osem):
    blk = ATT_BLK
    ncls = ATT_CLASSES
    npair = pl.num_programs(1)
    nsb = pl.num_programs(2)
    sb = pl.program_id(2)
    step = (pl.program_id(0) * npair + pl.program_id(1)) * nsb + sb
    n_steps = pl.num_programs(0) * npair * nsb
    lane = lax.broadcasted_iota(I32, (blk, LANES), 1)
    head0 = lane < ATT_DH
    col = lax.broadcasted_iota(I32, (blk, 2 * blk), 1)
    cur = step % 3
    prev = (step + 2) % 3

    def load_copies(s):
        row0 = pl.multiple_of(((s // (npair * nsb)) * nsb + s % nsb) * blk, blk)
        lane0 = pl.multiple_of(((s // nsb) % npair) * LANES, LANES)
        for r in range(ncls):
            dst = pl.ds(r * blk, blk)
            for hbm, buf, slot in ((q_hbm, qbuf, s % 2), (k_hbm, kbuf, s % 3), (v_hbm, vbuf, s % 3)):
                yield pltpu.make_async_copy(hbm.at[pl.ds(row0, blk), r, pl.ds(lane0, LANES)],
                                            buf.at[slot, dst, :], isem.at[s % 2])

    def store_copies(s):
        row0 = pl.multiple_of(((s // (npair * nsb)) * nsb + s % nsb) * blk, blk)
        lane0 = pl.multiple_of(((s // nsb) % npair) * LANES, LANES)
        for r in range(ncls):
            yield pltpu.make_async_copy(obuf.at[pl.ds(r * blk, blk), :],
                                        o_hbm.at[pl.ds(row0, blk), r, pl.ds(lane0, LANES)], osem.at[0])

    @pl.when(step == 0)
    def _():
        kbuf[2] = jnp.zeros(kbuf.shape[1:], kbuf.dtype)
        vbuf[2] = jnp.zeros(vbuf.shape[1:], vbuf.dtype)
        for cp in load_copies(step):
            cp.start()

    @pl.when(step + 1 < n_steps)
    def _():
        for cp in load_copies(step + 1):
            cp.start()

    for cp in load_copies(step):
        cp.wait()
    qs = step % 2

    def rows_of(buf, pieces, seg):
        return jnp.concatenate([buf[slot, pl.ds(o, seg), :] for slot, o in pieces], axis=0).astype(BF16)

    def block_attention(qb, kb, vb, valid):
        outs = []
        for hh in range(2):
            hm = head0 if hh == 0 else jnp.logical_not(head0)
            s = _dot_nt(jnp.where(hm, qb, jnp.zeros_like(qb)), kb)
            s = jnp.where(valid, s, NEG)
            m = jnp.max(s, axis=-1, keepdims=True)
            p = jnp.exp2(s - m)
            l = jnp.sum(p, axis=-1, keepdims=True)
            outs.append((_dot(p.astype(BF16), vb), m, l))
        (o0, m0, l0), (o1, m1, l1) = outs
        return (jnp.where(head0, o0, o1), jnp.where(head0, m0, m1), jnp.where(head0, l0, l1))

    def first_ok(is_first):
        return jnp.logical_or(col >= blk, jnp.logical_not(is_first))

    def pattern(p, dil, init):
        band = mask_ref[p] > 0.5
        nseg = ncls // dil
        seg = blk // nseg

        def body(idx, _):
            c = idx // nseg
            n = idx % nseg
            valid = jnp.logical_and(band, first_ok(jnp.logical_and(sb == 0, n == 0)))
            slot_p = jnp.where(n == 0, prev, cur)
            n_p = (n + nseg - 1) % nseg
            q_off = [pl.multiple_of((c + dil * k) * blk + seg * n, seg) for k in range(nseg)]
            p_off = [pl.multiple_of((c + dil * k) * blk + seg * n_p, seg) for k in range(nseg)]
            keys = [(slot_p, o) for o in p_off] + [(cur, o) for o in q_off]
            o, m, l = block_attention(rows_of(qbuf, [(qs, o) for o in q_off], seg),
                                      rows_of(kbuf, keys, seg), rows_of(vbuf, keys, seg), valid)
            for k, off in enumerate(q_off):
                rs = slice(k * seg, (k + 1) * seg)
                dst = pl.ds(off, seg)
                if init:
                    acc[dst, :] = o[rs]
                    m_s[dst, :] = m[rs]
                    l_s[dst, :] = l[rs]
                else:
                    m_old = m_s[dst, :]
                    m_new = jnp.maximum(m_old, m[rs])
                    a_old = jnp.exp2(m_old - m_new)
                    a_new = jnp.exp2(m[rs] - m_new)
                    acc[dst, :] = acc[dst, :] * a_old + o[rs] * a_new
                    l_s[dst, :] = l_s[dst, :] * a_old + l[rs] * a_new
                    m_s[dst, :] = m_new
            return 0

        lax.fori_loop(0, ncls, body, 0, unroll=8)

    pattern(2, ATT_DILATIONS[2], True)
    pattern(1, ATT_DILATIONS[1], False)
    pattern(0, ATT_DILATIONS[0], False)

    @pl.when(step > 0)
    def _():
        for cp in store_copies(step - 1):
            cp.wait()

    obuf[...] = acc[...] / l_s[...]
    for cp in store_copies(step):
        cp.start()

    @pl.when(step == n_steps - 1)
    def _():
        for cp in store_copies(step):
            cp.wait()


def dilated_attention(aq, ak, av, batch, seq):
    n, w = aq.shape
    nsb = seq // ATT_SUPER
    npair = w // LANES
    masks = jnp.asarray(_att_masks())
    view = lambda t: t.reshape(n // ATT_CLASSES, ATT_CLASSES, w)
    any_spec = pl.BlockSpec(memory_space=pl.ANY)
    out = pl.pallas_call(
        _att_kernel,
        grid=(batch, npair, nsb),
        in_specs=[any_spec, any_spec, any_spec, pl.BlockSpec(masks.shape, lambda b, hp, sb: (0, 0, 0))],
        out_specs=any_spec,
        out_shape=jax.ShapeDtypeStruct((n // ATT_CLASSES, ATT_CLASSES, w), F32),
        scratch_shapes=[pltpu.VMEM((2, ATT_SUPER, LANES), F32),
                        pltpu.VMEM((3, ATT_SUPER, LANES), F32),
                        pltpu.VMEM((3, ATT_SUPER, LANES), F32)]
        + [pltpu.VMEM((ATT_SUPER, LANES), F32)] * 4
        + [pltpu.SemaphoreType.DMA((2,)), pltpu.SemaphoreType.DMA((1,))],
        compiler_params=_params(3),
        name="dilated_attention",
    )(view(aq), view(ak), view(av), masks)
    return out.reshape(n, w)


def _memkv_kernel(mem_ref, g_ref, w_ref, k_ref, v_ref):
    h = _rms(mem_ref[...], g_ref[...]).astype(BF16)
    d = k_ref.shape[1]
    k_ref[...] = _dot(h, w_ref[:, :d]).astype(BF16)
    v_ref[...] = _dot(h, w_ref[:, d:]).astype(BF16)


def memory_kv(mem2d, gain, wkv, batch):
    n, d = mem2d.shape
    m = n // batch
    row = lambda b: (b, 0)
    fix = lambda b: (0, 0)
    return pl.pallas_call(
        _memkv_kernel,
        grid=(batch,),
        in_specs=[pl.BlockSpec((m, d), row), pl.BlockSpec((1, d), fix), pl.BlockSpec(wkv.shape, fix)],
        out_specs=[pl.BlockSpec((m, d), row)] * 2,
        out_shape=[jax.ShapeDtypeStruct((n, d), BF16)] * 2,
        compiler_params=_params(1),
        name="memory_kv",
    )(mem2d, gain.reshape(1, d), wkv)


def _cross_kernel(x_ref, yh_ref, ya_ref, wout_ref, gx_ref, wq_ref, kx_ref, vx_ref, wo_ref,
                  gf_ref, wr_ref, xe_ref, bkt_ref, cnt_ref):
    d = x_ref.shape[1]
    hw = yh_ref.shape[1]
    x = (x_ref[...] + _dot(yh_ref[...].astype(BF16), wout_ref[:hw, :])
         + _dot(ya_ref[...].astype(BF16), wout_ref[hw:, :]))
    h = _rms(x, gx_ref[...]).astype(BF16)
    q = _dot(h, wq_ref[...])
    dh = d // X_HEADS
    heads = []
    for hh in range(X_HEADS):
        sl = slice(hh * dh, (hh + 1) * dh)
        s = _dot_nt(q[:, sl].astype(BF16), kx_ref[:, sl]) * (dh ** -0.5)
        s = s - jnp.max(s, axis=-1, keepdims=True)
        p = jnp.exp(s)
        p = p / jnp.sum(p, axis=-1, keepdims=True)
        heads.append(_dot(p.astype(BF16), vx_ref[:, sl]).astype(BF16))
    x = x + _dot(jnp.concatenate(heads, axis=1), wo_ref[...])
    for s in range(d // LANES):
        xe_ref[:, s, :] = x[:, s * LANES:(s + 1) * LANES]

    hf = _rms(x, gf_ref[...])
    h_hi, h_lo = _split_bf16(hf)
    w_hi, w_lo = _split_bf16(wr_ref[...])
    both = _dot(h_hi, jnp.concatenate([w_hi, w_lo], axis=1))
    logits = both[:, :LANES] + both[:, LANES:] + _dot(h_lo, w_hi)
    lane = lax.broadcasted_iota(I32, logits.shape, 1)
    big = 1 << 20

    def first_max(vals):
        top = jnp.max(vals, axis=-1, keepdims=True)
        return top, jnp.min(jnp.where(vals == top, lane, big), axis=-1, keepdims=True)

    is_g = lane < N_GROUPS
    g_top, g_sel = first_max(jnp.where(is_g, logits, NEG))
    g_w = 1.0 / jnp.sum(jnp.where(is_g, jnp.exp(logits - g_top), 0.0), axis=-1, keepdims=True)
    e_lo = N_GROUPS + EXPERTS_PER_GROUP * g_sel
    in_grp = jnp.logical_and(lane >= e_lo, lane < e_lo + EXPERTS_PER_GROUP)
    e_log = jnp.where(in_grp, logits, NEG)
    v1, i1 = first_max(e_log)
    v2, i2 = first_max(jnp.where(lane == i1, NEG, e_log))
    t = jnp.exp(v2 - v1)
    w1 = g_w / (1.0 + t)
    w2 = g_w * t / (1.0 + t)
    j1 = i1 - e_lo
    j2 = i2 - e_lo
    lo = jnp.minimum(j1, j2)
    hi = jnp.maximum(j1, j2)
    pair = jnp.where(lo == 0, hi - 1, jnp.where(lo == 1, hi + 1, N_PAIRS - 1))
    bucket = g_sel * N_PAIRS + pair
    w_lo = jnp.where(j1 < j2, w1, w2)
    w_hi = jnp.where(j1 < j2, w2, w1)
    xe_ref[:, d // LANES, :] = jnp.where(lane == 0, w_lo, jnp.where(lane == 1, w_hi, 0.0))
    bkt_ref[...] = bucket
    cnt_ref[0] = jnp.sum((lane == bucket).astype(F32), axis=0, keepdims=True)


def cross_block(x2d, y_hg, y_att, w_out, g_cross, wq, kx, vx, wo, g_ffn, w_router, batch):
    n, d = x2d.shape
    tm = X_TILE
    nt = n // tm
    per_batch = nt // batch
    m = kx.shape[0] // batch
    row = lambda i: (i, 0)
    fix = lambda i: (0, 0)
    mem = lambda i: (i // per_batch, 0)
    return pl.pallas_call(
        _cross_kernel,
        grid=(nt,),
        in_specs=[pl.BlockSpec((tm, d), row), pl.BlockSpec((tm, y_hg.shape[1]), row),
                  pl.BlockSpec((tm, y_att.shape[1]), row), pl.BlockSpec(w_out.shape, fix),
                  pl.BlockSpec((1, d), fix), pl.BlockSpec(wq.shape, fix),
                  pl.BlockSpec((m, d), mem), pl.BlockSpec((m, d), mem), pl.BlockSpec(wo.shape, fix),
                  pl.BlockSpec((1, d), fix), pl.BlockSpec(w_router.shape, fix)],
        out_specs=[pl.BlockSpec((tm, d // LANES + 1, LANES), lambda i: (i, 0, 0)), pl.BlockSpec((tm, 1), row),
                   pl.BlockSpec((1, 1, LANES), lambda i: (i, 0, 0))],
        out_shape=[jax.ShapeDtypeStruct((n, d // LANES + 1, LANES), F32), jax.ShapeDtypeStruct((n, 1), I32),
                   jax.ShapeDtypeStruct((nt, 1, LANES), F32)],
        compiler_params=_params(1),
        name="cross_block",
    )(x2d, y_hg, y_att, w_out, g_cross.reshape(1, d), wq, kx, vx, wo, g_ffn.reshape(1, d), w_router)


def _position_kernel(bkt_ref, base_ref, tri_ref, pos_ref):
    lane = lax.broadcasted_iota(I32, (bkt_ref.shape[0], LANES), 1)
    onehot = (lane == bkt_ref[...]).astype(F32)
    before = _dot(tri_ref[...], onehot.astype(BF16))
    pos = jnp.sum(onehot * (before + base_ref[0]), axis=-1, keepdims=True)
    pos_ref[...] = pos.astype(I32)


def sorted_positions(bucket, base):
    n = bucket.shape[0]
    nt = base.shape[0]
    tm = n // nt
    tri = jnp.asarray(np.tril(np.ones((tm, tm), np.float32), -1), BF16)
    return pl.pallas_call(
        _position_kernel,
        grid=(nt,),
        in_specs=[pl.BlockSpec((tm, 1), lambda i: (i, 0)),
                  pl.BlockSpec((1, 1, LANES), lambda i: (i, 0, 0)),
                  pl.BlockSpec((tm, tm), lambda i: (0, 0))],
        out_specs=pl.BlockSpec((tm, 1), lambda i: (i, 0)),
        out_shape=jax.ShapeDtypeStruct((n, 1), I32),
        compiler_params=_params(1),
        name="sorted_positions",
    )(bucket, base, tri)


def _moe_kernel(pos_ref, ea_ref, eb_ref, cnt_ref,
                x_hbm, g_ref, wga_ref, wgb_ref, wua_ref, wub_ref, wda_ref, wdb_ref, gfin_ref,
                out_hbm, gsrc_ref, sdst_ref, xbuf, obuf, gsem, ssem, *, final_norm, n_tokens):
    j = pl.program_id(0)
    nt = pl.num_programs(0)
    rows = xbuf.shape[1]
    d = obuf.shape[2] * LANES
    slot = j % 2
    other = 1 - slot

    def start_gather(tile, sl):
        for r in range(rows):
            tok = gsrc_ref[tile * rows + r]
            pltpu.make_async_copy(x_hbm.at[pl.ds(tok, 1)], xbuf.at[sl, pl.ds(r, 1)], gsem.at[sl]).start()

    def start_scatter(tile, sl):
        for r in range(rows):
            dst = sdst_ref[tile * rows + r]
            pltpu.make_async_copy(obuf.at[sl, pl.ds(r, 1)], out_hbm.at[pl.ds(dst, 1)], ssem.at[sl]).start()

    def start_spare_scatter(sl):
        for r in range(rows):
            pltpu.make_async_copy(obuf.at[sl, pl.ds(r, 1)], out_hbm.at[pl.ds(n_tokens + sl * rows + r, 1)],
                                  ssem.at[sl]).start()

    def wait_gather(sl):
        pltpu.make_async_copy(x_hbm.at[pl.ds(0, rows)], xbuf.at[sl], gsem.at[sl]).wait()

    def wait_scatter(sl):
        pltpu.make_async_copy(obuf.at[sl], out_hbm.at[pl.ds(0, rows)], ssem.at[sl]).wait()

    @pl.when(j == 0)
    def _():
        def pad_tile(tile, _):
            @pl.when(jnp.logical_or(cnt_ref[tile] > 0, cnt_ref[jnp.maximum(tile - 1, 0)] > 0))
            def _():
                def pad_row(r, _):
                    gsrc_ref[tile * rows + r] = 0
                    sdst_ref[tile * rows + r] = n_tokens + (tile % 2) * rows + r
                    return 0
                lax.fori_loop(cnt_ref[tile], rows, pad_row, 0)
            return 0

        def fill(t, _):
            gsrc_ref[pos_ref[t]] = t
            sdst_ref[pos_ref[t]] = t
            return 0

        lax.fori_loop(0, nt, pad_tile, 0)
        lax.fori_loop(0, pos_ref.shape[0], fill, 0, unroll=8)
        obuf[...] = jnp.zeros_like(obuf)
        start_gather(0, 0)
        start_spare_scatter(0)
        start_spare_scatter(1)

    used = cnt_ref[j] > 0
    last_used = jnp.logical_and(used, jnp.logical_or(j == nt - 1, cnt_ref[jnp.minimum(j + 1, nt - 1)] == 0))

    for static_slot in range(2):
        @pl.when(jnp.logical_and(used, slot == static_slot))
        def _():
            wait_gather(static_slot)
            wait_scatter(static_slot)
            start_gather(jnp.minimum(j + 1, nt - 1), 1 - static_slot)

            @pl.when(j > 0)
            def _():
                start_scatter(j - 1, 1 - static_slot)

    @pl.when(cnt_ref[j] > 0)
    def _():
        x = jnp.concatenate([xbuf[slot, :, s, :] for s in range(d // LANES)], axis=1)
        h = _rms(x, g_ref[...]).astype(BF16)
        lane = lax.broadcasted_iota(I32, (rows, LANES), 1)
        wts = xbuf[slot, :, d // LANES, :]
        w_a = jnp.sum(jnp.where(lane == 0, wts, 0.0), axis=-1, keepdims=True)
        w_b = jnp.sum(jnp.where(lane == 1, wts, 0.0), axis=-1, keepdims=True)
        act_a = (jax.nn.silu(_dot(h, wga_ref[...])) * _dot(h, wua_ref[...]) * w_a).astype(BF16)
        act_b = (jax.nn.silu(_dot(h, wgb_ref[...])) * _dot(h, wub_ref[...]) * w_b).astype(BF16)
        y = x + _dot(act_a, wda_ref[...]) + _dot(act_b, wdb_ref[...])
        if final_norm:
            y = _rms(y, gfin_ref[...])
        for s in range(d // LANES):
            obuf[slot, :, s, :] = y[:, s * LANES:(s + 1) * LANES]

    @pl.when(last_used)
    def _():
        start_scatter(j, slot)
        wait_scatter(other)
        wait_scatter(slot)
        wait_gather(other)


def moe_ffn(x_ext, pos, tile_a, tile_b, tile_cnt, g_ffn, w_gate, w_up, w_down, g_final, final_norm):
    n, slabs, _ = x_ext.shape
    d = (slabs - 1) * LANES
    rows = MOE_TILE
    nt = tile_cnt.shape[0]
    ff = w_gate.shape[2]
    fix = lambda j, *_: (0, 0)
    exp_a = lambda j, pos, ea, eb, cnt: (ea[j], 0, 0)
    exp_b = lambda j, pos, ea, eb, cnt: (eb[j], 0, 0)
    up_spec = lambda im: pl.BlockSpec((None, d, ff), im)
    down_spec = lambda im: pl.BlockSpec((None, ff, d), im)
    grid_spec = pltpu.PrefetchScalarGridSpec(
        num_scalar_prefetch=4,
        grid=(nt,),
        in_specs=[pl.BlockSpec(memory_space=pl.ANY), pl.BlockSpec((1, d), fix),
                  up_spec(exp_a), up_spec(exp_b), up_spec(exp_a), up_spec(exp_b),
                  down_spec(exp_a), down_spec(exp_b), pl.BlockSpec((1, d), fix)],
        out_specs=pl.BlockSpec(memory_space=pl.ANY),
        scratch_shapes=[pltpu.SMEM((nt * rows,), I32), pltpu.SMEM((nt * rows,), I32),
                        pltpu.VMEM((2, rows, slabs, LANES), F32), pltpu.VMEM((2, rows, slabs - 1, LANES), F32),
                        pltpu.SemaphoreType.DMA((2,)), pltpu.SemaphoreType.DMA((2,))],
    )
    return pl.pallas_call(
        functools.partial(_moe_kernel, final_norm=final_norm, n_tokens=n),
        grid_spec=grid_spec,
        out_shape=jax.ShapeDtypeStruct((n + 2 * rows, slabs - 1, LANES), F32),
        compiler_params=_params(1),
        name="moe_ffn",
    )(pos, tile_a, tile_b, tile_cnt, x_ext, g_ffn.reshape(1, d),
      w_gate, w_gate, w_up, w_up, w_down, w_down, g_final.reshape(1, d))


_PAIR_LO = np.array([0, 0, 0, 1, 1, 2], np.int32)
_PAIR_HI = np.array([1, 2, 3, 2, 3, 3], np.int32)


def _tile_plan(counts, n_tokens):
    rows = MOE_TILE
    counts = counts.astype(I32)
    total = jnp.sum(counts, axis=0)
    padded = (total + rows - 1) // rows * rows
    start = jnp.cumsum(padded) - padded
    base = start[None, :] + jnp.cumsum(counts, axis=0) - counts
    nt = n_tokens // rows + N_BUCKETS
    first_row = jnp.arange(nt, dtype=I32) * rows
    end = (start + padded)[:N_BUCKETS]
    bucket = jnp.sum((first_row[:, None] >= end[None, :]).astype(I32), axis=1)
    used = bucket < N_BUCKETS
    last_used = jnp.max(jnp.where(used, bucket, 0))
    b_eff = jnp.where(used, bucket, last_used)
    cnt = jnp.where(used, jnp.clip(total[b_eff] - (first_row - start[b_eff]), 0, rows), 0)
    grp = b_eff // N_PAIRS
    pair = b_eff % N_PAIRS
    exp_a = grp * EXPERTS_PER_GROUP + jnp.asarray(_PAIR_LO)[pair]
    exp_b = grp * EXPERTS_PER_GROUP + jnp.asarray(_PAIR_HI)[pair]
    return base.astype(F32), exp_a.astype(I32), exp_b.astype(I32), cnt.astype(I32)


def kernel(x, mem, positions, norm_mix, w_in, hg_lower_bounds, hg_out_norm, w_out, norm_cross, norm_mem,
           wq_x, wkv_x, wo_x, norm_ffn, w_router_group, w_router_expert, w_gate, w_up, w_down, norm_final):
    batch, seq, d = x.shape
    depth = w_in.shape[0]
    n = batch * seq
    hg_width = hg_lower_bounds.shape[1]
    att_width = (w_in.shape[2] - 4 * hg_width) // 3
    assert hg_width == HG_HEADS * HG_DK and seq % ATT_SUPER == 0 and n % TOK_TILE == 0

    lb_sm = jax.nn.softmax(hg_lower_bounds.astype(F32), axis=0)
    lbs = jnp.cumsum(lb_sm, axis=0) - lb_sm[0:1]
    cos, sin = rope_tables(positions)
    xs = x.reshape(n, d)
    mem2d = mem.reshape(-1, d)
    w_router = jnp.concatenate([w_router_group, w_router_expert], axis=-1)
    w_router = jnp.pad(w_router, ((0, 0), (0, 0), (0, LANES - w_router.shape[-1])))
    wg16 = w_gate.astype(BF16).reshape((-1,) + w_gate.shape[2:])
    wu16 = w_up.astype(BF16).reshape((-1,) + w_up.shape[2:])
    wd16 = w_down.astype(BF16).reshape((-1,) + w_down.shape[2:])

    for l in range(depth):
        hproj, aq, ak, av = in_projection(xs, norm_mix[l], w_in[l].astype(BF16), cos, sin, hg_width, att_width)
        y_hg = hgrn_mixer(hproj, lbs[l], hg_out_norm[l], batch, seq)
        y_att = dilated_attention(aq, ak, av, batch, seq)
        kx, vx = memory_kv(mem2d, norm_mem[l], wkv_x[l].astype(BF16), batch)
        x_ext, bucket, counts = cross_block(xs, y_hg, y_att, w_out[l].astype(BF16), norm_cross[l],
                                            wq_x[l].astype(BF16), kx, vx, wo_x[l].astype(BF16),
                                            norm_ffn[l], w_router[l], batch)
        base, exp_a, exp_b, cnt = _tile_plan(counts.reshape(n // POS_TILE, -1, LANES).sum(axis=1), n)
        pos = sorted_positions(bucket, base.reshape(base.shape[0], 1, LANES))
        slabs = moe_ffn(x_ext, pos.reshape(n), exp_a + l * N_EXPERTS, exp_b + l * N_EXPERTS, cnt, norm_ffn[l],
                        wg16, wu16, wd16, norm_final, final_norm=(l == depth - 1))
        xs = slabs[:n].reshape(n, d)
    return xs.reshape(batch, seq, d)
```

```python
import functools

import numpy as np
import jax
import jax.numpy as jnp
from jax import lax
from jax.experimental import pallas as pl
from jax.experimental.pallas import tpu as pltpu

F32 = jnp.float32
BF16 = jnp.bfloat16
I32 = jnp.int32

NORM_EPS = 1e-6
ROPE_THETA = 10000.0
HG_HEADS = 4
HG_DK = 128
ATT_DH = 64
ATT_BLK = 128
ATT_DILATIONS = (1, 4, 16)
ATT_CLASSES = 16
ATT_SUPER = ATT_BLK * ATT_CLASSES
X_HEADS = 4
N_GROUPS = 4
EXPERTS_PER_GROUP = 4
N_EXPERTS = 16
N_PAIRS = 6
N_BUCKETS = N_GROUPS * N_PAIRS
EXPERT_FF = 512
LANES = 128
SUBLANES = 8
NEG = -1e30
LOG2_E = 1.4426950408889634

HG_T = 128
HG_STEP = 256
TOK_TILE = 512
X_TILE = 512
POS_TILE = 1024
MOE_TILE = 256
DMA_PRIORITIES = 2
VMEM_LIMIT = 56 * 1024 * 1024


def _dot(a, b):
    return jnp.dot(a, b, preferred_element_type=F32)


def _dot_nt(a, b):
    return lax.dot_general(a, b, (((1,), (1,)), ((), ())), preferred_element_type=F32)


def _dot_tn(a, b):
    return lax.dot_general(a, b, (((0,), (0,)), ((), ())), preferred_element_type=F32)


def _rms(x, g):
    return x * lax.rsqrt(jnp.mean(x * x, axis=-1, keepdims=True) + NORM_EPS) * g


def _split_bf16(a):
    hi = a.astype(BF16)
    return hi, (a - hi.astype(F32)).astype(BF16)


def _params(n_axes):
    return pltpu.CompilerParams(dimension_semantics=("arbitrary",) * n_axes,
                                vmem_limit_bytes=VMEM_LIMIT)


def _rope_kernel(pos_ref, inv_ref, cos_ref, sin_ref):
    ang = pos_ref[...].astype(F32) * inv_ref[...]
    lane = lax.broadcasted_iota(I32, ang.shape, 1)
    s = jnp.sin(ang)
    cos_ref[...] = jnp.cos(ang)
    sin_ref[...] = jnp.where((lane % ATT_DH) < ATT_DH // 2, -s, s)


def rope_tables(positions):
    n = positions.size
    inv = ROPE_THETA ** (-jnp.arange(0, ATT_DH, 2, dtype=F32) / ATT_DH)
    inv = jnp.tile(inv, LANES // (ATT_DH // 2)).reshape(1, LANES)
    tm = TOK_TILE
    return pl.pallas_call(
        _rope_kernel,
        grid=(n // tm,),
        in_specs=[pl.BlockSpec((tm, 1), lambda i: (i, 0)),
                  pl.BlockSpec((1, LANES), lambda i: (0, 0))],
        out_specs=[pl.BlockSpec((tm, LANES), lambda i: (i, 0))] * 2,
        out_shape=[jax.ShapeDtypeStruct((n, LANES), F32)] * 2,
        compiler_params=_params(1),
        name="rope_tables",
    )(positions.reshape(n, 1), inv)


def _inproj_kernel(x_ref, g_ref, w_ref, cos_ref, sin_ref, hp_ref, q_ref, k_ref, v_ref):
    h = _rms(x_ref[...], g_ref[...]).astype(BF16)
    hgw = hp_ref.shape[1]
    aw = q_ref.shape[1]
    for c in range(hgw // aw):
        hp_ref[:, c * aw:(c + 1) * aw] = _dot(h, w_ref[:, c * aw:(c + 1) * aw])
    cos = cos_ref[...]
    sin = sin_ref[...]
    lane = lax.broadcasted_iota(I32, cos.shape, 1)
    first = (lane % ATT_DH) < ATT_DH // 2

    def rope(t, scale):
        for p in range(aw // LANES):
            tp = t[:, p * LANES:(p + 1) * LANES]
            rot = jnp.where(first, pltpu.roll(tp, LANES - ATT_DH // 2, axis=1),
                            pltpu.roll(tp, ATT_DH // 2, axis=1))
            yield (tp * cos + rot * sin) * scale

    aq = _dot(h, w_ref[:, hgw:hgw + aw])
    for p, blk in enumerate(rope(aq, ATT_DH ** -0.5 * LOG2_E)):
        q_ref[:, p * LANES:(p + 1) * LANES] = blk
    ak = _dot(h, w_ref[:, hgw + aw:hgw + 2 * aw])
    for p, blk in enumerate(rope(ak, 1.0)):
        k_ref[:, p * LANES:(p + 1) * LANES] = blk
    v_ref[...] = _dot(h, w_ref[:, hgw + 2 * aw:hgw + 3 * aw])


def in_projection(x2d, gain, w_in, cos, sin, hg_width, att_width):
    n, d = cos.shape[0], x2d.shape[1]
    tm = TOK_TILE
    row = lambda i: (i, 0)
    fix = lambda i: (0, 0)
    return pl.pallas_call(
        _inproj_kernel,
        grid=(n // tm,),
        in_specs=[pl.BlockSpec((tm, d), row), pl.BlockSpec((1, d), fix),
                  pl.BlockSpec(w_in.shape, fix),
                  pl.BlockSpec((tm, LANES), row), pl.BlockSpec((tm, LANES), row)],
        out_specs=[pl.BlockSpec((tm, 4 * hg_width), row)] + [pl.BlockSpec((tm, att_width), row)] * 3,
        out_shape=[jax.ShapeDtypeStruct((n, 4 * hg_width), F32)]
        + [jax.ShapeDtypeStruct((n, att_width), F32)] * 3,
        compiler_params=_params(1),
        name="in_projection",
    )(x2d, gain.reshape(1, d), w_in, cos, sin)


def _level_table(t):
    ti = np.arange(t)[:, None]
    si = np.arange(t)[None, :]
    x = np.maximum(ti ^ si, 1)
    lvl = np.floor(np.log2(x)).astype(np.int32)
    diag = int(np.log2(t))
    return np.where(si < ti, lvl, np.where(si == ti, diag, -1)).astype(np.int32)


def _hgrn_kernel(q_ref, f_ref, i_ref, gt_ref, lb_ref, gain_ref, lvl_ref, y_ref, st_ref):
    t_rows = lvl_ref.shape[0]
    n_lev = t_rows.bit_length() - 1

    @pl.when(pl.program_id(1) == 0)
    def _():
        st_ref[...] = jnp.zeros_like(st_ref)

    row = lax.broadcasted_iota(I32, (t_rows, HG_DK), 0)
    lvl = lvl_ref[...]
    lb_all = lb_ref[...]
    for u, h in [(u, h) for u in range(q_ref.shape[0] // t_rows) for h in range(HG_HEADS)]:
        rs = slice(u * t_rows, (u + 1) * t_rows)
        sl = slice(h * HG_DK, (h + 1) * HG_DK)
        lb = lb_all[:, sl]
        f = lb + (1.0 - lb) * jax.nn.sigmoid(f_ref[rs, sl])
        g = jnp.log(f) * LOG2_E
        b = g
        s = 1
        while s < t_rows:
            b = b + jnp.where(row >= s, pltpu.roll(b, s, axis=0), 0.0)
            s *= 2
        kk = 1.0 - f
        q = q_ref[rs, sl]
        v16 = i_ref[rs, sl].astype(BF16)
        scores = jnp.where(lvl == n_lev, _dot_nt(q.astype(BF16), kk.astype(BF16)), 0.0)
        first = b - g
        last = b
        for j in range(n_lev):
            if j > 0:
                half = 1 << (j - 1)
                bit = ((row >> (j - 1)) & 1) == 1
                first = jnp.where(bit, pltpu.roll(first, half, axis=0), first)
                last = jnp.where(bit, last, pltpu.roll(last, t_rows - half, axis=0))
            qj = (q * jnp.exp2(b - first)).astype(BF16)
            kj = (kk * jnp.exp2(last - b)).astype(BF16)
            scores = jnp.where(lvl == j, _dot_nt(qj, kj), scores)
        st = st_ref[h]
        o = _dot(scores.astype(BF16), v16) + _dot_nt((q * jnp.exp2(b)).astype(BF16), st.astype(BF16))
        b_last = b[t_rows - 1:t_rows, :]
        kdec = (kk * jnp.exp2(b_last - b)).astype(BF16)
        st_ref[h] = jnp.exp2(b_last) * st + _dot_tn(v16, kdec)
        o = o * lax.rsqrt(jnp.mean(o * o, axis=-1, keepdims=True) + NORM_EPS) * gain_ref[:, sl]
        y_ref[rs, sl] = o * jax.nn.silu(gt_ref[rs, sl])


def hgrn_mixer(hproj, lb, out_gain, batch, seq):
    n, w4 = hproj.shape
    w = w4 // 4
    t = HG_STEP
    nb = seq // t
    col = lambda c: (lambda b, i: (b * nb + i, c))
    fix = lambda b, i: (0, 0)
    lvl = jnp.asarray(_level_table(HG_T))
    return pl.pallas_call(
        _hgrn_kernel,
        grid=(batch, nb),
        in_specs=[pl.BlockSpec((t, w), col(0)), pl.BlockSpec((t, w), col(1)),
                  pl.BlockSpec((t, w), col(2)), pl.BlockSpec((t, w), col(3)),
                  pl.BlockSpec((1, w), fix), pl.BlockSpec((1, w), fix),
                  pl.BlockSpec((HG_T, HG_T), fix)],
        out_specs=pl.BlockSpec((t, w), col(0)),
        out_shape=jax.ShapeDtypeStruct((n, w), F32),
        scratch_shapes=[pltpu.VMEM((HG_HEADS, HG_DK, HG_DK), F32)],
        compiler_params=_params(2),
        name="hgrn_mixer",
    )(hproj, hproj, hproj, hproj, lb.reshape(1, w), out_gain.reshape(1, w), lvl)


def _att_masks():
    blk = ATT_BLK
    rows = np.arange(blk)
    out = []
    for dil in ATT_DILATIONS:
        seg = blk * dil // ATT_CLASSES
        pos = (ATT_CLASSES // dil) * (rows % seg) + rows // seg
        kpos = np.concatenate([pos, blk + pos])
        dist = pos[:, None] + blk - kpos[None, :]
        out.append(((dist >= 0) & (dist <= blk)).astype(np.float32))
    return np.stack(out)


def _att_kernel(q_hbm, k_hbm, v_hbm, mask_ref, o_hbm,
                qbuf, kbuf, vbuf, acc, m_s, l_s, obuf, isem, osem):
    blk = ATT_BLK
    ncls = ATT_CLASSES
    npair = pl.num_programs(1)
    nsb = pl.num_programs(2)
    sb = pl.program_id(2)
    step = (pl.program_id(0) * npair + pl.program_id(1)) * nsb + sb
    n_steps = pl.num_programs(0) * npair * nsb
    lane = lax.broadcasted_iota(I32, (blk, LANES), 1)
    head0 = lane < ATT_DH
    col = lax.broadcasted_iota(I32, (blk, 2 * blk), 1)
    cur = step % 3
    prev = (step + 2) % 3

    def load_copies(s):
        row0 = pl.multiple_of(((s // (npair * nsb)) * nsb + s % nsb) * blk, blk)
        lane0 = pl.multiple_of(((s // nsb) % npair) * LANES, LANES)
        for r in range(ncls):
            dst = pl.ds(r * blk, blk)
            for hbm, buf, slot in ((q_hbm, qbuf, s % 2), (k_hbm, kbuf, s % 3), (v_hbm, vbuf, s % 3)):
                yield pltpu.make_async_copy(hbm.at[pl.ds(row0, blk), r, pl.ds(lane0, LANES)],
                                            buf.at[slot, dst, :], isem.at[s % 2])

    def store_copies(s):
        row0 = pl.multiple_of(((s // (npair * nsb)) * nsb + s % nsb) * blk, blk)
        lane0 = pl.multiple_of(((s // nsb) % npair) * LANES, LANES)
        for r in range(ncls):
            yield pltpu.make_async_copy(obuf.at[pl.ds(r * blk, blk), :],
                                        o_hbm.at[pl.ds(row0, blk), r, pl.ds(lane0, LANES)], osem.at[0])

    @pl.when(step == 0)
    def _():
        kbuf[2] = jnp.zeros(kbuf.shape[1:], kbuf.dtype)
        vbuf[2] = jnp.zeros(vbuf.shape[1:], vbuf.dtype)
        for cp in load_copies(step):
            cp.start()

    @pl.when(step + 1 < n_steps)
    def _():
        for cp in load_copies(step + 1):
            cp.start()

    for cp in load_copies(step):
        cp.wait()
    qs = step % 2

    def rows_of(buf, pieces, seg):
        return jnp.concatenate([buf[slot, pl.ds(o, seg), :] for slot, o in pieces], axis=0).astype(BF16)

    def block_attention(qb, kb, vb, valid):
        outs = []
        for hh in range(2):
            hm = head0 if hh == 0 else jnp.logical_not(head0)
            s = _dot_nt(jnp.where(hm, qb, jnp.zeros_like(qb)), kb)
            s = jnp.where(valid, s, NEG)
            m = jnp.max(s, axis=-1, keepdims=True)
            p = jnp.exp2(s - m)
            l = jnp.sum(p, axis=-1, keepdims=True)
            outs.append((_dot(p.astype(BF16), vb), m, l))
        (o0, m0, l0), (o1, m1, l1) = outs
        return (jnp.where(head0, o0, o1), jnp.where(head0, m0, m1), jnp.where(head0, l0, l1))

    def first_ok(is_first):
        return jnp.logical_or(col >= blk, jnp.logical_not(is_first))

    def pattern(p, dil):
        band = mask_ref[p] > 0.5
        nseg = ncls // dil
        seg = blk // nseg

        def body(idx, _):
            c = idx // nseg
            n = idx % nseg
            valid = jnp.logical_and(band, first_ok(jnp.logical_and(sb == 0, n == 0)))
            slot_p = jnp.where(n == 0, prev, cur)
            n_p = (n + nseg - 1) % nseg
            q_off = [pl.multiple_of((c + dil * k) * blk + seg * n, seg) for k in range(nseg)]
            p_off = [pl.multiple_of((c + dil * k) * blk + seg * n_p, seg) for k in range(nseg)]
            keys = [(slot_p, o) for o in p_off] + [(cur, o) for o in q_off]
            o, m, l = block_attention(rows_of(qbuf, [(qs, o) for o in q_off], seg),
                                      rows_of(kbuf, keys, seg), rows_of(vbuf, keys, seg), valid)
            for k, off in enumerate(q_off):
                rs = slice(k * seg, (k + 1) * seg)
                dst = pl.ds(off, seg)
                acc[p, dst, :] = o[rs]
                m_s[p, dst, :] = m[rs]
                l_s[p, dst, :] = l[rs]
            return 0

        lax.fori_loop(0, ncls, body, 0, unroll=True)

    for p, dil in enumerate(ATT_DILATIONS):
        pattern(p, dil)

    @pl.when(step > 0)
    def _():
        for cp in store_copies(step - 1):
            cp.wait()

    def combine(i, _):
        rows = pl.ds(pl.multiple_of(i * blk, blk), blk)
        ms = [m_s[p, rows, :] for p in range(len(ATT_DILATIONS))]
        top = functools.reduce(jnp.maximum, ms)
        ws = [jnp.exp2(m - top) for m in ms]
        num = sum(w * acc[p, rows, :] for p, w in enumerate(ws))
        den = sum(w * l_s[p, rows, :] for p, w in enumerate(ws))
        obuf[rows, :] = num / den
        return 0

    lax.fori_loop(0, ncls, combine, 0, unroll=2)
    for cp in store_copies(step):
        cp.start()

    @pl.when(step == n_steps - 1)
    def _():
        for cp in store_copies(step):
            cp.wait()


def dilated_attention(aq, ak, av, batch, seq):
    n, w = aq.shape
    nsb = seq // ATT_SUPER
    npair = w // LANES
    masks = jnp.asarray(_att_masks())
    view = lambda t: t.reshape(n // ATT_CLASSES, ATT_CLASSES, w)
    any_spec = pl.BlockSpec(memory_space=pl.ANY)
    out = pl.pallas_call(
        _att_kernel,
        grid=(batch, npair, nsb),
        in_specs=[any_spec, any_spec, any_spec, pl.BlockSpec(masks.shape, lambda b, hp, sb: (0, 0, 0))],
        out_specs=any_spec,
        out_shape=jax.ShapeDtypeStruct((n // ATT_CLASSES, ATT_CLASSES, w), F32),
        scratch_shapes=[pltpu.VMEM((2, ATT_SUPER, LANES), F32),
                        pltpu.VMEM((3, ATT_SUPER, LANES), F32),
                        pltpu.VMEM((3, ATT_SUPER, LANES), F32)]
        + [pltpu.VMEM((len(ATT_DILATIONS), ATT_SUPER, LANES), F32)] * 3
        + [pltpu.VMEM((ATT_SUPER, LANES), F32),
           pltpu.SemaphoreType.DMA((2,)), pltpu.SemaphoreType.DMA((1,))],
        compiler_params=_params(3),
        name="dilated_attention",
    )(view(aq), view(ak), view(av), masks)
    return out.reshape(n, w)


def _memkv_kernel(mem_ref, g_ref, w_ref, k_ref, v_ref):
    h = _rms(mem_ref[...], g_ref[...]).astype(BF16)
    d = k_ref.shape[1]
    k_ref[...] = _dot(h, w_ref[:, :d]).astype(BF16)
    v_ref[...] = _dot(h, w_ref[:, d:]).astype(BF16)


def memory_kv(mem2d, gain, wkv, batch):
    n, d = mem2d.shape
    m = n // batch
    row = lambda b: (b, 0)
    fix = lambda b: (0, 0)
    return pl.pallas_call(
        _memkv_kernel,
        grid=(batch,),
        in_specs=[pl.BlockSpec((m, d), row), pl.BlockSpec((1, d), fix), pl.BlockSpec(wkv.shape, fix)],
        out_specs=[pl.BlockSpec((m, d), row)] * 2,
        out_shape=[jax.ShapeDtypeStruct((n, d), BF16)] * 2,
        compiler_params=_params(1),
        name="memory_kv",
    )(mem2d, gain.reshape(1, d), wkv)


def _cross_kernel(x_ref, yh_ref, ya_ref, wout_ref, gx_ref, wq_ref, kx_ref, vx_ref, wo_ref,
                  gf_ref, wr_ref, xe_ref, bkt_ref, cnt_ref):
    d = x_ref.shape[1]
    hw = yh_ref.shape[1]
    x = (x_ref[...] + _dot(yh_ref[...].astype(BF16), wout_ref[:hw, :])
         + _dot(ya_ref[...].astype(BF16), wout_ref[hw:, :]))
    h = _rms(x, gx_ref[...]).astype(BF16)
    q = _dot(h, wq_ref[...])
    dh = d // X_HEADS
    heads = []
    for hh in range(X_HEADS):
        sl = slice(hh * dh, (hh + 1) * dh)
        s = _dot_nt(q[:, sl].astype(BF16), kx_ref[:, sl]) * (dh ** -0.5)
        s = s - jnp.max(s, axis=-1, keepdims=True)
        p = jnp.exp(s)
        p = p / jnp.sum(p, axis=-1, keepdims=True)
        heads.append(_dot(p.astype(BF16), vx_ref[:, sl]).astype(BF16))
    x = x + _dot(jnp.concatenate(heads, axis=1), wo_ref[...])
    xe_ref[:, :d] = x

    hf = _rms(x, gf_ref[...])
    h_hi, h_lo = _split_bf16(hf)
    w_hi, w_lo = _split_bf16(wr_ref[...])
    both = _dot(h_hi, jnp.concatenate([w_hi, w_lo], axis=1))
    logits = both[:, :LANES] + both[:, LANES:] + _dot(h_lo, w_hi)
    lane = lax.broadcasted_iota(I32, logits.shape, 1)
    big = 1 << 20

    def first_max(vals):
        top = jnp.max(vals, axis=-1, keepdims=True)
        return top, jnp.min(jnp.where(vals == top, lane, big), axis=-1, keepdims=True)

    is_g = lane < N_GROUPS
    g_top, g_sel = first_max(jnp.where(is_g, logits, NEG))
    g_w = 1.0 / jnp.sum(jnp.where(is_g, jnp.exp(logits - g_top), 0.0), axis=-1, keepdims=True)
    e_lo = N_GROUPS + EXPERTS_PER_GROUP * g_sel
    in_grp = jnp.logical_and(lane >= e_lo, lane < e_lo + EXPERTS_PER_GROUP)
    e_log = jnp.where(in_grp, logits, NEG)
    v1, i1 = first_max(e_log)
    v2, i2 = first_max(jnp.where(lane == i1, NEG, e_log))
    t = jnp.exp(v2 - v1)
    w1 = g_w / (1.0 + t)
    w2 = g_w * t / (1.0 + t)
    j1 = i1 - e_lo
    j2 = i2 - e_lo
    lo = jnp.minimum(j1, j2)
    hi = jnp.maximum(j1, j2)
    pair = jnp.where(lo == 0, hi - 1, jnp.where(lo == 1, hi + 1, N_PAIRS - 1))
    bucket = g_sel * N_PAIRS + pair
    w_lo = jnp.where(j1 < j2, w1, w2)
    w_hi = jnp.where(j1 < j2, w2, w1)
    xe_ref[:, d:] = jnp.where(lane == 0, w_lo, jnp.where(lane == 1, w_hi, 0.0))
    bkt_ref[...] = bucket
    cnt_ref[0] = jnp.sum((lane == bucket).astype(F32), axis=0, keepdims=True)


def cross_block(x2d, y_hg, y_att, w_out, g_cross, wq, kx, vx, wo, g_ffn, w_router, batch):
    n, d = y_hg.shape[0], x2d.shape[1]
    tm = X_TILE
    nt = n // tm
    per_batch = nt // batch
    m = kx.shape[0] // batch
    row = lambda i: (i, 0)
    fix = lambda i: (0, 0)
    mem = lambda i: (i // per_batch, 0)
    return pl.pallas_call(
        _cross_kernel,
        grid=(nt,),
        in_specs=[pl.BlockSpec((tm, d), row), pl.BlockSpec((tm, y_hg.shape[1]), row),
                  pl.BlockSpec((tm, y_att.shape[1]), row), pl.BlockSpec(w_out.shape, fix),
                  pl.BlockSpec((1, d), fix), pl.BlockSpec(wq.shape, fix),
                  pl.BlockSpec((m, d), mem), pl.BlockSpec((m, d), mem), pl.BlockSpec(wo.shape, fix),
                  pl.BlockSpec((1, d), fix), pl.BlockSpec(w_router.shape, fix)],
        out_specs=[pl.BlockSpec((tm, d + LANES), row), pl.BlockSpec((tm, 1), row),
                   pl.BlockSpec((1, 1, LANES), lambda i: (i, 0, 0))],
        out_shape=[jax.ShapeDtypeStruct((n, d + LANES), F32), jax.ShapeDtypeStruct((n, 1), I32),
                   jax.ShapeDtypeStruct((nt, 1, LANES), F32)],
        compiler_params=_params(1),
        name="cross_block",
    )(x2d, y_hg, y_att, w_out, g_cross.reshape(1, d), wq, kx, vx, wo, g_ffn.reshape(1, d), w_router)


def _position_kernel(bkt_ref, base_ref, tri_ref, pos_ref):
    lane = lax.broadcasted_iota(I32, (bkt_ref.shape[0], LANES), 1)
    onehot = (lane == bkt_ref[...]).astype(F32)
    before = _dot(tri_ref[...], onehot.astype(BF16))
    pos = jnp.sum(onehot * (before + base_ref[0]), axis=-1, keepdims=True)
    pos_ref[...] = pos.astype(I32)


def sorted_positions(bucket, base):
    n = bucket.shape[0]
    nt = base.shape[0]
    tm = n // nt
    tri = jnp.asarray(np.tril(np.ones((tm, tm), np.float32), -1), BF16)
    return pl.pallas_call(
        _position_kernel,
        grid=(nt,),
        in_specs=[pl.BlockSpec((tm, 1), lambda i: (i, 0)),
                  pl.BlockSpec((1, 1, LANES), lambda i: (i, 0, 0)),
                  pl.BlockSpec((tm, tm), lambda i: (0, 0))],
        out_specs=pl.BlockSpec((tm, 1), lambda i: (i, 0)),
        out_shape=jax.ShapeDtypeStruct((n, 1), I32),
        compiler_params=_params(1),
        name="sorted_positions",
    )(bucket, base, tri)


def _moe_kernel(pos_ref, ea_ref, eb_ref, cnt_ref,
                x_hbm, g_ref, wga_ref, wgb_ref, wua_ref, wub_ref, wda_ref, wdb_ref, gfin_ref,
                out_hbm, gsrc_ref, sdst_ref, xbuf, obuf, gsem, ssem, *, final_norm, n_tokens):
    j = pl.program_id(0)
    nt = pl.num_programs(0)
    rows = xbuf.shape[1]
    d = obuf.shape[2]
    slot = j % 2
    other = 1 - slot

    def start_gather(tile, sl):
        for r in range(rows):
            tok = gsrc_ref[tile * rows + r]
            pltpu.make_async_copy(x_hbm.at[pl.ds(tok, 1)], xbuf.at[sl, pl.ds(r, 1)],
                                  gsem.at[sl]).start(priority=r % DMA_PRIORITIES)

    def start_scatter(tile, sl):
        for r in range(rows):
            dst = sdst_ref[tile * rows + r]
            pltpu.make_async_copy(obuf.at[sl, pl.ds(r, 1)], out_hbm.at[pl.ds(dst, 1)],
                                  ssem.at[sl]).start(priority=r % DMA_PRIORITIES)

    def start_spare_scatter(sl):
        for r in range(rows):
            pltpu.make_async_copy(obuf.at[sl, pl.ds(r, 1)], out_hbm.at[pl.ds(n_tokens + sl * rows + r, 1)],
                                  ssem.at[sl]).start()

    def wait_gather(sl):
        pltpu.make_async_copy(x_hbm.at[pl.ds(0, rows)], xbuf.at[sl], gsem.at[sl]).wait()

    def wait_scatter(sl):
        pltpu.make_async_copy(obuf.at[sl], out_hbm.at[pl.ds(0, rows)], ssem.at[sl]).wait()

    @pl.when(j == 0)
    def _():
        def pad_tile(tile, _):
            @pl.when(jnp.logical_or(cnt_ref[tile] > 0, cnt_ref[jnp.maximum(tile - 1, 0)] > 0))
            def _():
                def pad_row(r, _):
                    gsrc_ref[tile * rows + r] = 0
                    sdst_ref[tile * rows + r] = n_tokens + (tile % 2) * rows + r
                    return 0
                lax.fori_loop(cnt_ref[tile], rows, pad_row, 0)
            return 0

        def fill(t, _):
            gsrc_ref[pos_ref[t]] = t
            sdst_ref[pos_ref[t]] = t
            return 0

        lax.fori_loop(0, nt, pad_tile, 0)
        lax.fori_loop(0, pos_ref.shape[0], fill, 0, unroll=8)
        obuf[...] = jnp.zeros_like(obuf)
        start_gather(0, 0)
        start_spare_scatter(0)
        start_spare_scatter(1)

    used = cnt_ref[j] > 0
    last_used = jnp.logical_and(used, jnp.logical_or(j == nt - 1, cnt_ref[jnp.minimum(j + 1, nt - 1)] == 0))

    for static_slot in range(2):
        @pl.when(jnp.logical_and(used, slot == static_slot))
        def _():
            wait_gather(static_slot)
            wait_scatter(static_slot)
            start_gather(jnp.minimum(j + 1, nt - 1), 1 - static_slot)

            @pl.when(j > 0)
            def _():
                start_scatter(j - 1, 1 - static_slot)

    @pl.when(cnt_ref[j] > 0)
    def _():
        xe = xbuf[slot]
        x = xe[:, :d]
        h = _rms(x, g_ref[...]).astype(BF16)
        lane = lax.broadcasted_iota(I32, (rows, LANES), 1)
        wts = xe[:, d:]
        w_a = jnp.sum(jnp.where(lane == 0, wts, 0.0), axis=-1, keepdims=True)
        w_b = jnp.sum(jnp.where(lane == 1, wts, 0.0), axis=-1, keepdims=True)
        act_a = (jax.nn.silu(_dot(h, wga_ref[...])) * _dot(h, wua_ref[...]) * w_a).astype(BF16)
        act_b = (jax.nn.silu(_dot(h, wgb_ref[...])) * _dot(h, wub_ref[...]) * w_b).astype(BF16)
        y = x + _dot(act_a, wda_ref[...]) + _dot(act_b, wdb_ref[...])
        if final_norm:
            y = _rms(y, gfin_ref[...])
        obuf[slot] = y

    @pl.when(last_used)
    def _():
        start_scatter(j, slot)
        wait_scatter(other)
        wait_scatter(slot)
        wait_gather(other)


def moe_ffn(x_ext, pos, tile_a, tile_b, tile_cnt, g_ffn, w_gate, w_up, w_down, g_final, final_norm):
    n, de = x_ext.shape
    d = de - LANES
    rows = MOE_TILE
    nt = tile_cnt.shape[0]
    ff = w_gate.shape[2]
    fix = lambda j, *_: (0, 0)
    exp_a = lambda j, pos, ea, eb, cnt: (ea[j], 0, 0)
    exp_b = lambda j, pos, ea, eb, cnt: (eb[j], 0, 0)
    up_spec = lambda im: pl.BlockSpec((None, d, ff), im)
    down_spec = lambda im: pl.BlockSpec((None, ff, d), im)
    grid_spec = pltpu.PrefetchScalarGridSpec(
        num_scalar_prefetch=4,
        grid=(nt,),
        in_specs=[pl.BlockSpec(memory_space=pl.ANY), pl.BlockSpec((1, d), fix),
                  up_spec(exp_a), up_spec(exp_b), up_spec(exp_a), up_spec(exp_b),
                  down_spec(exp_a), down_spec(exp_b), pl.BlockSpec((1, d), fix)],
        out_specs=pl.BlockSpec(memory_space=pl.ANY),
        scratch_shapes=[pltpu.SMEM((nt * rows,), I32), pltpu.SMEM((nt * rows,), I32),
                        pltpu.VMEM((2, rows, de), F32), pltpu.VMEM((2, rows, d), F32),
                        pltpu.SemaphoreType.DMA((2,)), pltpu.SemaphoreType.DMA((2,))],
    )
    return pl.pallas_call(
        functools.partial(_moe_kernel, final_norm=final_norm, n_tokens=n),
        grid_spec=grid_spec,
        out_shape=jax.ShapeDtypeStruct((n + 2 * rows, d), F32),
        compiler_params=_params(1),
        name="moe_ffn",
    )(pos, tile_a, tile_b, tile_cnt, x_ext, g_ffn.reshape(1, d),
      w_gate, w_gate, w_up, w_up, w_down, w_down, g_final.reshape(1, d))


_PAIR_LO = np.array([0, 0, 0, 1, 1, 2], np.int32)
_PAIR_HI = np.array([1, 2, 3, 2, 3, 3], np.int32)


def _tile_plan(counts, n_tokens):
    rows = MOE_TILE
    counts = counts.astype(I32)
    total = jnp.sum(counts, axis=0)
    padded = (total + rows - 1) // rows * rows
    start = jnp.cumsum(padded) - padded
    base = start[None, :] + jnp.cumsum(counts, axis=0) - counts
    nt = n_tokens // rows + N_BUCKETS
    first_row = jnp.arange(nt, dtype=I32) * rows
    end = (start + padded)[:N_BUCKETS]
    bucket = jnp.sum((first_row[:, None] >= end[None, :]).astype(I32), axis=1)
    used = bucket < N_BUCKETS
    last_used = jnp.max(jnp.where(used, bucket, 0))
    b_eff = jnp.where(used, bucket, last_used)
    cnt = jnp.where(used, jnp.clip(total[b_eff] - (first_row - start[b_eff]), 0, rows), 0)
    grp = b_eff // N_PAIRS
    pair = b_eff % N_PAIRS
    exp_a = grp * EXPERTS_PER_GROUP + jnp.asarray(_PAIR_LO)[pair]
    exp_b = grp * EXPERTS_PER_GROUP + jnp.asarray(_PAIR_HI)[pair]
    return base.astype(F32), exp_a.astype(I32), exp_b.astype(I32), cnt.astype(I32)


def kernel(x, mem, positions, norm_mix, w_in, hg_lower_bounds, hg_out_norm, w_out, norm_cross, norm_mem,
           wq_x, wkv_x, wo_x, norm_ffn, w_router_group, w_router_expert, w_gate, w_up, w_down, norm_final):
    batch, seq, d = x.shape
    depth = w_in.shape[0]
    n = batch * seq
    hg_width = hg_lower_bounds.shape[1]
    att_width = (w_in.shape[2] - 4 * hg_width) // 3
    assert hg_width == HG_HEADS * HG_DK and seq % ATT_SUPER == 0 and n % TOK_TILE == 0

    lb_sm = jax.nn.softmax(hg_lower_bounds.astype(F32), axis=0)
    lbs = jnp.cumsum(lb_sm, axis=0) - lb_sm[0:1]
    cos, sin = rope_tables(positions)
    xs = x.reshape(n, d)
    mem2d = mem.reshape(-1, d)
    w_router = jnp.concatenate([w_router_group, w_router_expert], axis=-1)
    w_router = jnp.pad(w_router, ((0, 0), (0, 0), (0, LANES - w_router.shape[-1])))
    wg16 = w_gate.astype(BF16).reshape((-1,) + w_gate.shape[2:])
    wu16 = w_up.astype(BF16).reshape((-1,) + w_up.shape[2:])
    wd16 = w_down.astype(BF16).reshape((-1,) + w_down.shape[2:])

    for l in range(depth):
        hproj, aq, ak, av = in_projection(xs, norm_mix[l], w_in[l].astype(BF16), cos, sin, hg_width, att_width)
        y_hg = hgrn_mixer(hproj, lbs[l], hg_out_norm[l], batch, seq)
        y_att = dilated_attention(aq, ak, av, batch, seq)
        kx, vx = memory_kv(mem2d, norm_mem[l], wkv_x[l].astype(BF16), batch)
        x_ext, bucket, counts = cross_block(xs, y_hg, y_att, w_out[l].astype(BF16), norm_cross[l],
                                            wq_x[l].astype(BF16), kx, vx, wo_x[l].astype(BF16),
                                            norm_ffn[l], w_router[l], batch)
        base, exp_a, exp_b, cnt = _tile_plan(counts.reshape(n // POS_TILE, -1, LANES).sum(axis=1), n)
        pos = sorted_positions(bucket, base.reshape(base.shape[0], 1, LANES))
        xs = moe_ffn(x_ext, pos.reshape(n), exp_a + l * N_EXPERTS, exp_b + l * N_EXPERTS, cnt, norm_ffn[l],
                     wg16, wu16, wd16, norm_final, final_norm=(l == depth - 1))
    return xs[:n].reshape(batch, seq, d)
```

```python
import functools

import numpy as np
import jax
import jax.numpy as jnp
from jax import lax
from jax.experimental import pallas as pl
from jax.experimental.pallas import tpu as pltpu

F32 = jnp.float32
BF16 = jnp.bfloat16
I32 = jnp.int32

NORM_EPS = 1e-6
ROPE_THETA = 10000.0
HG_HEADS = 4
HG_DK = 128
ATT_DH = 64
ATT_BLK = 128
ATT_DILATIONS = (1, 4, 16)
ATT_CLASSES = 16
ATT_SUPER = ATT_BLK * ATT_CLASSES
X_HEADS = 4
N_GROUPS = 4
EXPERTS_PER_GROUP = 4
N_EXPERTS = 16
N_PAIRS = 6
N_BUCKETS = N_GROUPS * N_PAIRS
EXPERT_FF = 512
LANES = 128
SUBLANES = 8
NEG = -1e30
LOG2_E = 1.4426950408889634

HG_T = 128
HG_STEP = 256
TOK_TILE = 512
X_TILE = 512
POS_TILE = 1024
MOE_TILE = 256
DMA_PRIORITIES = 2
COPY_GROUP = 32
VMEM_LIMIT = 56 * 1024 * 1024


def _dot(a, b):
    return jnp.dot(a, b, preferred_element_type=F32)


def _dot_nt(a, b):
    return lax.dot_general(a, b, (((1,), (1,)), ((), ())), preferred_element_type=F32)


def _dot_tn(a, b):
    return lax.dot_general(a, b, (((0,), (0,)), ((), ())), preferred_element_type=F32)


def _rms(x, g):
    return x * lax.rsqrt(jnp.mean(x * x, axis=-1, keepdims=True) + NORM_EPS) * g


def _split_bf16(a):
    hi = a.astype(BF16)
    return hi, (a - hi.astype(F32)).astype(BF16)


def _params(n_axes):
    return pltpu.CompilerParams(dimension_semantics=("arbitrary",) * n_axes,
                                vmem_limit_bytes=VMEM_LIMIT)


def _rope_kernel(pos_ref, inv_ref, cos_ref, sin_ref):
    ang = pos_ref[...].astype(F32) * inv_ref[...]
    lane = lax.broadcasted_iota(I32, ang.shape, 1)
    s = jnp.sin(ang)
    cos_ref[...] = jnp.cos(ang)
    sin_ref[...] = jnp.where((lane % ATT_DH) < ATT_DH // 2, -s, s)


def rope_tables(positions):
    n = positions.size
    inv = ROPE_THETA ** (-jnp.arange(0, ATT_DH, 2, dtype=F32) / ATT_DH)
    inv = jnp.tile(inv, LANES // (ATT_DH // 2)).reshape(1, LANES)
    tm = TOK_TILE
    return pl.pallas_call(
        _rope_kernel,
        grid=(n // tm,),
        in_specs=[pl.BlockSpec((tm, 1), lambda i: (i, 0)),
                  pl.BlockSpec((1, LANES), lambda i: (0, 0))],
        out_specs=[pl.BlockSpec((tm, LANES), lambda i: (i, 0))] * 2,
        out_shape=[jax.ShapeDtypeStruct((n, LANES), F32)] * 2,
        compiler_params=_params(1),
        name="rope_tables",
    )(positions.reshape(n, 1), inv)


def _inproj_kernel(x_ref, g_ref, w_ref, cos_ref, sin_ref, hp_ref, q_ref, k_ref, v_ref):
    h = _rms(x_ref[...], g_ref[...]).astype(BF16)
    hgw = hp_ref.shape[1]
    aw = q_ref.shape[1]
    for c in range(hgw // aw):
        hp_ref[:, c * aw:(c + 1) * aw] = _dot(h, w_ref[:, c * aw:(c + 1) * aw])
    cos = cos_ref[...]
    sin = sin_ref[...]
    lane = lax.broadcasted_iota(I32, cos.shape, 1)
    first = (lane % ATT_DH) < ATT_DH // 2

    def rope(t, scale):
        for p in range(aw // LANES):
            tp = t[:, p * LANES:(p + 1) * LANES]
            rot = jnp.where(first, pltpu.roll(tp, LANES - ATT_DH // 2, axis=1),
                            pltpu.roll(tp, ATT_DH // 2, axis=1))
            yield (tp * cos + rot * sin) * scale

    aq = _dot(h, w_ref[:, hgw:hgw + aw])
    for p, blk in enumerate(rope(aq, ATT_DH ** -0.5 * LOG2_E)):
        q_ref[:, p * LANES:(p + 1) * LANES] = blk
    ak = _dot(h, w_ref[:, hgw + aw:hgw + 2 * aw])
    for p, blk in enumerate(rope(ak, 1.0)):
        k_ref[:, p * LANES:(p + 1) * LANES] = blk
    v_ref[...] = _dot(h, w_ref[:, hgw + 2 * aw:hgw + 3 * aw])


def in_projection(x2d, gain, w_in, cos, sin, hg_width, att_width):
    n, d = cos.shape[0], x2d.shape[1]
    tm = TOK_TILE
    row = lambda i: (i, 0)
    fix = lambda i: (0, 0)
    return pl.pallas_call(
        _inproj_kernel,
        grid=(n // tm,),
        in_specs=[pl.BlockSpec((tm, d), row), pl.BlockSpec((1, d), fix),
                  pl.BlockSpec(w_in.shape, fix),
                  pl.BlockSpec((tm, LANES), row), pl.BlockSpec((tm, LANES), row)],
        out_specs=[pl.BlockSpec((tm, 4 * hg_width), row)] + [pl.BlockSpec((tm, att_width), row)] * 3,
        out_shape=[jax.ShapeDtypeStruct((n, 4 * hg_width), F32)]
        + [jax.ShapeDtypeStruct((n, att_width), F32)] * 3,
        compiler_params=_params(1),
        name="in_projection",
    )(x2d, gain.reshape(1, d), w_in, cos, sin)


def _level_table(t):
    ti = np.arange(t)[:, None]
    si = np.arange(t)[None, :]
    x = np.maximum(ti ^ si, 1)
    lvl = np.floor(np.log2(x)).astype(np.int32)
    diag = int(np.log2(t))
    return np.where(si < ti, lvl, np.where(si == ti, diag, -1)).astype(np.int32)


def _hgrn_kernel(q_ref, f_ref, i_ref, gt_ref, lb_ref, gain_ref, lvl_ref, y_ref, st_ref):
    t_rows = lvl_ref.shape[0]
    n_lev = t_rows.bit_length() - 1

    @pl.when(pl.program_id(1) == 0)
    def _():
        st_ref[...] = jnp.zeros_like(st_ref)

    row = lax.broadcasted_iota(I32, (t_rows, HG_DK), 0)
    lvl = lvl_ref[...]
    lb_all = lb_ref[...]
    for u, h in [(u, h) for u in range(q_ref.shape[0] // t_rows) for h in range(HG_HEADS)]:
        rs = slice(u * t_rows, (u + 1) * t_rows)
        sl = slice(h * HG_DK, (h + 1) * HG_DK)
        lb = lb_all[:, sl]
        f = lb + (1.0 - lb) * jax.nn.sigmoid(f_ref[rs, sl])
        g = jnp.log(f) * LOG2_E
        b = g
        s = 1
        while s < t_rows:
            b = b + jnp.where(row >= s, pltpu.roll(b, s, axis=0), 0.0)
            s *= 2
        kk = 1.0 - f
        q = q_ref[rs, sl]
        v16 = i_ref[rs, sl].astype(BF16)
        scores = jnp.where(lvl == n_lev, _dot_nt(q.astype(BF16), kk.astype(BF16)), 0.0)
        first = b - g
        last = b
        for j in range(n_lev):
            if j > 0:
                half = 1 << (j - 1)
                bit = ((row >> (j - 1)) & 1) == 1
                first = jnp.where(bit, pltpu.roll(first, half, axis=0), first)
                last = jnp.where(bit, last, pltpu.roll(last, t_rows - half, axis=0))
            qj = (q * jnp.exp2(b - first)).astype(BF16)
            kj = (kk * jnp.exp2(last - b)).astype(BF16)
            scores = jnp.where(lvl == j, _dot_nt(qj, kj), scores)
        st = st_ref[h]
        o = _dot(scores.astype(BF16), v16) + _dot_nt((q * jnp.exp2(b)).astype(BF16), st.astype(BF16))
        b_last = b[t_rows - 1:t_rows, :]
        kdec = (kk * jnp.exp2(b_last - b)).astype(BF16)
        st_ref[h] = jnp.exp2(b_last) * st + _dot_tn(v16, kdec)
        o = o * lax.rsqrt(jnp.mean(o * o, axis=-1, keepdims=True) + NORM_EPS) * gain_ref[:, sl]
        y_ref[rs, sl] = o * jax.nn.silu(gt_ref[rs, sl])


def hgrn_mixer(hproj, lb, out_gain, batch, seq):
    n, w4 = hproj.shape
    w = w4 // 4
    t = HG_STEP
    nb = seq // t
    col = lambda c: (lambda b, i: (b * nb + i, c))
    fix = lambda b, i: (0, 0)
    lvl = jnp.asarray(_level_table(HG_T))
    return pl.pallas_call(
        _hgrn_kernel,
        grid=(batch, nb),
        in_specs=[pl.BlockSpec((t, w), col(0)), pl.BlockSpec((t, w), col(1)),
                  pl.BlockSpec((t, w), col(2)), pl.BlockSpec((t, w), col(3)),
                  pl.BlockSpec((1, w), fix), pl.BlockSpec((1, w), fix),
                  pl.BlockSpec((HG_T, HG_T), fix)],
        out_specs=pl.BlockSpec((t, w), col(0)),
        out_shape=jax.ShapeDtypeStruct((n, w), F32),
        scratch_shapes=[pltpu.VMEM((HG_HEADS, HG_DK, HG_DK), F32)],
        compiler_params=_params(2),
        name="hgrn_mixer",
    )(hproj, hproj, hproj, hproj, lb.reshape(1, w), out_gain.reshape(1, w), lvl)


def _att_masks():
    blk = ATT_BLK
    rows = np.arange(blk)
    out = []
    for dil in ATT_DILATIONS:
        seg = blk * dil // ATT_CLASSES
        pos = (ATT_CLASSES // dil) * (rows % seg) + rows // seg
        kpos = np.concatenate([pos, blk + pos])
        dist = pos[:, None] + blk - kpos[None, :]
        out.append(((dist >= 0) & (dist <= blk)).astype(np.float32))
    return np.stack(out)


def _att_kernel(q_hbm, k_hbm, v_hbm, mask_ref, o_hbm,
                qbuf, kbuf, vbuf, acc, m_s, l_s, obuf, isem, osem):
    blk = ATT_BLK
    ncls = ATT_CLASSES
    npair = pl.num_programs(1)
    nsb = pl.num_programs(2)
    sb = pl.program_id(2)
    step = (pl.program_id(0) * npair + pl.program_id(1)) * nsb + sb
    n_steps = pl.num_programs(0) * npair * nsb
    lane = lax.broadcasted_iota(I32, (blk, LANES), 1)
    head0 = lane < ATT_DH
    col = lax.broadcasted_iota(I32, (blk, 2 * blk), 1)
    cur = step % 3
    prev = (step + 2) % 3

    def load_copies(s):
        row0 = pl.multiple_of(((s // (npair * nsb)) * nsb + s % nsb) * blk, blk)
        lane0 = pl.multiple_of(((s // nsb) % npair) * LANES, LANES)
        for r in range(ncls):
            dst = pl.ds(r * blk, blk)
            for hbm, buf, slot in ((q_hbm, qbuf, s % 2), (k_hbm, kbuf, s % 3), (v_hbm, vbuf, s % 3)):
                yield pltpu.make_async_copy(hbm.at[pl.ds(row0, blk), r, pl.ds(lane0, LANES)],
                                            buf.at[slot, dst, :], isem.at[s % 2])

    def store_copies(s):
        row0 = pl.multiple_of(((s // (npair * nsb)) * nsb + s % nsb) * blk, blk)
        lane0 = pl.multiple_of(((s // nsb) % npair) * LANES, LANES)
        for r in range(ncls):
            yield pltpu.make_async_copy(obuf.at[pl.ds(r * blk, blk), :],
                                        o_hbm.at[pl.ds(row0, blk), r, pl.ds(lane0, LANES)], osem.at[0])

    @pl.when(step == 0)
    def _():
        kbuf[2] = jnp.zeros(kbuf.shape[1:], kbuf.dtype)
        vbuf[2] = jnp.zeros(vbuf.shape[1:], vbuf.dtype)
        for cp in load_copies(step):
            cp.start()

    @pl.when(step + 1 < n_steps)
    def _():
        for cp in load_copies(step + 1):
            cp.start()

    for cp in load_copies(step):
        cp.wait()
    qs = step % 2

    def rows_of(buf, pieces, seg):
        return jnp.concatenate([buf[slot, pl.ds(o, seg), :] for slot, o in pieces], axis=0).astype(BF16)

    def block_attention(qb, kb, vb, valid):
        outs = []
        for hh in range(2):
            hm = head0 if hh == 0 else jnp.logical_not(head0)
            s = _dot_nt(jnp.where(hm, qb, jnp.zeros_like(qb)), kb)
            s = jnp.where(valid, s, NEG)
            m = jnp.max(s, axis=-1, keepdims=True)
            p = jnp.exp2(s - m)
            l = jnp.sum(p, axis=-1, keepdims=True)
            outs.append((_dot(p.astype(BF16), vb), m, l))
        (o0, m0, l0), (o1, m1, l1) = outs
        return (jnp.where(head0, o0, o1), jnp.where(head0, m0, m1), jnp.where(head0, l0, l1))

    def first_ok(is_first):
        return jnp.logical_or(col >= blk, jnp.logical_not(is_first))

    def pattern(p, dil):
        band = mask_ref[p] > 0.5
        nseg = ncls // dil
        seg = blk // nseg

        def body(idx, _):
            c = idx // nseg
            n = idx % nseg
            valid = jnp.logical_and(band, first_ok(jnp.logical_and(sb == 0, n == 0)))
            slot_p = jnp.where(n == 0, prev, cur)
            n_p = (n + nseg - 1) % nseg
            q_off = [pl.multiple_of((c + dil * k) * blk + seg * n, seg) for k in range(nseg)]
            p_off = [pl.multiple_of((c + dil * k) * blk + seg * n_p, seg) for k in range(nseg)]
            keys = [(slot_p, o) for o in p_off] + [(cur, o) for o in q_off]
            o, m, l = block_attention(rows_of(qbuf, [(qs, o) for o in q_off], seg),
                                      rows_of(kbuf, keys, seg), rows_of(vbuf, keys, seg), valid)
            for k, off in enumerate(q_off):
                rs = slice(k * seg, (k + 1) * seg)
                dst = pl.ds(off, seg)
                acc[p, dst, :] = o[rs]
                m_s[p, dst, :] = m[rs]
                l_s[p, dst, :] = l[rs]
            return 0

        lax.fori_loop(0, ncls, body, 0, unroll=True)

    for p, dil in enumerate(ATT_DILATIONS):
        pattern(p, dil)

    @pl.when(step > 0)
    def _():
        for cp in store_copies(step - 1):
            cp.wait()

    def combine(i, _):
        rows = pl.ds(pl.multiple_of(i * blk, blk), blk)
        ms = [m_s[p, rows, :] for p in range(len(ATT_DILATIONS))]
        top = functools.reduce(jnp.maximum, ms)
        ws = [jnp.exp2(m - top) for m in ms]
        num = sum(w * acc[p, rows, :] for p, w in enumerate(ws))
        den = sum(w * l_s[p, rows, :] for p, w in enumerate(ws))
        obuf[rows, :] = num / den
        return 0

    lax.fori_loop(0, ncls, combine, 0, unroll=2)
    for cp in store_copies(step):
        cp.start()

    @pl.when(step == n_steps - 1)
    def _():
        for cp in store_copies(step):
            cp.wait()


def dilated_attention(aq, ak, av, batch, seq):
    n, w = aq.shape
    nsb = seq // ATT_SUPER
    npair = w // LANES
    masks = jnp.asarray(_att_masks())
    view = lambda t: t.reshape(n // ATT_CLASSES, ATT_CLASSES, w)
    any_spec = pl.BlockSpec(memory_space=pl.ANY)
    out = pl.pallas_call(
        _att_kernel,
        grid=(batch, npair, nsb),
        in_specs=[any_spec, any_spec, any_spec, pl.BlockSpec(masks.shape, lambda b, hp, sb: (0, 0, 0))],
        out_specs=any_spec,
        out_shape=jax.ShapeDtypeStruct((n // ATT_CLASSES, ATT_CLASSES, w), F32),
        scratch_shapes=[pltpu.VMEM((2, ATT_SUPER, LANES), F32),
                        pltpu.VMEM((3, ATT_SUPER, LANES), F32),
                        pltpu.VMEM((3, ATT_SUPER, LANES), F32)]
        + [pltpu.VMEM((len(ATT_DILATIONS), ATT_SUPER, LANES), F32)] * 3
        + [pltpu.VMEM((ATT_SUPER, LANES), F32),
           pltpu.SemaphoreType.DMA((2,)), pltpu.SemaphoreType.DMA((1,))],
        compiler_params=_params(3),
        name="dilated_attention",
    )(view(aq), view(ak), view(av), masks)
    return out.reshape(n, w)


def _memkv_kernel(mem_ref, g_ref, w_ref, k_ref, v_ref):
    h = _rms(mem_ref[...], g_ref[...]).astype(BF16)
    d = k_ref.shape[1]
    k_ref[...] = _dot(h, w_ref[:, :d]).astype(BF16)
    v_ref[...] = _dot(h, w_ref[:, d:]).astype(BF16)


def memory_kv(mem2d, gain, wkv, batch):
    n, d = mem2d.shape
    m = n // batch
    row = lambda b: (b, 0)
    fix = lambda b: (0, 0)
    return pl.pallas_call(
        _memkv_kernel,
        grid=(batch,),
        in_specs=[pl.BlockSpec((m, d), row), pl.BlockSpec((1, d), fix), pl.BlockSpec(wkv.shape, fix)],
        out_specs=[pl.BlockSpec((m, d), row)] * 2,
        out_shape=[jax.ShapeDtypeStruct((n, d), BF16)] * 2,
        compiler_params=_params(1),
        name="memory_kv",
    )(mem2d, gain.reshape(1, d), wkv)


def _cross_kernel(x_ref, yh_ref, ya_ref, wout_ref, gx_ref, wq_ref, kx_ref, vx_ref, wo_ref,
                  gf_ref, wr_ref, xe_ref, bkt_ref, cnt_ref):
    d = x_ref.shape[1]
    hw = yh_ref.shape[1]
    x = (x_ref[...] + _dot(yh_ref[...].astype(BF16), wout_ref[:hw, :])
         + _dot(ya_ref[...].astype(BF16), wout_ref[hw:, :]))
    h = _rms(x, gx_ref[...]).astype(BF16)
    q = _dot(h, wq_ref[...])
    dh = d // X_HEADS
    heads = []
    for hh in range(X_HEADS):
        sl = slice(hh * dh, (hh + 1) * dh)
        s = _dot_nt(q[:, sl].astype(BF16), kx_ref[:, sl]) * (dh ** -0.5)
        s = s - jnp.max(s, axis=-1, keepdims=True)
        p = jnp.exp(s)
        p = p / jnp.sum(p, axis=-1, keepdims=True)
        heads.append(_dot(p.astype(BF16), vx_ref[:, sl]).astype(BF16))
    x = x + _dot(jnp.concatenate(heads, axis=1), wo_ref[...])
    xe_ref[:, :d] = x

    hf = _rms(x, gf_ref[...])
    h_hi, h_lo = _split_bf16(hf)
    w_hi, w_lo = _split_bf16(wr_ref[...])
    both = _dot(h_hi, jnp.concatenate([w_hi, w_lo], axis=1))
    logits = both[:, :LANES] + both[:, LANES:] + _dot(h_lo, w_hi)
    lane = lax.broadcasted_iota(I32, logits.shape, 1)
    big = 1 << 20

    def first_max(vals):
        top = jnp.max(vals, axis=-1, keepdims=True)
        return top, jnp.min(jnp.where(vals == top, lane, big), axis=-1, keepdims=True)

    is_g = lane < N_GROUPS
    g_top, g_sel = first_max(jnp.where(is_g, logits, NEG))
    g_w = 1.0 / jnp.sum(jnp.where(is_g, jnp.exp(logits - g_top), 0.0), axis=-1, keepdims=True)
    e_lo = N_GROUPS + EXPERTS_PER_GROUP * g_sel
    in_grp = jnp.logical_and(lane >= e_lo, lane < e_lo + EXPERTS_PER_GROUP)
    e_log = jnp.where(in_grp, logits, NEG)
    v1, i1 = first_max(e_log)
    v2, i2 = first_max(jnp.where(lane == i1, NEG, e_log))
    t = jnp.exp(v2 - v1)
    w1 = g_w / (1.0 + t)
    w2 = g_w * t / (1.0 + t)
    j1 = i1 - e_lo
    j2 = i2 - e_lo
    lo = jnp.minimum(j1, j2)
    hi = jnp.maximum(j1, j2)
    pair = jnp.where(lo == 0, hi - 1, jnp.where(lo == 1, hi + 1, N_PAIRS - 1))
    bucket = g_sel * N_PAIRS + pair
    w_lo = jnp.where(j1 < j2, w1, w2)
    w_hi = jnp.where(j1 < j2, w2, w1)
    xe_ref[:, d:] = jnp.where(lane == 0, w_lo, jnp.where(lane == 1, w_hi, 0.0))
    bkt_ref[...] = bucket
    cnt_ref[0] = jnp.sum((lane == bucket).astype(F32), axis=0, keepdims=True)


def cross_block(x2d, y_hg, y_att, w_out, g_cross, wq, kx, vx, wo, g_ffn, w_router, batch):
    n, d = y_hg.shape[0], x2d.shape[1]
    tm = X_TILE
    nt = n // tm
    per_batch = nt // batch
    m = kx.shape[0] // batch
    row = lambda i: (i, 0)
    fix = lambda i: (0, 0)
    mem = lambda i: (i // per_batch, 0)
    return pl.pallas_call(
        _cross_kernel,
        grid=(nt,),
        in_specs=[pl.BlockSpec((tm, d), row), pl.BlockSpec((tm, y_hg.shape[1]), row),
                  pl.BlockSpec((tm, y_att.shape[1]), row), pl.BlockSpec(w_out.shape, fix),
                  pl.BlockSpec((1, d), fix), pl.BlockSpec(wq.shape, fix),
                  pl.BlockSpec((m, d), mem), pl.BlockSpec((m, d), mem), pl.BlockSpec(wo.shape, fix),
                  pl.BlockSpec((1, d), fix), pl.BlockSpec(w_router.shape, fix)],
        out_specs=[pl.BlockSpec((tm, d + LANES), row), pl.BlockSpec((tm, 1), row),
                   pl.BlockSpec((1, 1, LANES), lambda i: (i, 0, 0))],
        out_shape=[jax.ShapeDtypeStruct((n, d + LANES), F32), jax.ShapeDtypeStruct((n, 1), I32),
                   jax.ShapeDtypeStruct((nt, 1, LANES), F32)],
        compiler_params=_params(1),
        name="cross_block",
    )(x2d, y_hg, y_att, w_out, g_cross.reshape(1, d), wq, kx, vx, wo, g_ffn.reshape(1, d), w_router)


def _position_kernel(bkt_ref, base_ref, tri_ref, pos_ref):
    lane = lax.broadcasted_iota(I32, (bkt_ref.shape[0], LANES), 1)
    onehot = (lane == bkt_ref[...]).astype(F32)
    before = _dot(tri_ref[...], onehot.astype(BF16))
    pos = jnp.sum(onehot * (before + base_ref[0]), axis=-1, keepdims=True)
    pos_ref[...] = pos.astype(I32)


def sorted_positions(bucket, base):
    n = bucket.shape[0]
    nt = base.shape[0]
    tm = n // nt
    tri = jnp.asarray(np.tril(np.ones((tm, tm), np.float32), -1), BF16)
    return pl.pallas_call(
        _position_kernel,
        grid=(nt,),
        in_specs=[pl.BlockSpec((tm, 1), lambda i: (i, 0)),
                  pl.BlockSpec((1, 1, LANES), lambda i: (i, 0, 0)),
                  pl.BlockSpec((tm, tm), lambda i: (0, 0))],
        out_specs=pl.BlockSpec((tm, 1), lambda i: (i, 0)),
        out_shape=jax.ShapeDtypeStruct((n, 1), I32),
        compiler_params=_params(1),
        name="sorted_positions",
    )(bucket, base, tri)


def _moe_kernel(pos_ref, ea_ref, eb_ref, cnt_ref,
                x_hbm, g_ref, wga_ref, wgb_ref, wua_ref, wub_ref, wda_ref, wdb_ref, gfin_ref,
                out_hbm, gsrc_ref, sdst_ref, xbuf, obuf, gsem, ssem, *, final_norm, n_tokens):
    j = pl.program_id(0)
    nt = pl.num_programs(0)
    rows = xbuf.shape[1]
    d = obuf.shape[2]
    slot = j % 2
    other = 1 - slot

    def for_row_groups(n, fn):
        for g in range(rows // COPY_GROUP):
            @pl.when(g * COPY_GROUP < n)
            def _():
                for r in range(g * COPY_GROUP, (g + 1) * COPY_GROUP):
                    fn(r)

    def copied_rows(n):
        return pl.multiple_of((n + COPY_GROUP - 1) // COPY_GROUP * COPY_GROUP, COPY_GROUP)

    def start_gather(tile, n, sl):
        def copy(r):
            tok = gsrc_ref[tile * rows + r]
            pltpu.make_async_copy(x_hbm.at[pl.ds(tok, 1)], xbuf.at[sl, pl.ds(r, 1)],
                                  gsem.at[sl]).start(priority=r % DMA_PRIORITIES)
        for_row_groups(n, copy)

    def start_scatter(tile, n, sl):
        def copy(r):
            dst = sdst_ref[tile * rows + r]
            pltpu.make_async_copy(obuf.at[sl, pl.ds(r, 1)], out_hbm.at[pl.ds(dst, 1)],
                                  ssem.at[sl]).start(priority=r % DMA_PRIORITIES)
        for_row_groups(n, copy)

    def wait_gather(n, sl):
        m = copied_rows(n)

        @pl.when(m > 0)
        def _():
            pltpu.make_async_copy(x_hbm.at[pl.ds(0, m)], xbuf.at[sl, pl.ds(0, m)], gsem.at[sl]).wait()

    def wait_scatter(n, sl):
        m = copied_rows(n)

        @pl.when(m > 0)
        def _():
            pltpu.make_async_copy(obuf.at[sl, pl.ds(0, m)], out_hbm.at[pl.ds(0, m)], ssem.at[sl]).wait()

    @pl.when(j == 0)
    def _():
        def pad_tile(tile, _):
            @pl.when(cnt_ref[tile] > 0)
            def _():
                def pad_row(r, _):
                    gsrc_ref[tile * rows + r] = 0
                    sdst_ref[tile * rows + r] = n_tokens + (tile % 2) * rows + r
                    return 0
                lax.fori_loop(cnt_ref[tile], copied_rows(cnt_ref[tile]), pad_row, 0)
            return 0

        def fill(t, _):
            gsrc_ref[pos_ref[t]] = t
            sdst_ref[pos_ref[t]] = t
            return 0

        lax.fori_loop(0, nt, pad_tile, 0)
        lax.fori_loop(0, pos_ref.shape[0], fill, 0, unroll=8)
        xbuf[...] = jnp.zeros_like(xbuf)
        start_gather(0, cnt_ref[0], 0)
        obuf[...] = jnp.zeros_like(obuf)
        spare = [pltpu.make_async_copy(obuf.at[s], out_hbm.at[pl.ds(n_tokens + s * rows, rows)], ssem.at[s])
                 for s in range(2)]
        for cp in spare:
            cp.start()
        for cp in spare:
            cp.wait()

    used = cnt_ref[j] > 0
    nxt = jnp.minimum(j + 1, nt - 1)
    n_next = jnp.where(j + 1 < nt, cnt_ref[nxt], 0)
    n_prev = cnt_ref[jnp.maximum(j - 1, 0)]
    last_used = jnp.logical_and(used, n_next == 0)

    for static_slot in range(2):
        @pl.when(jnp.logical_and(used, slot == static_slot))
        def _():
            wait_gather(cnt_ref[j], static_slot)

            @pl.when(j >= 2)
            def _():
                wait_scatter(cnt_ref[jnp.maximum(j - 2, 0)], static_slot)

            start_gather(nxt, n_next, 1 - static_slot)

            @pl.when(j >= 1)
            def _():
                start_scatter(j - 1, n_prev, 1 - static_slot)

    @pl.when(cnt_ref[j] > 0)
    def _():
        xe = xbuf[slot]
        x = xe[:, :d]
        h = _rms(x, g_ref[...]).astype(BF16)
        lane = lax.broadcasted_iota(I32, (rows, LANES), 1)
        wts = xe[:, d:]
        w_a = jnp.sum(jnp.where(lane == 0, wts, 0.0), axis=-1, keepdims=True)
        w_b = jnp.sum(jnp.where(lane == 1, wts, 0.0), axis=-1, keepdims=True)
        act_a = (jax.nn.silu(_dot(h, wga_ref[...])) * _dot(h, wua_ref[...]) * w_a).astype(BF16)
        act_b = (jax.nn.silu(_dot(h, wgb_ref[...])) * _dot(h, wub_ref[...]) * w_b).astype(BF16)
        y = x + _dot(act_a, wda_ref[...]) + _dot(act_b, wdb_ref[...])
        if final_norm:
            y = _rms(y, gfin_ref[...])
        obuf[slot] = y

    @pl.when(last_used)
    def _():
        start_scatter(j, cnt_ref[j], slot)

        @pl.when(j >= 1)
        def _():
            wait_scatter(n_prev, other)

        wait_scatter(cnt_ref[j], slot)


def moe_ffn(x_ext, pos, tile_a, tile_b, tile_cnt, g_ffn, w_gate, w_up, w_down, g_final, final_norm):
    n, de = x_ext.shape
    d = de - LANES
    rows = MOE_TILE
    nt = tile_cnt.shape[0]
    ff = w_gate.shape[2]
    fix = lambda j, *_: (0, 0)
    exp_a = lambda j, pos, ea, eb, cnt: (ea[j], 0, 0)
    exp_b = lambda j, pos, ea, eb, cnt: (eb[j], 0, 0)
    up_spec = lambda im: pl.BlockSpec((None, d, ff), im)
    down_spec = lambda im: pl.BlockSpec((None, ff, d), im)
    grid_spec = pltpu.PrefetchScalarGridSpec(
        num_scalar_prefetch=4,
        grid=(nt,),
        in_specs=[pl.BlockSpec(memory_space=pl.ANY), pl.BlockSpec((1, d), fix),
                  up_spec(exp_a), up_spec(exp_b), up_spec(exp_a), up_spec(exp_b),
                  down_spec(exp_a), down_spec(exp_b), pl.BlockSpec((1, d), fix)],
        out_specs=pl.BlockSpec(memory_space=pl.ANY),
        scratch_shapes=[pltpu.SMEM((nt * rows,), I32), pltpu.SMEM((nt * rows,), I32),
                        pltpu.VMEM((2, rows, de), F32), pltpu.VMEM((2, rows, d), F32),
                        pltpu.SemaphoreType.DMA((2,)), pltpu.SemaphoreType.DMA((2,))],
    )
    return pl.pallas_call(
        functools.partial(_moe_kernel, final_norm=final_norm, n_tokens=n),
        grid_spec=grid_spec,
        out_shape=jax.ShapeDtypeStruct((n + 2 * rows, d), F32),
        compiler_params=_params(1),
        name="moe_ffn",
    )(pos, tile_a, tile_b, tile_cnt, x_ext, g_ffn.reshape(1, d),
      w_gate, w_gate, w_up, w_up, w_down, w_down, g_final.reshape(1, d))


_PAIR_LO = np.array([0, 0, 0, 1, 1, 2], np.int32)
_PAIR_HI = np.array([1, 2, 3, 2, 3, 3], np.int32)


def _tile_plan(counts, n_tokens):
    rows = MOE_TILE
    counts = counts.astype(I32)
    total = jnp.sum(counts, axis=0)
    padded = (total + rows - 1) // rows * rows
    start = jnp.cumsum(padded) - padded
    base = start[None, :] + jnp.cumsum(counts, axis=0) - counts
    nt = n_tokens // rows + N_BUCKETS
    first_row = jnp.arange(nt, dtype=I32) * rows
    end = (start + padded)[:N_BUCKETS]
    bucket = jnp.sum((first_row[:, None] >= end[None, :]).astype(I32), axis=1)
    used = bucket < N_BUCKETS
    last_used = jnp.max(jnp.where(used, bucket, 0))
    b_eff = jnp.where(used, bucket, last_used)
    cnt = jnp.where(used, jnp.clip(total[b_eff] - (first_row - start[b_eff]), 0, rows), 0)
    grp = b_eff // N_PAIRS
    pair = b_eff % N_PAIRS
    exp_a = grp * EXPERTS_PER_GROUP + jnp.asarray(_PAIR_LO)[pair]
    exp_b = grp * EXPERTS_PER_GROUP + jnp.asarray(_PAIR_HI)[pair]
    return base.astype(F32), exp_a.astype(I32), exp_b.astype(I32), cnt.astype(I32)


def kernel(x, mem, positions, norm_mix, w_in, hg_lower_bounds, hg_out_norm, w_out, norm_cross, norm_mem,
           wq_x, wkv_x, wo_x, norm_ffn, w_router_group, w_router_expert, w_gate, w_up, w_down, norm_final):
    batch, seq, d = x.shape
    depth = w_in.shape[0]
    n = batch * seq
    hg_width = hg_lower_bounds.shape[1]
    att_width = (w_in.shape[2] - 4 * hg_width) // 3
    assert hg_width == HG_HEADS * HG_DK and seq % ATT_SUPER == 0 and n % TOK_TILE == 0

    lb_sm = jax.nn.softmax(hg_lower_bounds.astype(F32), axis=0)
    lbs = jnp.cumsum(lb_sm, axis=0) - lb_sm[0:1]
    cos, sin = rope_tables(positions)
    xs = x.reshape(n, d)
    mem2d = mem.reshape(-1, d)
    w_router = jnp.concatenate([w_router_group, w_router_expert], axis=-1)
    w_router = jnp.pad(w_router, ((0, 0), (0, 0), (0, LANES - w_router.shape[-1])))
    wg16 = w_gate.astype(BF16).reshape((-1,) + w_gate.shape[2:])
    wu16 = w_up.astype(BF16).reshape((-1,) + w_up.shape[2:])
    wd16 = w_down.astype(BF16).reshape((-1,) + w_down.shape[2:])

    for l in range(depth):
        hproj, aq, ak, av = in_projection(xs, norm_mix[l], w_in[l].astype(BF16), cos, sin, hg_width, att_width)
        y_hg = hgrn_mixer(hproj, lbs[l], hg_out_norm[l], batch, seq)
        y_att = dilated_attention(aq, ak, av, batch, seq)
        kx, vx = memory_kv(mem2d, norm_mem[l], wkv_x[l].astype(BF16), batch)
        x_ext, bucket, counts = cross_block(xs, y_hg, y_att, w_out[l].astype(BF16), norm_cross[l],
                                            wq_x[l].astype(BF16), kx, vx, wo_x[l].astype(BF16),
                                            norm_ffn[l], w_router[l], batch)
        base, exp_a, exp_b, cnt = _tile_plan(counts.reshape(n // POS_TILE, -1, LANES).sum(axis=1), n)
        pos = sorted_positions(bucket, base.reshape(base.shape[0], 1, LANES))
        xs = moe_ffn(x_ext, pos.reshape(n), exp_a + l * N_EXPERTS, exp_b + l * N_EXPERTS, cnt, norm_ffn[l],
                     wg16, wu16, wd16, norm_final, final_norm=(l == depth - 1))
    return xs[:n].reshape(batch, seq, d)
```

```python
import functools

import numpy as np
import jax
import jax.numpy as jnp
from jax import lax
from jax.experimental import pallas as pl
from jax.experimental.pallas import tpu as pltpu

F32 = jnp.float32
BF16 = jnp.bfloat16
I32 = jnp.int32

NORM_EPS = 1e-6
ROPE_THETA = 10000.0
HG_HEADS = 4
HG_DK = 128
ATT_DH = 64
ATT_BLK = 128
ATT_DILATIONS = (1, 4, 16)
ATT_CLASSES = 16
ATT_SUPER = ATT_BLK * ATT_CLASSES
X_HEADS = 4
N_GROUPS = 4
EXPERTS_PER_GROUP = 4
N_EXPERTS = 16
N_PAIRS = 6
N_BUCKETS = N_GROUPS * N_PAIRS
EXPERT_FF = 512
LANES = 128
SUBLANES = 8
NEG = -1e30
LOG2_E = 1.4426950408889634

HG_T = 128
HG_STEP = 256
TOK_TILE = 512
X_TILE = 512
POS_TILE = 1024
MOE_TILE = 256
DMA_PRIORITIES = 2
COPY_GROUP = 32
VMEM_LIMIT = 56 * 1024 * 1024


def _dot(a, b):
    return jnp.dot(a, b, preferred_element_type=F32)


def _dot_nt(a, b):
    return lax.dot_general(a, b, (((1,), (1,)), ((), ())), preferred_element_type=F32)


def _dot_tn(a, b):
    return lax.dot_general(a, b, (((0,), (0,)), ((), ())), preferred_element_type=F32)


def _rms(x, g):
    return x * lax.rsqrt(jnp.mean(x * x, axis=-1, keepdims=True) + NORM_EPS) * g


def _split_bf16(a):
    hi = a.astype(BF16)
    return hi, (a - hi.astype(F32)).astype(BF16)


def _params(n_axes):
    return pltpu.CompilerParams(dimension_semantics=("arbitrary",) * n_axes,
                                vmem_limit_bytes=VMEM_LIMIT)


def _rope_kernel(pos_ref, inv_ref, cos_ref, sin_ref):
    ang = pos_ref[...].astype(F32) * inv_ref[...]
    lane = lax.broadcasted_iota(I32, ang.shape, 1)
    s = jnp.sin(ang)
    cos_ref[...] = jnp.cos(ang)
    sin_ref[...] = jnp.where((lane % ATT_DH) < ATT_DH // 2, -s, s)


def rope_tables(positions):
    n = positions.size
    inv = ROPE_THETA ** (-jnp.arange(0, ATT_DH, 2, dtype=F32) / ATT_DH)
    inv = jnp.tile(inv, LANES // (ATT_DH // 2)).reshape(1, LANES)
    tm = TOK_TILE
    return pl.pallas_call(
        _rope_kernel,
        grid=(n // tm,),
        in_specs=[pl.BlockSpec((tm, 1), lambda i: (i, 0)),
                  pl.BlockSpec((1, LANES), lambda i: (0, 0))],
        out_specs=[pl.BlockSpec((tm, LANES), lambda i: (i, 0))] * 2,
        out_shape=[jax.ShapeDtypeStruct((n, LANES), F32)] * 2,
        compiler_params=_params(1),
        name="rope_tables",
    )(positions.reshape(n, 1), inv)


def _inproj_kernel(x_ref, g_ref, w_ref, cos_ref, sin_ref, hp_ref, q_ref, k_ref, v_ref):
    h = _rms(x_ref[...], g_ref[...]).astype(BF16)
    hgw = hp_ref.shape[1]
    aw = q_ref.shape[1]
    for c in range(hgw // aw):
        hp_ref[:, c * aw:(c + 1) * aw] = _dot(h, w_ref[:, c * aw:(c + 1) * aw])
    cos = cos_ref[...]
    sin = sin_ref[...]
    lane = lax.broadcasted_iota(I32, cos.shape, 1)
    first = (lane % ATT_DH) < ATT_DH // 2

    def rope(t, scale):
        for p in range(aw // LANES):
            tp = t[:, p * LANES:(p + 1) * LANES]
            rot = jnp.where(first, pltpu.roll(tp, LANES - ATT_DH // 2, axis=1),
                            pltpu.roll(tp, ATT_DH // 2, axis=1))
            yield (tp * cos + rot * sin) * scale

    aq = _dot(h, w_ref[:, hgw:hgw + aw])
    for p, blk in enumerate(rope(aq, ATT_DH ** -0.5 * LOG2_E)):
        q_ref[:, p * LANES:(p + 1) * LANES] = blk
    ak = _dot(h, w_ref[:, hgw + aw:hgw + 2 * aw])
    for p, blk in enumerate(rope(ak, 1.0)):
        k_ref[:, p * LANES:(p + 1) * LANES] = blk
    v_ref[...] = _dot(h, w_ref[:, hgw + 2 * aw:hgw + 3 * aw])


def in_projection(x2d, gain, w_in, cos, sin, hg_width, att_width):
    n, d = x2d.shape
    tm = TOK_TILE
    row = lambda i: (i, 0)
    fix = lambda i: (0, 0)
    return pl.pallas_call(
        _inproj_kernel,
        grid=(n // tm,),
        in_specs=[pl.BlockSpec((tm, d), row), pl.BlockSpec((1, d), fix),
                  pl.BlockSpec(w_in.shape, fix),
                  pl.BlockSpec((tm, LANES), row), pl.BlockSpec((tm, LANES), row)],
        out_specs=[pl.BlockSpec((tm, 4 * hg_width), row)] + [pl.BlockSpec((tm, att_width), row)] * 3,
        out_shape=[jax.ShapeDtypeStruct((n, 4 * hg_width), F32)]
        + [jax.ShapeDtypeStruct((n, att_width), F32)] * 3,
        compiler_params=_params(1),
        name="in_projection",
    )(x2d, gain.reshape(1, d), w_in, cos, sin)


def _level_table(t):
    ti = np.arange(t)[:, None]
    si = np.arange(t)[None, :]
    x = np.maximum(ti ^ si, 1)
    lvl = np.floor(np.log2(x)).astype(np.int32)
    diag = int(np.log2(t))
    return np.where(si < ti, lvl, np.where(si == ti, diag, -1)).astype(np.int32)


def _hgrn_kernel(q_ref, f_ref, i_ref, gt_ref, lb_ref, gain_ref, lvl_ref, y_ref, st_ref):
    t_rows = lvl_ref.shape[0]
    n_lev = t_rows.bit_length() - 1

    @pl.when(pl.program_id(1) == 0)
    def _():
        st_ref[...] = jnp.zeros_like(st_ref)

    row = lax.broadcasted_iota(I32, (t_rows, HG_DK), 0)
    lvl = lvl_ref[...]
    lb_all = lb_ref[...]
    for u, h in [(u, h) for u in range(q_ref.shape[0] // t_rows) for h in range(HG_HEADS)]:
        rs = slice(u * t_rows, (u + 1) * t_rows)
        sl = slice(h * HG_DK, (h + 1) * HG_DK)
        lb = lb_all[:, sl]
        f = lb + (1.0 - lb) * jax.nn.sigmoid(f_ref[rs, sl])
        g = jnp.log(f) * LOG2_E
        b = g
        s = 1
        while s < t_rows:
            b = b + jnp.where(row >= s, pltpu.roll(b, s, axis=0), 0.0)
            s *= 2
        kk = 1.0 - f
        q = q_ref[rs, sl]
        v16 = i_ref[rs, sl].astype(BF16)
        scores = jnp.where(lvl == n_lev, _dot_nt(q.astype(BF16), kk.astype(BF16)), 0.0)
        first = b - g
        last = b
        for j in range(n_lev):
            if j > 0:
                half = 1 << (j - 1)
                bit = ((row >> (j - 1)) & 1) == 1
                first = jnp.where(bit, pltpu.roll(first, half, axis=0), first)
                last = jnp.where(bit, last, pltpu.roll(last, t_rows - half, axis=0))
            qj = (q * jnp.exp2(b - first)).astype(BF16)
            kj = (kk * jnp.exp2(last - b)).astype(BF16)
            scores = jnp.where(lvl == j, _dot_nt(qj, kj), scores)
        st = st_ref[h]
        o = _dot(scores.astype(BF16), v16) + _dot_nt((q * jnp.exp2(b)).astype(BF16), st.astype(BF16))
        b_last = b[t_rows - 1:t_rows, :]
        kdec = (kk * jnp.exp2(b_last - b)).astype(BF16)
        st_ref[h] = jnp.exp2(b_last) * st + _dot_tn(v16, kdec)
        o = o * lax.rsqrt(jnp.mean(o * o, axis=-1, keepdims=True) + NORM_EPS) * gain_ref[:, sl]
        y_ref[rs, sl] = o * jax.nn.silu(gt_ref[rs, sl])


def hgrn_mixer(hproj, lb, out_gain, batch, seq):
    n, w4 = hproj.shape
    w = w4 // 4
    t = HG_STEP
    nb = seq // t
    col = lambda c: (lambda b, i: (b * nb + i, c))
    fix = lambda b, i: (0, 0)
    lvl = jnp.asarray(_level_table(HG_T))
    return pl.pallas_call(
        _hgrn_kernel,
        grid=(batch, nb),
        in_specs=[pl.BlockSpec((t, w), col(0)), pl.BlockSpec((t, w), col(1)),
                  pl.BlockSpec((t, w), col(2)), pl.BlockSpec((t, w), col(3)),
                  pl.BlockSpec((1, w), fix), pl.BlockSpec((1, w), fix),
                  pl.BlockSpec((HG_T, HG_T), fix)],
        out_specs=pl.BlockSpec((t, w), col(0)),
        out_shape=jax.ShapeDtypeStruct((n, w), F32),
        scratch_shapes=[pltpu.VMEM((HG_HEADS, HG_DK, HG_DK), F32)],
        compiler_params=_params(2),
        name="hgrn_mixer",
    )(hproj, hproj, hproj, hproj, lb.reshape(1, w), out_gain.reshape(1, w), lvl)


def _att_masks():
    blk = ATT_BLK
    rows = np.arange(blk)
    out = []
    for dil in ATT_DILATIONS:
        seg = blk * dil // ATT_CLASSES
        pos = (ATT_CLASSES // dil) * (rows % seg) + rows // seg
        kpos = np.concatenate([pos, blk + pos])
        dist = pos[:, None] + blk - kpos[None, :]
        out.append(((dist >= 0) & (dist <= blk)).astype(np.float32))
    return np.stack(out)


def _att_kernel(q_hbm, k_hbm, v_hbm, mask_ref, o_hbm,
                qbuf, kbuf, vbuf, acc, m_s, l_s, obuf, isem, osem):
    blk = ATT_BLK
    ncls = ATT_CLASSES
    npair = pl.num_programs(1)
    nsb = pl.num_programs(2)
    sb = pl.program_id(2)
    step = (pl.program_id(0) * npair + pl.program_id(1)) * nsb + sb
    n_steps = pl.num_programs(0) * npair * nsb
    lane = lax.broadcasted_iota(I32, (blk, LANES), 1)
    head0 = lane < ATT_DH
    col = lax.broadcasted_iota(I32, (blk, 2 * blk), 1)
    cur = step % 3
    prev = (step + 2) % 3

    def load_copies(s):
        row0 = pl.multiple_of(((s // (npair * nsb)) * nsb + s % nsb) * blk, blk)
        lane0 = pl.multiple_of(((s // nsb) % npair) * LANES, LANES)
        for r in range(ncls):
            dst = pl.ds(r * blk, blk)
            for hbm, buf, slot in ((q_hbm, qbuf, s % 2), (k_hbm, kbuf, s % 3), (v_hbm, vbuf, s % 3)):
                yield pltpu.make_async_copy(hbm.at[pl.ds(row0, blk), r, pl.ds(lane0, LANES)],
                                            buf.at[slot, dst, :], isem.at[s % 2])

    def store_copies(s):
        row0 = pl.multiple_of(((s // (npair * nsb)) * nsb + s % nsb) * blk, blk)
        lane0 = pl.multiple_of(((s // nsb) % npair) * LANES, LANES)
        for r in range(ncls):
            yield pltpu.make_async_copy(obuf.at[pl.ds(r * blk, blk), :],
                                        o_hbm.at[pl.ds(row0, blk), r, pl.ds(lane0, LANES)], osem.at[0])

    @pl.when(step == 0)
    def _():
        kbuf[2] = jnp.zeros(kbuf.shape[1:], kbuf.dtype)
        vbuf[2] = jnp.zeros(vbuf.shape[1:], vbuf.dtype)
        for cp in load_copies(step):
            cp.start()

    @pl.when(step + 1 < n_steps)
    def _():
        for cp in load_copies(step + 1):
            cp.start()

    for cp in load_copies(step):
        cp.wait()
    qs = step % 2

    def rows_of(buf, pieces, seg):
        return jnp.concatenate([buf[slot, pl.ds(o, seg), :] for slot, o in pieces], axis=0).astype(BF16)

    def block_attention(qb, kb, vb, valid):
        outs = []
        for hh in range(2):
            hm = head0 if hh == 0 else jnp.logical_not(head0)
            s = _dot_nt(jnp.where(hm, qb, jnp.zeros_like(qb)), kb)
            s = jnp.where(valid, s, NEG)
            m = jnp.max(s, axis=-1, keepdims=True)
            p = jnp.exp2(s - m)
            l = jnp.sum(p, axis=-1, keepdims=True)
            outs.append((_dot(p.astype(BF16), vb), m, l))
        (o0, m0, l0), (o1, m1, l1) = outs
        return (jnp.where(head0, o0, o1), jnp.where(head0, m0, m1), jnp.where(head0, l0, l1))

    def first_ok(is_first):
        return jnp.logical_or(col >= blk, jnp.logical_not(is_first))

    def pattern(p, dil):
        band = mask_ref[p] > 0.5
        nseg = ncls // dil
        seg = blk // nseg

        def body(idx, _):
            c = idx // nseg
            n = idx % nseg
            valid = jnp.logical_and(band, first_ok(jnp.logical_and(sb == 0, n == 0)))
            slot_p = jnp.where(n == 0, prev, cur)
            n_p = (n + nseg - 1) % nseg
            q_off = [pl.multiple_of((c + dil * k) * blk + seg * n, seg) for k in range(nseg)]
            p_off = [pl.multiple_of((c + dil * k) * blk + seg * n_p, seg) for k in range(nseg)]
            keys = [(slot_p, o) for o in p_off] + [(cur, o) for o in q_off]
            o, m, l = block_attention(rows_of(qbuf, [(qs, o) for o in q_off], seg),
                                      rows_of(kbuf, keys, seg), rows_of(vbuf, keys, seg), valid)
            for k, off in enumerate(q_off):
                rs = slice(k * seg, (k + 1) * seg)
                dst = pl.ds(off, seg)
                acc[p, dst, :] = o[rs]
                m_s[p, dst, :] = m[rs]
                l_s[p, dst, :] = l[rs]
            return 0

        lax.fori_loop(0, ncls, body, 0, unroll=True)

    for p, dil in enumerate(ATT_DILATIONS):
        pattern(p, dil)

    @pl.when(step > 0)
    def _():
        for cp in store_copies(step - 1):
            cp.wait()

    def combine(i, _):
        rows = pl.ds(pl.multiple_of(i * blk, blk), blk)
        ms = [m_s[p, rows, :] for p in range(len(ATT_DILATIONS))]
        top = functools.reduce(jnp.maximum, ms)
        ws = [jnp.exp2(m - top) for m in ms]
        num = sum(w * acc[p, rows, :] for p, w in enumerate(ws))
        den = sum(w * l_s[p, rows, :] for p, w in enumerate(ws))
        obuf[rows, :] = num / den
        return 0

    lax.fori_loop(0, ncls, combine, 0, unroll=2)
    for cp in store_copies(step):
        cp.start()

    @pl.when(step == n_steps - 1)
    def _():
        for cp in store_copies(step):
            cp.wait()


def dilated_attention(aq, ak, av, batch, seq):
    n, w = aq.shape
    nsb = seq // ATT_SUPER
    npair = w // LANES
    masks = jnp.asarray(_att_masks())
    view = lambda t: t.reshape(n // ATT_CLASSES, ATT_CLASSES, w)
    any_spec = pl.BlockSpec(memory_space=pl.ANY)
    out = pl.pallas_call(
        _att_kernel,
        grid=(batch, npair, nsb),
        in_specs=[any_spec, any_spec, any_spec, pl.BlockSpec(masks.shape, lambda b, hp, sb: (0, 0, 0))],
        out_specs=any_spec,
        out_shape=jax.ShapeDtypeStruct((n // ATT_CLASSES, ATT_CLASSES, w), F32),
        scratch_shapes=[pltpu.VMEM((2, ATT_SUPER, LANES), F32),
                        pltpu.VMEM((3, ATT_SUPER, LANES), F32),
                        pltpu.VMEM((3, ATT_SUPER, LANES), F32)]
        + [pltpu.VMEM((len(ATT_DILATIONS), ATT_SUPER, LANES), F32)] * 3
        + [pltpu.VMEM((ATT_SUPER, LANES), F32),
           pltpu.SemaphoreType.DMA((2,)), pltpu.SemaphoreType.DMA((1,))],
        compiler_params=_params(3),
        name="dilated_attention",
    )(view(aq), view(ak), view(av), masks)
    return out.reshape(n, w)


def _memkv_kernel(mem_ref, g_ref, w_ref, k_ref, v_ref):
    h = _rms(mem_ref[...], g_ref[...]).astype(BF16)
    d = k_ref.shape[1]
    k_ref[...] = _dot(h, w_ref[:, :d]).astype(BF16)
    v_ref[...] = _dot(h, w_ref[:, d:]).astype(BF16)


def memory_kv(mem2d, gain, wkv, batch):
    n, d = mem2d.shape
    m = n // batch
    row = lambda b: (b, 0)
    fix = lambda b: (0, 0)
    return pl.pallas_call(
        _memkv_kernel,
        grid=(batch,),
        in_specs=[pl.BlockSpec((m, d), row), pl.BlockSpec((1, d), fix), pl.BlockSpec(wkv.shape, fix)],
        out_specs=[pl.BlockSpec((m, d), row)] * 2,
        out_shape=[jax.ShapeDtypeStruct((n, d), BF16)] * 2,
        compiler_params=_params(1),
        name="memory_kv",
    )(mem2d, gain.reshape(1, d), wkv)


def _cross_kernel(x_ref, yh_ref, ya_ref, wout_ref, gx_ref, wq_ref, kx_ref, vx_ref, wo_ref,
                  gf_ref, wr_ref, xe_ref, bkt_ref, cnt_ref):
    d = x_ref.shape[1]
    hw = yh_ref.shape[1]
    x = (x_ref[...] + _dot(yh_ref[...].astype(BF16), wout_ref[:hw, :])
         + _dot(ya_ref[...].astype(BF16), wout_ref[hw:, :]))
    h = _rms(x, gx_ref[...]).astype(BF16)
    q = _dot(h, wq_ref[...])
    dh = d // X_HEADS
    heads = []
    for hh in range(X_HEADS):
        sl = slice(hh * dh, (hh + 1) * dh)
        s = _dot_nt(q[:, sl].astype(BF16), kx_ref[:, sl]) * (dh ** -0.5)
        s = s - jnp.max(s, axis=-1, keepdims=True)
        p = jnp.exp(s)
        p = p / jnp.sum(p, axis=-1, keepdims=True)
        heads.append(_dot(p.astype(BF16), vx_ref[:, sl]).astype(BF16))
    x = x + _dot(jnp.concatenate(heads, axis=1), wo_ref[...])
    xe_ref[:, :d] = x

    hf = _rms(x, gf_ref[...])
    h_hi, h_lo = _split_bf16(hf)
    w_hi, w_lo = _split_bf16(wr_ref[...])
    both = _dot(h_hi, jnp.concatenate([w_hi, w_lo], axis=1))
    logits = both[:, :LANES] + both[:, LANES:] + _dot(h_lo, w_hi)
    lane = lax.broadcasted_iota(I32, logits.shape, 1)
    big = 1 << 20

    def first_max(vals):
        top = jnp.max(vals, axis=-1, keepdims=True)
        return top, jnp.min(jnp.where(vals == top, lane, big), axis=-1, keepdims=True)

    is_g = lane < N_GROUPS
    g_top, g_sel = first_max(jnp.where(is_g, logits, NEG))
    g_w = 1.0 / jnp.sum(jnp.where(is_g, jnp.exp(logits - g_top), 0.0), axis=-1, keepdims=True)
    e_lo = N_GROUPS + EXPERTS_PER_GROUP * g_sel
    in_grp = jnp.logical_and(lane >= e_lo, lane < e_lo + EXPERTS_PER_GROUP)
    e_log = jnp.where(in_grp, logits, NEG)
    v1, i1 = first_max(e_log)
    v2, i2 = first_max(jnp.where(lane == i1, NEG, e_log))
    t = jnp.exp(v2 - v1)
    w1 = g_w / (1.0 + t)
    w2 = g_w * t / (1.0 + t)
    j1 = i1 - e_lo
    j2 = i2 - e_lo
    lo = jnp.minimum(j1, j2)
    hi = jnp.maximum(j1, j2)
    pair = jnp.where(lo == 0, hi - 1, jnp.where(lo == 1, hi + 1, N_PAIRS - 1))
    bucket = g_sel * N_PAIRS + pair
    w_lo = jnp.where(j1 < j2, w1, w2)
    w_hi = jnp.where(j1 < j2, w2, w1)
    xe_ref[:, d:] = jnp.where(lane == 0, w_lo, jnp.where(lane == 1, w_hi, 0.0))
    bkt_ref[...] = bucket
    cnt_ref[0] = jnp.sum((lane == bucket).astype(F32), axis=0, keepdims=True)


def cross_block(x2d, y_hg, y_att, w_out, g_cross, wq, kx, vx, wo, g_ffn, w_router, batch):
    n, d = x2d.shape
    tm = X_TILE
    nt = n // tm
    per_batch = nt // batch
    m = kx.shape[0] // batch
    row = lambda i: (i, 0)
    fix = lambda i: (0, 0)
    mem = lambda i: (i // per_batch, 0)
    return pl.pallas_call(
        _cross_kernel,
        grid=(nt,),
        in_specs=[pl.BlockSpec((tm, d), row), pl.BlockSpec((tm, y_hg.shape[1]), row),
                  pl.BlockSpec((tm, y_att.shape[1]), row), pl.BlockSpec(w_out.shape, fix),
                  pl.BlockSpec((1, d), fix), pl.BlockSpec(wq.shape, fix),
                  pl.BlockSpec((m, d), mem), pl.BlockSpec((m, d), mem), pl.BlockSpec(wo.shape, fix),
                  pl.BlockSpec((1, d), fix), pl.BlockSpec(w_router.shape, fix)],
        out_specs=[pl.BlockSpec((tm, d + LANES), row), pl.BlockSpec((tm, 1), row),
                   pl.BlockSpec((1, 1, LANES), lambda i: (i, 0, 0))],
        out_shape=[jax.ShapeDtypeStruct((n, d + LANES), F32), jax.ShapeDtypeStruct((n, 1), I32),
                   jax.ShapeDtypeStruct((nt, 1, LANES), F32)],
        compiler_params=_params(1),
        name="cross_block",
    )(x2d, y_hg, y_att, w_out, g_cross.reshape(1, d), wq, kx, vx, wo, g_ffn.reshape(1, d), w_router)


def _position_kernel(bkt_ref, base_ref, tri_ref, pos_ref):
    lane = lax.broadcasted_iota(I32, (bkt_ref.shape[0], LANES), 1)
    onehot = (lane == bkt_ref[...]).astype(F32)
    before = _dot(tri_ref[...], onehot.astype(BF16))
    pos = jnp.sum(onehot * (before + base_ref[0]), axis=-1, keepdims=True)
    pos_ref[...] = pos.astype(I32)


def sorted_positions(bucket, base):
    n = bucket.shape[0]
    nt = base.shape[0]
    tm = n // nt
    tri = jnp.asarray(np.tril(np.ones((tm, tm), np.float32), -1), BF16)
    return pl.pallas_call(
        _position_kernel,
        grid=(nt,),
        in_specs=[pl.BlockSpec((tm, 1), lambda i: (i, 0)),
                  pl.BlockSpec((1, 1, LANES), lambda i: (i, 0, 0)),
                  pl.BlockSpec((tm, tm), lambda i: (0, 0))],
        out_specs=pl.BlockSpec((tm, 1), lambda i: (i, 0)),
        out_shape=jax.ShapeDtypeStruct((n, 1), I32),
        compiler_params=_params(1),
        name="sorted_positions",
    )(bucket, base, tri)


def _moe_kernel(pos_ref, ea_ref, eb_ref, cnt_ref,
                x_hbm, g_ref, wga_ref, wgb_ref, wua_ref, wub_ref, wda_ref, wdb_ref, gfin_ref,
                out_hbm, src_ref, xbuf, obuf, gsem, ssem, *, final_norm):
    j = pl.program_id(0)
    nt = pl.num_programs(0)
    rows = xbuf.shape[1]
    d = obuf.shape[2]
    slot = j % 2
    other = 1 - slot

    def for_rows(n, fn):
        for g in range(rows // COPY_GROUP):
            @pl.when((g + 1) * COPY_GROUP <= n)
            def _():
                for r in range(g * COPY_GROUP, (g + 1) * COPY_GROUP):
                    fn(r, r % DMA_PRIORITIES)

        def single(r, _):
            fn(r, 0)
            return 0

        lax.fori_loop(n // COPY_GROUP * COPY_GROUP, n, single, 0)

    def wait_rows(n, bulk_copy, row_copy):
        n8 = pl.multiple_of(n // SUBLANES * SUBLANES, SUBLANES)

        @pl.when(n8 > 0)
        def _():
            bulk_copy(n8).wait()

        def single(r, _):
            row_copy.wait()
            return 0

        lax.fori_loop(n8, n, single, 0)

    def start_gather(tile, n, sl):
        def copy(r, priority):
            tok = src_ref[tile * rows + r]
            pltpu.make_async_copy(x_hbm.at[pl.ds(tok, 1)], xbuf.at[sl, pl.ds(r, 1)],
                                  gsem.at[sl]).start(priority=priority)
        for_rows(n, copy)

    def start_scatter(tile, n, sl):
        def copy(r, priority):
            tok = src_ref[tile * rows + r]
            pltpu.make_async_copy(obuf.at[sl, pl.ds(r, 1)], out_hbm.at[pl.ds(tok, 1)],
                                  ssem.at[sl]).start(priority=priority)
        for_rows(n, copy)

    def wait_gather(n, sl):
        wait_rows(n,
                  lambda m: pltpu.make_async_copy(x_hbm.at[pl.ds(0, m)], xbuf.at[sl, pl.ds(0, m)], gsem.at[sl]),
                  pltpu.make_async_copy(x_hbm.at[pl.ds(0, 1)], xbuf.at[sl, pl.ds(0, 1)], gsem.at[sl]))

    def wait_scatter(n, sl):
        wait_rows(n,
                  lambda m: pltpu.make_async_copy(obuf.at[sl, pl.ds(0, m)], out_hbm.at[pl.ds(0, m)], ssem.at[sl]),
                  pltpu.make_async_copy(obuf.at[sl, pl.ds(0, 1)], out_hbm.at[pl.ds(0, 1)], ssem.at[sl]))

    @pl.when(j == 0)
    def _():
        def fill(t, _):
            src_ref[pos_ref[t]] = t
            return 0

        lax.fori_loop(0, pos_ref.shape[0], fill, 0, unroll=8)
        xbuf[...] = jnp.zeros_like(xbuf)
        start_gather(0, cnt_ref[0], 0)

    used = cnt_ref[j] > 0
    nxt = jnp.minimum(j + 1, nt - 1)
    n_next = jnp.where(j + 1 < nt, cnt_ref[nxt], 0)
    n_prev = cnt_ref[jnp.maximum(j - 1, 0)]
    last_used = jnp.logical_and(used, n_next == 0)

    for static_slot in range(2):
        @pl.when(jnp.logical_and(used, slot == static_slot))
        def _():
            wait_gather(cnt_ref[j], static_slot)

            @pl.when(j >= 2)
            def _():
                wait_scatter(cnt_ref[jnp.maximum(j - 2, 0)], static_slot)

            start_gather(nxt, n_next, 1 - static_slot)

            @pl.when(j >= 1)
            def _():
                start_scatter(j - 1, n_prev, 1 - static_slot)

    @pl.when(cnt_ref[j] > 0)
    def _():
        xe = xbuf[slot]
        x = xe[:, :d]
        h = _rms(x, g_ref[...]).astype(BF16)
        lane = lax.broadcasted_iota(I32, (rows, LANES), 1)
        wts = xe[:, d:]
        w_a = jnp.sum(jnp.where(lane == 0, wts, 0.0), axis=-1, keepdims=True)
        w_b = jnp.sum(jnp.where(lane == 1, wts, 0.0), axis=-1, keepdims=True)
        act_a = (jax.nn.silu(_dot(h, wga_ref[...])) * _dot(h, wua_ref[...]) * w_a).astype(BF16)
        act_b = (jax.nn.silu(_dot(h, wgb_ref[...])) * _dot(h, wub_ref[...]) * w_b).astype(BF16)
        y = x + _dot(act_a, wda_ref[...]) + _dot(act_b, wdb_ref[...])
        if final_norm:
            y = _rms(y, gfin_ref[...])
        obuf[slot] = y

    @pl.when(last_used)
    def _():
        start_scatter(j, cnt_ref[j], slot)

        @pl.when(j >= 1)
        def _():
            wait_scatter(n_prev, other)

        wait_scatter(cnt_ref[j], slot)


def moe_ffn(x_ext, pos, tile_a, tile_b, tile_cnt, g_ffn, w_gate, w_up, w_down, g_final, final_norm):
    n, de = x_ext.shape
    d = de - LANES
    rows = MOE_TILE
    nt = tile_cnt.shape[0]
    ff = w_gate.shape[2]
    fix = lambda j, *_: (0, 0)
    exp_a = lambda j, pos, ea, eb, cnt: (ea[j], 0, 0)
    exp_b = lambda j, pos, ea, eb, cnt: (eb[j], 0, 0)
    up_spec = lambda im: pl.BlockSpec((None, d, ff), im)
    down_spec = lambda im: pl.BlockSpec((None, ff, d), im)
    grid_spec = pltpu.PrefetchScalarGridSpec(
        num_scalar_prefetch=4,
        grid=(nt,),
        in_specs=[pl.BlockSpec(memory_space=pl.ANY), pl.BlockSpec((1, d), fix),
                  up_spec(exp_a), up_spec(exp_b), up_spec(exp_a), up_spec(exp_b),
                  down_spec(exp_a), down_spec(exp_b), pl.BlockSpec((1, d), fix)],
        out_specs=pl.BlockSpec(memory_space=pl.ANY),
        scratch_shapes=[pltpu.SMEM((nt * rows,), I32),
                        pltpu.VMEM((2, rows, de), F32), pltpu.VMEM((2, rows, d), F32),
                        pltpu.SemaphoreType.DMA((2,)), pltpu.SemaphoreType.DMA((2,))],
    )
    return pl.pallas_call(
        functools.partial(_moe_kernel, final_norm=final_norm),
        grid_spec=grid_spec,
        out_shape=jax.ShapeDtypeStruct((n, d), F32),
        compiler_params=_params(1),
        name="moe_ffn",
    )(pos, tile_a, tile_b, tile_cnt, x_ext, g_ffn.reshape(1, d),
      w_gate, w_gate, w_up, w_up, w_down, w_down, g_final.reshape(1, d))


_PAIR_LO = np.array([0, 0, 0, 1, 1, 2], np.int32)
_PAIR_HI = np.array([1, 2, 3, 2, 3, 3], np.int32)


def _tile_plan(counts, n_tokens):
    rows = MOE_TILE
    counts = counts.astype(I32)
    total = jnp.sum(counts, axis=0)
    padded = (total + rows - 1) // rows * rows
    start = jnp.cumsum(padded) - padded
    base = start[None, :] + jnp.cumsum(counts, axis=0) - counts
    nt = n_tokens // rows + N_BUCKETS
    first_row = jnp.arange(nt, dtype=I32) * rows
    end = (start + padded)[:N_BUCKETS]
    bucket = jnp.sum((first_row[:, None] >= end[None, :]).astype(I32), axis=1)
    used = bucket < N_BUCKETS
    last_used = jnp.max(jnp.where(used, bucket, 0))
    b_eff = jnp.where(used, bucket, last_used)
    cnt = jnp.where(used, jnp.clip(total[b_eff] - (first_row - start[b_eff]), 0, rows), 0)
    grp = b_eff // N_PAIRS
    pair = b_eff % N_PAIRS
    exp_a = grp * EXPERTS_PER_GROUP + jnp.asarray(_PAIR_LO)[pair]
    exp_b = grp * EXPERTS_PER_GROUP + jnp.asarray(_PAIR_HI)[pair]
    return base.astype(F32), exp_a.astype(I32), exp_b.astype(I32), cnt.astype(I32)


def kernel(x, mem, positions, norm_mix, w_in, hg_lower_bounds, hg_out_norm, w_out, norm_cross, norm_mem,
           wq_x, wkv_x, wo_x, norm_ffn, w_router_group, w_router_expert, w_gate, w_up, w_down, norm_final):
    batch, seq, d = x.shape
    depth = w_in.shape[0]
    n = batch * seq
    hg_width = hg_lower_bounds.shape[1]
    att_width = (w_in.shape[2] - 4 * hg_width) // 3
    assert hg_width == HG_HEADS * HG_DK and seq % ATT_SUPER == 0 and n % TOK_TILE == 0

    lb_sm = jax.nn.softmax(hg_lower_bounds.astype(F32), axis=0)
    lbs = jnp.cumsum(lb_sm, axis=0) - lb_sm[0:1]
    cos, sin = rope_tables(positions)
    xs = x.reshape(n, d)
    mem2d = mem.reshape(-1, d)
    w_router = jnp.concatenate([w_router_group, w_router_expert], axis=-1)
    w_router = jnp.pad(w_router, ((0, 0), (0, 0), (0, LANES - w_router.shape[-1])))
    wg16 = w_gate.astype(BF16).reshape((-1,) + w_gate.shape[2:])
    wu16 = w_up.astype(BF16).reshape((-1,) + w_up.shape[2:])
    wd16 = w_down.astype(BF16).reshape((-1,) + w_down.shape[2:])

    for l in range(depth):
        hproj, aq, ak, av = in_projection(xs, norm_mix[l], w_in[l].astype(BF16), cos, sin, hg_width, att_width)
        y_hg = hgrn_mixer(hproj, lbs[l], hg_out_norm[l], batch, seq)
        y_att = dilated_attention(aq, ak, av, batch, seq)
        kx, vx = memory_kv(mem2d, norm_mem[l], wkv_x[l].astype(BF16), batch)
        x_ext, bucket, counts = cross_block(xs, y_hg, y_att, w_out[l].astype(BF16), norm_cross[l],
                                            wq_x[l].astype(BF16), kx, vx, wo_x[l].astype(BF16),
                                            norm_ffn[l], w_router[l], batch)
        base, exp_a, exp_b, cnt = _tile_plan(counts.reshape(n // POS_TILE, -1, LANES).sum(axis=1), n)
        pos = sorted_positions(bucket, base.reshape(base.shape[0], 1, LANES))
        xs = moe_ffn(x_ext, pos.reshape(n), exp_a + l * N_EXPERTS, exp_b + l * N_EXPERTS, cnt, norm_ffn[l],
                     wg16, wu16, wd16, norm_final, final_norm=(l == depth - 1))
    return xs.reshape(batch, seq, d)
```

```python
import functools

import numpy as np
import jax
import jax.numpy as jnp
from jax import lax
from jax.experimental import pallas as pl
from jax.experimental.pallas import tpu as pltpu

F32 = jnp.float32
BF16 = jnp.bfloat16
I32 = jnp.int32

NORM_EPS = 1e-6
ROPE_THETA = 10000.0
HG_HEADS = 4
HG_DK = 128
ATT_DH = 64
ATT_BLK = 128
ATT_DILATIONS = (1, 4, 16)
ATT_CLASSES = 16
ATT_SUPER = ATT_BLK * ATT_CLASSES
X_HEADS = 4
N_GROUPS = 4
EXPERTS_PER_GROUP = 4
N_EXPERTS = 16
N_PAIRS = 6
N_BUCKETS = N_GROUPS * N_PAIRS
EXPERT_FF = 512
LANES = 128
SUBLANES = 8
NEG = -1e30
LOG2_E = 1.4426950408889634

HG_T = 128
HG_STEP = 512
TOK_TILE = 512
X_TILE = 512
POS_TILE = 1024
MOE_TILE = 256
DMA_PRIORITIES = 2
COPY_GROUP = 32
VMEM_LIMIT = 56 * 1024 * 1024


def _dot(a, b):
    return jnp.dot(a, b, preferred_element_type=F32)


def _dot_nt(a, b):
    return lax.dot_general(a, b, (((1,), (1,)), ((), ())), preferred_element_type=F32)


def _dot_tn(a, b):
    return lax.dot_general(a, b, (((0,), (0,)), ((), ())), preferred_element_type=F32)


def _rms(x, g):
    return x * lax.rsqrt(jnp.mean(x * x, axis=-1, keepdims=True) + NORM_EPS) * g


def _split_bf16(a):
    hi = a.astype(BF16)
    return hi, (a - hi.astype(F32)).astype(BF16)


def _params(n_axes):
    return pltpu.CompilerParams(dimension_semantics=("arbitrary",) * n_axes,
                                vmem_limit_bytes=VMEM_LIMIT)


def _rope_kernel(pos_ref, inv_ref, cos_ref, sin_ref):
    nfreq = ATT_DH // 2
    ngrp = LANES // nfreq
    q = pos_ref.shape[0] // ngrp
    lane = lax.broadcasted_iota(I32, (q, LANES), 1)
    grp = lane // nfreq
    pos = pos_ref[...].astype(F32)
    packed = pos[0:q]
    for k in range(1, ngrp):
        packed = jnp.where(grp == k, pos[k * q:(k + 1) * q], packed)
    ang = packed * inv_ref[...]
    c = jnp.cos(ang)
    s = jnp.sin(ang)
    neg = (lane % ATT_DH) < nfreq

    def spread(t, k):
        out = pltpu.roll(t, (-k * nfreq) % LANES, axis=1)
        for j in range(1, ngrp):
            out = jnp.where(grp == j, pltpu.roll(t, ((j - k) * nfreq) % LANES, axis=1), out)
        return out

    for k in range(ngrp):
        cos_ref[k * q:(k + 1) * q, :] = spread(c, k)
        sk = spread(s, k)
        sin_ref[k * q:(k + 1) * q, :] = jnp.where(neg, -sk, sk)


def rope_tables(positions):
    n = positions.size
    inv = ROPE_THETA ** (-jnp.arange(0, ATT_DH, 2, dtype=F32) / ATT_DH)
    inv = jnp.tile(inv, LANES // (ATT_DH // 2)).reshape(1, LANES)
    tm = TOK_TILE
    return pl.pallas_call(
        _rope_kernel,
        grid=(n // tm,),
        in_specs=[pl.BlockSpec((tm, 1), lambda i: (i, 0)),
                  pl.BlockSpec((1, LANES), lambda i: (0, 0))],
        out_specs=[pl.BlockSpec((tm, LANES), lambda i: (i, 0))] * 2,
        out_shape=[jax.ShapeDtypeStruct((n, LANES), F32)] * 2,
        compiler_params=_params(1),
        name="rope_tables",
    )(positions.reshape(n, 1), inv)


def _inproj_kernel(x_ref, g_ref, w_ref, cos_ref, sin_ref, hp_ref, q_ref, k_ref, v_ref):
    h = _rms(x_ref[...], g_ref[...]).astype(BF16)
    hgw = hp_ref.shape[1]
    aw = q_ref.shape[1]
    for c in range(hgw // aw):
        hp_ref[:, c * aw:(c + 1) * aw] = _dot(h, w_ref[:, c * aw:(c + 1) * aw])
    cos = cos_ref[...]
    sin = sin_ref[...]
    lane = lax.broadcasted_iota(I32, cos.shape, 1)
    first = (lane % ATT_DH) < ATT_DH // 2

    def rope(t, scale):
        for p in range(aw // LANES):
            tp = t[:, p * LANES:(p + 1) * LANES]
            rot = jnp.where(first, pltpu.roll(tp, LANES - ATT_DH // 2, axis=1),
                            pltpu.roll(tp, ATT_DH // 2, axis=1))
            yield (tp * cos + rot * sin) * scale

    aq = _dot(h, w_ref[:, hgw:hgw + aw])
    for p, blk in enumerate(rope(aq, ATT_DH ** -0.5 * LOG2_E)):
        q_ref[:, p * LANES:(p + 1) * LANES] = blk
    ak = _dot(h, w_ref[:, hgw + aw:hgw + 2 * aw])
    for p, blk in enumerate(rope(ak, 1.0)):
        k_ref[:, p * LANES:(p + 1) * LANES] = blk
    v_ref[...] = _dot(h, w_ref[:, hgw + 2 * aw:hgw + 3 * aw])


def in_projection(x2d, gain, w_in, cos, sin, hg_width, att_width):
    n, d = x2d.shape
    tm = TOK_TILE
    row = lambda i: (i, 0)
    fix = lambda i: (0, 0)
    return pl.pallas_call(
        _inproj_kernel,
        grid=(n // tm,),
        in_specs=[pl.BlockSpec((tm, d), row), pl.BlockSpec((1, d), fix),
                  pl.BlockSpec(w_in.shape, fix),
                  pl.BlockSpec((tm, LANES), row), pl.BlockSpec((tm, LANES), row)],
        out_specs=[pl.BlockSpec((tm, 4 * hg_width), row)] + [pl.BlockSpec((tm, att_width), row)] * 3,
        out_shape=[jax.ShapeDtypeStruct((n, 4 * hg_width), F32)]
        + [jax.ShapeDtypeStruct((n, att_width), F32)] * 3,
        compiler_params=_params(1),
        name="in_projection",
    )(x2d, gain.reshape(1, d), w_in, cos, sin)


def _level_table(t):
    ti = np.arange(t)[:, None]
    si = np.arange(t)[None, :]
    x = np.maximum(ti ^ si, 1)
    lvl = np.floor(np.log2(x)).astype(np.int32)
    diag = int(np.log2(t))
    return np.where(si < ti, lvl, np.where(si == ti, diag, -1)).astype(np.int32)


def _hgrn_kernel(q_ref, f_ref, i_ref, gt_ref, lb_ref, gain_ref, lvl_ref, y_ref, st_ref):
    t_rows = lvl_ref.shape[0]
    n_lev = t_rows.bit_length() - 1

    @pl.when(pl.program_id(1) == 0)
    def _():
        st_ref[...] = jnp.zeros_like(st_ref)

    row = lax.broadcasted_iota(I32, (t_rows, HG_DK), 0)
    lvl = lvl_ref[...]
    lb_all = lb_ref[...]
    for u, h in [(u, h) for u in range(q_ref.shape[0] // t_rows) for h in range(HG_HEADS)]:
        rs = slice(u * t_rows, (u + 1) * t_rows)
        sl = slice(h * HG_DK, (h + 1) * HG_DK)
        lb = lb_all[:, sl]
        f = lb + (1.0 - lb) * jax.nn.sigmoid(f_ref[rs, sl])
        g = jnp.log(f) * LOG2_E
        b = g
        s = 1
        while s < t_rows:
            b = b + jnp.where(row >= s, pltpu.roll(b, s, axis=0), 0.0)
            s *= 2
        kk = 1.0 - f
        q = q_ref[rs, sl]
        v16 = i_ref[rs, sl].astype(BF16)
        scores = jnp.where(lvl == n_lev, _dot_nt(q.astype(BF16), kk.astype(BF16)), 0.0)
        first = b - g
        last = b
        for j in range(n_lev):
            if j > 0:
                half = 1 << (j - 1)
                bit = ((row >> (j - 1)) & 1) == 1
                first = jnp.where(bit, pltpu.roll(first, half, axis=0), first)
                last = jnp.where(bit, last, pltpu.roll(last, t_rows - half, axis=0))
            qj = (q * jnp.exp2(b - first)).astype(BF16)
            kj = (kk * jnp.exp2(last - b)).astype(BF16)
            scores = jnp.where(lvl == j, _dot_nt(qj, kj), scores)
        st = st_ref[h]
        o = _dot(scores.astype(BF16), v16) + _dot_nt((q * jnp.exp2(b)).astype(BF16), st.astype(BF16))
        b_last = b[t_rows - 1:t_rows, :]
        kdec = (kk * jnp.exp2(b_last - b)).astype(BF16)
        st_ref[h] = jnp.exp2(b_last) * st + _dot_tn(v16, kdec)
        o = o * lax.rsqrt(jnp.mean(o * o, axis=-1, keepdims=True) + NORM_EPS) * gain_ref[:, sl]
        y_ref[rs, sl] = o * jax.nn.silu(gt_ref[rs, sl])


def hgrn_mixer(hproj, lb, out_gain, batch, seq):
    n, w4 = hproj.shape
    w = w4 // 4
    t = HG_STEP
    nb = seq // t
    col = lambda c: (lambda b, i: (b * nb + i, c))
    fix = lambda b, i: (0, 0)
    lvl = jnp.asarray(_level_table(HG_T))
    return pl.pallas_call(
        _hgrn_kernel,
        grid=(batch, nb),
        in_specs=[pl.BlockSpec((t, w), col(0)), pl.BlockSpec((t, w), col(1)),
                  pl.BlockSpec((t, w), col(2)), pl.BlockSpec((t, w), col(3)),
                  pl.BlockSpec((1, w), fix), pl.BlockSpec((1, w), fix),
                  pl.BlockSpec((HG_T, HG_T), fix)],
        out_specs=pl.BlockSpec((t, w), col(0)),
        out_shape=jax.ShapeDtypeStruct((n, w), F32),
        scratch_shapes=[pltpu.VMEM((HG_HEADS, HG_DK, HG_DK), F32)],
        compiler_params=_params(2),
        name="hgrn_mixer",
    )(hproj, hproj, hproj, hproj, lb.reshape(1, w), out_gain.reshape(1, w), lvl)


def _att_masks():
    blk = ATT_BLK
    rows = np.arange(blk)
    out = []
    for dil in ATT_DILATIONS:
        seg = blk * dil // ATT_CLASSES
        pos = (ATT_CLASSES // dil) * (rows % seg) + rows // seg
        kpos = np.concatenate([pos, blk + pos])
        dist = pos[:, None] + blk - kpos[None, :]
        out.append(((dist >= 0) & (dist <= blk)).astype(np.float32))
    return np.stack(out)


def _att_kernel(q_hbm, k_hbm, v_hbm, mask_ref, o_hbm,
                qbuf, kbuf, vbuf, acc, m_s, l_s, obuf, isem, osem):
    blk = ATT_BLK
    ncls = ATT_CLASSES
    npair = pl.num_programs(1)
    nsb = pl.num_programs(2)
    sb = pl.program_id(2)
    step = (pl.program_id(0) * npair + pl.program_id(1)) * nsb + sb
    n_steps = pl.num_programs(0) * npair * nsb
    lane = lax.broadcasted_iota(I32, (blk, LANES), 1)
    head0 = lane < ATT_DH
    col = lax.broadcasted_iota(I32, (blk, 2 * blk), 1)
    cur = step % 3
    prev = (step + 2) % 3

    def load_copies(s):
        row0 = pl.multiple_of(((s // (npair * nsb)) * nsb + s % nsb) * blk, blk)
        lane0 = pl.multiple_of(((s // nsb) % npair) * LANES, LANES)
        for r in range(ncls):
            dst = pl.ds(r * blk, blk)
            for hbm, buf, slot in ((q_hbm, qbuf, s % 2), (k_hbm, kbuf, s % 3), (v_hbm, vbuf, s % 3)):
                yield pltpu.make_async_copy(hbm.at[pl.ds(row0, blk), r, pl.ds(lane0, LANES)],
                                            buf.at[slot, dst, :], isem.at[s % 2])

    def store_copies(s):
        row0 = pl.multiple_of(((s // (npair * nsb)) * nsb + s % nsb) * blk, blk)
        lane0 = pl.multiple_of(((s // nsb) % npair) * LANES, LANES)
        for r in range(ncls):
            yield pltpu.make_async_copy(obuf.at[pl.ds(r * blk, blk), :],
                                        o_hbm.at[pl.ds(row0, blk), r, pl.ds(lane0, LANES)], osem.at[0])

    @pl.when(step == 0)
    def _():
        kbuf[2] = jnp.zeros(kbuf.shape[1:], kbuf.dtype)
        vbuf[2] = jnp.zeros(vbuf.shape[1:], vbuf.dtype)
        for cp in load_copies(step):
            cp.start()

    @pl.when(step + 1 < n_steps)
    def _():
        for cp in load_copies(step + 1):
            cp.start()

    for cp in load_copies(step):
        cp.wait()
    qs = step % 2

    def rows_of(buf, pieces, seg):
        return jnp.concatenate([buf[slot, pl.ds(o, seg), :] for slot, o in pieces], axis=0).astype(BF16)

    def block_attention(qb, kb, vb, valid):
        outs = []
        for hh in range(2):
            hm = head0 if hh == 0 else jnp.logical_not(head0)
            s = _dot_nt(jnp.where(hm, qb, jnp.zeros_like(qb)), kb)
            s = jnp.where(valid, s, NEG)
            m = jnp.max(s, axis=-1, keepdims=True)
            p = jnp.exp2(s - m)
            l = jnp.sum(p, axis=-1, keepdims=True)
            outs.append((_dot(p.astype(BF16), vb), m, l))
        (o0, m0, l0), (o1, m1, l1) = outs
        return (jnp.where(head0, o0, o1), jnp.where(head0, m0, m1), jnp.where(head0, l0, l1))

    def first_ok(is_first):
        return jnp.logical_or(col >= blk, jnp.logical_not(is_first))

    def pattern(p, dil):
        band = mask_ref[p] > 0.5
        nseg = ncls // dil
        seg = blk // nseg

        def body(idx, _):
            c = idx // nseg
            n = idx % nseg
            valid = jnp.logical_and(band, first_ok(jnp.logical_and(sb == 0, n == 0)))
            slot_p = jnp.where(n == 0, prev, cur)
            n_p = (n + nseg - 1) % nseg
            q_off = [pl.multiple_of((c + dil * k) * blk + seg * n, seg) for k in range(nseg)]
            p_off = [pl.multiple_of((c + dil * k) * blk + seg * n_p, seg) for k in range(nseg)]
            keys = [(slot_p, o) for o in p_off] + [(cur, o) for o in q_off]
            o, m, l = block_attention(rows_of(qbuf, [(qs, o) for o in q_off], seg),
                                      rows_of(kbuf, keys, seg), rows_of(vbuf, keys, seg), valid)
            for k, off in enumerate(q_off):
                rs = slice(k * seg, (k + 1) * seg)
                dst = pl.ds(off, seg)
                acc[p, dst, :] = o[rs]
                m_s[p, dst, :] = m[rs]
                l_s[p, dst, :] = l[rs]
            return 0

        lax.fori_loop(0, ncls, body, 0, unroll=True)

    for p, dil in enumerate(ATT_DILATIONS):
        pattern(p, dil)

    @pl.when(step > 0)
    def _():
        for cp in store_copies(step - 1):
            cp.wait()

    def combine(i, _):
        rows = pl.ds(pl.multiple_of(i * blk, blk), blk)
        ms = [m_s[p, rows, :] for p in range(len(ATT_DILATIONS))]
        top = functools.reduce(jnp.maximum, ms)
        ws = [jnp.exp2(m - top) for m in ms]
        num = sum(w * acc[p, rows, :] for p, w in enumerate(ws))
        den = sum(w * l_s[p, rows, :] for p, w in enumerate(ws))
        obuf[rows, :] = num / den
        return 0

    lax.fori_loop(0, ncls, combine, 0, unroll=2)
    for cp in store_copies(step):
        cp.start()

    @pl.when(step == n_steps - 1)
    def _():
        for cp in store_copies(step):
            cp.wait()


def dilated_attention(aq, ak, av, batch, seq):
    n, w = aq.shape
    nsb = seq // ATT_SUPER
    npair = w // LANES
    masks = jnp.asarray(_att_masks())
    view = lambda t: t.reshape(n // ATT_CLASSES, ATT_CLASSES, w)
    any_spec = pl.BlockSpec(memory_space=pl.ANY)
    out = pl.pallas_call(
        _att_kernel,
        grid=(batch, npair, nsb),
        in_specs=[any_spec, any_spec, any_spec, pl.BlockSpec(masks.shape, lambda b, hp, sb: (0, 0, 0))],
        out_specs=any_spec,
        out_shape=jax.ShapeDtypeStruct((n // ATT_CLASSES, ATT_CLASSES, w), F32),
        scratch_shapes=[pltpu.VMEM((2, ATT_SUPER, LANES), F32),
                        pltpu.VMEM((3, ATT_SUPER, LANES), F32),
                        pltpu.VMEM((3, ATT_SUPER, LANES), F32)]
        + [pltpu.VMEM((len(ATT_DILATIONS), ATT_SUPER, LANES), F32)] * 3
        + [pltpu.VMEM((ATT_SUPER, LANES), F32),
           pltpu.SemaphoreType.DMA((2,)), pltpu.SemaphoreType.DMA((1,))],
        compiler_params=_params(3),
        name="dilated_attention",
    )(view(aq), view(ak), view(av), masks)
    return out.reshape(n, w)


def _memkv_kernel(mem_ref, g_ref, w_ref, k_ref, v_ref):
    h = _rms(mem_ref[...], g_ref[...]).astype(BF16)
    d = k_ref.shape[1]
    k_ref[...] = _dot(h, w_ref[:, :d]).astype(BF16)
    v_ref[...] = _dot(h, w_ref[:, d:]).astype(BF16)


def memory_kv(mem2d, gain, wkv, batch):
    n, d = mem2d.shape
    m = n // batch
    row = lambda b: (b, 0)
    fix = lambda b: (0, 0)
    return pl.pallas_call(
        _memkv_kernel,
        grid=(batch,),
        in_specs=[pl.BlockSpec((m, d), row), pl.BlockSpec((1, d), fix), pl.BlockSpec(wkv.shape, fix)],
        out_specs=[pl.BlockSpec((m, d), row)] * 2,
        out_shape=[jax.ShapeDtypeStruct((n, d), BF16)] * 2,
        compiler_params=_params(1),
        name="memory_kv",
    )(mem2d, gain.reshape(1, d), wkv)


def _cross_kernel(x_ref, yh_ref, ya_ref, wout_ref, gx_ref, wq_ref, kx_ref, vx_ref, wo_ref,
                  gf_ref, wr_ref, xe_ref, bkt_ref, cnt_ref):
    d = x_ref.shape[1]
    hw = yh_ref.shape[1]
    x = (x_ref[...] + _dot(yh_ref[...].astype(BF16), wout_ref[:hw, :])
         + _dot(ya_ref[...].astype(BF16), wout_ref[hw:, :]))
    h = _rms(x, gx_ref[...]).astype(BF16)
    q = _dot(h, wq_ref[...])
    dh = d // X_HEADS
    heads = []
    for hh in range(X_HEADS):
        sl = slice(hh * dh, (hh + 1) * dh)
        s = _dot_nt(q[:, sl].astype(BF16), kx_ref[:, sl]) * (dh ** -0.5)
        s = s - jnp.max(s, axis=-1, keepdims=True)
        p = jnp.exp(s)
        p = p / jnp.sum(p, axis=-1, keepdims=True)
        heads.append(_dot(p.astype(BF16), vx_ref[:, sl]).astype(BF16))
    x = x + _dot(jnp.concatenate(heads, axis=1), wo_ref[...])
    xe_ref[:, :d] = x

    hf = _rms(x, gf_ref[...])
    h_hi, h_lo = _split_bf16(hf)
    w_hi, w_lo = _split_bf16(wr_ref[...])
    both = _dot(h_hi, jnp.concatenate([w_hi, w_lo], axis=1))
    logits = both[:, :LANES] + both[:, LANES:] + _dot(h_lo, w_hi)
    lane = lax.broadcasted_iota(I32, logits.shape, 1)
    big = 1 << 20

    def first_max(vals):
        top = jnp.max(vals, axis=-1, keepdims=True)
        return top, jnp.min(jnp.where(vals == top, lane, big), axis=-1, keepdims=True)

    is_g = lane < N_GROUPS
    g_top, g_sel = first_max(jnp.where(is_g, logits, NEG))
    g_w = 1.0 / jnp.sum(jnp.where(is_g, jnp.exp(logits - g_top), 0.0), axis=-1, keepdims=True)
    e_lo = N_GROUPS + EXPERTS_PER_GROUP * g_sel
    in_grp = jnp.logical_and(lane >= e_lo, lane < e_lo + EXPERTS_PER_GROUP)
    e_log = jnp.where(in_grp, logits, NEG)
    v1, i1 = first_max(e_log)
    v2, i2 = first_max(jnp.where(lane == i1, NEG, e_log))
    t = jnp.exp(v2 - v1)
    w1 = g_w / (1.0 + t)
    w2 = g_w * t / (1.0 + t)
    j1 = i1 - e_lo
    j2 = i2 - e_lo
    lo = jnp.minimum(j1, j2)
    hi = jnp.maximum(j1, j2)
    pair = jnp.where(lo == 0, hi - 1, jnp.where(lo == 1, hi + 1, N_PAIRS - 1))
    bucket = g_sel * N_PAIRS + pair
    w_lo = jnp.where(j1 < j2, w1, w2)
    w_hi = jnp.where(j1 < j2, w2, w1)
    xe_ref[:, d:] = jnp.where(lane == 0, w_lo, jnp.where(lane == 1, w_hi, 0.0))
    bkt_ref[...] = bucket
    cnt_ref[0] = jnp.sum((lane == bucket).astype(F32), axis=0, keepdims=True)


def cross_block(x2d, y_hg, y_att, w_out, g_cross, wq, kx, vx, wo, g_ffn, w_router, batch):
    n, d = x2d.shape
    tm = X_TILE
    nt = n // tm
    per_batch = nt // batch
    m = kx.shape[0] // batch
    row = lambda i: (i, 0)
    fix = lambda i: (0, 0)
    mem = lambda i: (i // per_batch, 0)
    return pl.pallas_call(
        _cross_kernel,
        grid=(nt,),
        in_specs=[pl.BlockSpec((tm, d), row), pl.BlockSpec((tm, y_hg.shape[1]), row),
                  pl.BlockSpec((tm, y_att.shape[1]), row), pl.BlockSpec(w_out.shape, fix),
                  pl.BlockSpec((1, d), fix), pl.BlockSpec(wq.shape, fix),
                  pl.BlockSpec((m, d), mem), pl.BlockSpec((m, d), mem), pl.BlockSpec(wo.shape, fix),
                  pl.BlockSpec((1, d), fix), pl.BlockSpec(w_router.shape, fix)],
        out_specs=[pl.BlockSpec((tm, d + LANES), row), pl.BlockSpec((tm, 1), row),
                   pl.BlockSpec((1, 1, LANES), lambda i: (i, 0, 0))],
        out_shape=[jax.ShapeDtypeStruct((n, d + LANES), F32), jax.ShapeDtypeStruct((n, 1), I32),
                   jax.ShapeDtypeStruct((nt, 1, LANES), F32)],
        compiler_params=_params(1),
        name="cross_block",
    )(x2d, y_hg, y_att, w_out, g_cross.reshape(1, d), wq, kx, vx, wo, g_ffn.reshape(1, d), w_router)


def _position_kernel(bkt_ref, base_ref, tri_ref, pos_ref):
    lane = lax.broadcasted_iota(I32, (bkt_ref.shape[0], LANES), 1)
    onehot = (lane == bkt_ref[...]).astype(F32)
    before = _dot(tri_ref[...], onehot.astype(BF16))
    pos = jnp.sum(onehot * (before + base_ref[0]), axis=-1, keepdims=True)
    pos_ref[...] = pos.astype(I32)


def sorted_positions(bucket, base):
    n = bucket.shape[0]
    nt = base.shape[0]
    tm = n // nt
    tri = jnp.asarray(np.tril(np.ones((tm, tm), np.float32), -1), BF16)
    return pl.pallas_call(
        _position_kernel,
        grid=(nt,),
        in_specs=[pl.BlockSpec((tm, 1), lambda i: (i, 0)),
                  pl.BlockSpec((1, 1, LANES), lambda i: (i, 0, 0)),
                  pl.BlockSpec((tm, tm), lambda i: (0, 0))],
        out_specs=pl.BlockSpec((tm, 1), lambda i: (i, 0)),
        out_shape=jax.ShapeDtypeStruct((n, 1), I32),
        compiler_params=_params(1),
        name="sorted_positions",
    )(bucket, base, tri)


def _moe_kernel(pos_ref, ea_ref, eb_ref, cnt_ref,
                x_hbm, g_ref, wga_ref, wgb_ref, wua_ref, wub_ref, wda_ref, wdb_ref, gfin_ref,
                out_hbm, src_ref, xbuf, obuf, gsem, ssem, *, final_norm):
    j = pl.program_id(0)
    nt = pl.num_programs(0)
    rows = xbuf.shape[1]
    d = obuf.shape[2]
    slot = j % 2
    other = 1 - slot

    def for_rows(n, fn):
        for g in range(rows // COPY_GROUP):
            @pl.when((g + 1) * COPY_GROUP <= n)
            def _():
                for r in range(g * COPY_GROUP, (g + 1) * COPY_GROUP):
                    fn(r, r % DMA_PRIORITIES)

        def single(r, _):
            fn(r, 0)
            return 0

        lax.fori_loop(n // COPY_GROUP * COPY_GROUP, n, single, 0)

    def wait_rows(n, bulk_copy, row_copy):
        n8 = pl.multiple_of(n // SUBLANES * SUBLANES, SUBLANES)

        @pl.when(n8 > 0)
        def _():
            bulk_copy(n8).wait()

        def single(r, _):
            row_copy.wait()
            return 0

        lax.fori_loop(n8, n, single, 0)

    def start_gather(tile, n, sl):
        def copy(r, priority):
            tok = src_ref[tile * rows + r]
            pltpu.make_async_copy(x_hbm.at[pl.ds(tok, 1)], xbuf.at[sl, pl.ds(r, 1)],
                                  gsem.at[sl]).start(priority=priority)
        for_rows(n, copy)

    def start_scatter(tile, n, sl):
        def copy(r, priority):
            tok = src_ref[tile * rows + r]
            pltpu.make_async_copy(obuf.at[sl, pl.ds(r, 1)], out_hbm.at[pl.ds(tok, 1)],
                                  ssem.at[sl]).start(priority=priority)
        for_rows(n, copy)

    def wait_gather(n, sl):
        wait_rows(n,
                  lambda m: pltpu.make_async_copy(x_hbm.at[pl.ds(0, m)], xbuf.at[sl, pl.ds(0, m)], gsem.at[sl]),
                  pltpu.make_async_copy(x_hbm.at[pl.ds(0, 1)], xbuf.at[sl, pl.ds(0, 1)], gsem.at[sl]))

    def wait_scatter(n, sl):
        wait_rows(n,
                  lambda m: pltpu.make_async_copy(obuf.at[sl, pl.ds(0, m)], out_hbm.at[pl.ds(0, m)], ssem.at[sl]),
                  pltpu.make_async_copy(obuf.at[sl, pl.ds(0, 1)], out_hbm.at[pl.ds(0, 1)], ssem.at[sl]))

    @pl.when(j == 0)
    def _():
        def fill(t, _):
            src_ref[pos_ref[t]] = t
            return 0

        lax.fori_loop(0, pos_ref.shape[0], fill, 0, unroll=8)
        xbuf[...] = jnp.zeros_like(xbuf)
        start_gather(0, cnt_ref[0], 0)

    used = cnt_ref[j] > 0
    nxt = jnp.minimum(j + 1, nt - 1)
    n_next = jnp.where(j + 1 < nt, cnt_ref[nxt], 0)
    n_prev = cnt_ref[jnp.maximum(j - 1, 0)]
    last_used = jnp.logical_and(used, n_next == 0)

    for static_slot in range(2):
        @pl.when(jnp.logical_and(used, slot == static_slot))
        def _():
            wait_gather(cnt_ref[j], static_slot)

            @pl.when(j >= 2)
            def _():
                wait_scatter(cnt_ref[jnp.maximum(j - 2, 0)], static_slot)

            start_gather(nxt, n_next, 1 - static_slot)

            @pl.when(j >= 1)
            def _():
                start_scatter(j - 1, n_prev, 1 - static_slot)

    @pl.when(cnt_ref[j] > 0)
    def _():
        xe = xbuf[slot]
        x = xe[:, :d]
        h = _rms(x, g_ref[...]).astype(BF16)
        lane = lax.broadcasted_iota(I32, (rows, LANES), 1)
        wts = xe[:, d:]
        w_a = jnp.sum(jnp.where(lane == 0, wts, 0.0), axis=-1, keepdims=True)
        w_b = jnp.sum(jnp.where(lane == 1, wts, 0.0), axis=-1, keepdims=True)
        act_a = (jax.nn.silu(_dot(h, wga_ref[...])) * _dot(h, wua_ref[...]) * w_a).astype(BF16)
        act_b = (jax.nn.silu(_dot(h, wgb_ref[...])) * _dot(h, wub_ref[...]) * w_b).astype(BF16)
        y = x + _dot(act_a, wda_ref[...]) + _dot(act_b, wdb_ref[...])
        if final_norm:
            y = _rms(y, gfin_ref[...])
        obuf[slot] = y

    @pl.when(last_used)
    def _():
        start_scatter(j, cnt_ref[j], slot)

        @pl.when(j >= 1)
        def _():
            wait_scatter(n_prev, other)

        wait_scatter(cnt_ref[j], slot)


def moe_ffn(x_ext, pos, tile_a, tile_b, tile_cnt, g_ffn, w_gate, w_up, w_down, g_final, final_norm):
    n, de = x_ext.shape
    d = de - LANES
    rows = MOE_TILE
    nt = tile_cnt.shape[0]
    ff = w_gate.shape[2]
    fix = lambda j, *_: (0, 0)
    exp_a = lambda j, pos, ea, eb, cnt: (ea[j], 0, 0)
    exp_b = lambda j, pos, ea, eb, cnt: (eb[j], 0, 0)
    up_spec = lambda im: pl.BlockSpec((None, d, ff), im)
    down_spec = lambda im: pl.BlockSpec((None, ff, d), im)
    grid_spec = pltpu.PrefetchScalarGridSpec(
        num_scalar_prefetch=4,
        grid=(nt,),
        in_specs=[pl.BlockSpec(memory_space=pl.ANY), pl.BlockSpec((1, d), fix),
                  up_spec(exp_a), up_spec(exp_b), up_spec(exp_a), up_spec(exp_b),
                  down_spec(exp_a), down_spec(exp_b), pl.BlockSpec((1, d), fix)],
        out_specs=pl.BlockSpec(memory_space=pl.ANY),
        scratch_shapes=[pltpu.SMEM((nt * rows,), I32),
                        pltpu.VMEM((2, rows, de), F32), pltpu.VMEM((2, rows, d), F32),
                        pltpu.SemaphoreType.DMA((2,)), pltpu.SemaphoreType.DMA((2,))],
    )
    return pl.pallas_call(
        functools.partial(_moe_kernel, final_norm=final_norm),
        grid_spec=grid_spec,
        out_shape=jax.ShapeDtypeStruct((n, d), F32),
        compiler_params=_params(1),
        name="moe_ffn",
    )(pos, tile_a, tile_b, tile_cnt, x_ext, g_ffn.reshape(1, d),
      w_gate, w_gate, w_up, w_up, w_down, w_down, g_final.reshape(1, d))


_PAIR_LO = np.array([0, 0, 0, 1, 1, 2], np.int32)
_PAIR_HI = np.array([1, 2, 3, 2, 3, 3], np.int32)


def _tile_plan(counts, n_tokens):
    rows = MOE_TILE
    counts = counts.astype(I32)
    total = jnp.sum(counts, axis=0)
    padded = (total + rows - 1) // rows * rows
    start = jnp.cumsum(padded) - padded
    base = start[None, :] + jnp.cumsum(counts, axis=0) - counts
    nt = n_tokens // rows + N_BUCKETS
    first_row = jnp.arange(nt, dtype=I32) * rows
    end = (start + padded)[:N_BUCKETS]
    bucket = jnp.sum((first_row[:, None] >= end[None, :]).astype(I32), axis=1)
    used = bucket < N_BUCKETS
    last_used = jnp.max(jnp.where(used, bucket, 0))
    b_eff = jnp.where(used, bucket, last_used)
    cnt = jnp.where(used, jnp.clip(total[b_eff] - (first_row - start[b_eff]), 0, rows), 0)
    grp = b_eff // N_PAIRS
    pair = b_eff % N_PAIRS
    exp_a = grp * EXPERTS_PER_GROUP + jnp.asarray(_PAIR_LO)[pair]
    exp_b = grp * EXPERTS_PER_GROUP + jnp.asarray(_PAIR_HI)[pair]
    return base.astype(F32), exp_a.astype(I32), exp_b.astype(I32), cnt.astype(I32)


def kernel(x, mem, positions, norm_mix, w_in, hg_lower_bounds, hg_out_norm, w_out, norm_cross, norm_mem,
           wq_x, wkv_x, wo_x, norm_ffn, w_router_group, w_router_expert, w_gate, w_up, w_down, norm_final):
    batch, seq, d = x.shape
    depth = w_in.shape[0]
    n = batch * seq
    hg_width = hg_lower_bounds.shape[1]
    att_width = (w_in.shape[2] - 4 * hg_width) // 3
    assert hg_width == HG_HEADS * HG_DK and seq % ATT_SUPER == 0 and n % TOK_TILE == 0

    lb_sm = jax.nn.softmax(hg_lower_bounds.astype(F32), axis=0)
    lbs = jnp.cumsum(lb_sm, axis=0) - lb_sm[0:1]
    cos, sin = rope_tables(positions)
    xs = x.reshape(n, d)
    mem2d = mem.reshape(-1, d)
    w_router = jnp.concatenate([w_router_group, w_router_expert], axis=-1)
    w_router = jnp.pad(w_router, ((0, 0), (0, 0), (0, LANES - w_router.shape[-1])))
    wg16 = w_gate.astype(BF16).reshape((-1,) + w_gate.shape[2:])
    wu16 = w_up.astype(BF16).reshape((-1,) + w_up.shape[2:])
    wd16 = w_down.astype(BF16).reshape((-1,) + w_down.shape[2:])

    for l in range(depth):
        hproj, aq, ak, av = in_projection(xs, norm_mix[l], w_in[l].astype(BF16), cos, sin, hg_width, att_width)
        y_hg = hgrn_mixer(hproj, lbs[l], hg_out_norm[l], batch, seq)
        y_att = dilated_attention(aq, ak, av, batch, seq)
        kx, vx = memory_kv(mem2d, norm_mem[l], wkv_x[l].astype(BF16), batch)
        x_ext, bucket, counts = cross_block(xs, y_hg, y_att, w_out[l].astype(BF16), norm_cross[l],
                                            wq_x[l].astype(BF16), kx, vx, wo_x[l].astype(BF16),
                                            norm_ffn[l], w_router[l], batch)
        base, exp_a, exp_b, cnt = _tile_plan(counts.reshape(n // POS_TILE, -1, LANES).sum(axis=1), n)
        pos = sorted_positions(bucket, base.reshape(base.shape[0], 1, LANES))
        xs = moe_ffn(x_ext, pos.reshape(n), exp_a + l * N_EXPERTS, exp_b + l * N_EXPERTS, cnt, norm_ffn[l],
                     wg16, wu16, wd16, norm_final, final_norm=(l == depth - 1))
    return xs.reshape(batch, seq, d)
```

```python
import functools

import numpy as np
import jax
import jax.numpy as jnp
from jax import lax
from jax.experimental import pallas as pl
from jax.experimental.pallas import tpu as pltpu

F32 = jnp.float32
BF16 = jnp.bfloat16
I32 = jnp.int32

NORM_EPS = 1e-6
ROPE_THETA = 10000.0
HG_HEADS = 4
HG_DK = 128
ATT_DH = 64
ATT_BLK = 128
ATT_DILATIONS = (1, 4, 16)
ATT_CLASSES = 16
ATT_SUPER = ATT_BLK * ATT_CLASSES
X_HEADS = 4
N_GROUPS = 4
EXPERTS_PER_GROUP = 4
N_EXPERTS = 16
N_PAIRS = 6
N_BUCKETS = N_GROUPS * N_PAIRS
EXPERT_FF = 512
LANES = 128
SUBLANES = 8
NEG = -1e30
LOG2_E = 1.4426950408889634

HG_T = 128
HG_STEP = 512
TOK_TILE = 512
X_TILE = 1024
POS_TILE = 1024
MOE_TILE = 256
DMA_PRIORITIES = 2
COPY_GROUP = 32
VMEM_LIMIT = 56 * 1024 * 1024


def _dot(a, b):
    return jnp.dot(a, b, preferred_element_type=F32)


def _dot_nt(a, b):
    return lax.dot_general(a, b, (((1,), (1,)), ((), ())), preferred_element_type=F32)


def _dot_tn(a, b):
    return lax.dot_general(a, b, (((0,), (0,)), ((), ())), preferred_element_type=F32)


def _rms(x, g):
    return x * lax.rsqrt(jnp.mean(x * x, axis=-1, keepdims=True) + NORM_EPS) * g


def _split_bf16(a):
    hi = a.astype(BF16)
    return hi, (a - hi.astype(F32)).astype(BF16)


def _params(n_axes):
    return pltpu.CompilerParams(dimension_semantics=("arbitrary",) * n_axes,
                                vmem_limit_bytes=VMEM_LIMIT)


def _rope_kernel(pos_ref, inv_ref, cos_ref, sin_ref):
    nfreq = ATT_DH // 2
    ngrp = LANES // nfreq
    q = pos_ref.shape[0] // ngrp
    lane = lax.broadcasted_iota(I32, (q, LANES), 1)
    grp = lane // nfreq
    pos = pos_ref[...].astype(F32)
    packed = pos[0:q]
    for k in range(1, ngrp):
        packed = jnp.where(grp == k, pos[k * q:(k + 1) * q], packed)
    ang = packed * inv_ref[...]
    c = jnp.cos(ang)
    s = jnp.sin(ang)
    neg = (lane % ATT_DH) < nfreq

    def spread(t, k):
        out = pltpu.roll(t, (-k * nfreq) % LANES, axis=1)
        for j in range(1, ngrp):
            out = jnp.where(grp == j, pltpu.roll(t, ((j - k) * nfreq) % LANES, axis=1), out)
        return out

    for k in range(ngrp):
        cos_ref[k * q:(k + 1) * q, :] = spread(c, k)
        sk = spread(s, k)
        sin_ref[k * q:(k + 1) * q, :] = jnp.where(neg, -sk, sk)


def rope_tables(positions):
    n = positions.size
    inv = ROPE_THETA ** (-jnp.arange(0, ATT_DH, 2, dtype=F32) / ATT_DH)
    inv = jnp.tile(inv, LANES // (ATT_DH // 2)).reshape(1, LANES)
    tm = TOK_TILE
    return pl.pallas_call(
        _rope_kernel,
        grid=(n // tm,),
        in_specs=[pl.BlockSpec((tm, 1), lambda i: (i, 0)),
                  pl.BlockSpec((1, LANES), lambda i: (0, 0))],
        out_specs=[pl.BlockSpec((tm, LANES), lambda i: (i, 0))] * 2,
        out_shape=[jax.ShapeDtypeStruct((n, LANES), F32)] * 2,
        compiler_params=_params(1),
        name="rope_tables",
    )(positions.reshape(n, 1), inv)


def _inproj_kernel(x_ref, g_ref, w_ref, cos_ref, sin_ref, hp_ref, q_ref, k_ref, v_ref):
    h = _rms(x_ref[...], g_ref[...]).astype(BF16)
    hgw = hp_ref.shape[1]
    aw = q_ref.shape[1]
    for c in range(hgw // aw):
        hp_ref[:, c * aw:(c + 1) * aw] = _dot(h, w_ref[:, c * aw:(c + 1) * aw])
    cos = cos_ref[...]
    sin = sin_ref[...]
    lane = lax.broadcasted_iota(I32, cos.shape, 1)
    first = (lane % ATT_DH) < ATT_DH // 2

    def rope(t, scale):
        for p in range(aw // LANES):
            tp = t[:, p * LANES:(p + 1) * LANES]
            rot = jnp.where(first, pltpu.roll(tp, LANES - ATT_DH // 2, axis=1),
                            pltpu.roll(tp, ATT_DH // 2, axis=1))
            yield (tp * cos + rot * sin) * scale

    aq = _dot(h, w_ref[:, hgw:hgw + aw])
    for p, blk in enumerate(rope(aq, ATT_DH ** -0.5 * LOG2_E)):
        q_ref[:, p * LANES:(p + 1) * LANES] = blk
    ak = _dot(h, w_ref[:, hgw + aw:hgw + 2 * aw])
    for p, blk in enumerate(rope(ak, 1.0)):
        k_ref[:, p * LANES:(p + 1) * LANES] = blk
    v_ref[...] = _dot(h, w_ref[:, hgw + 2 * aw:hgw + 3 * aw])


def in_projection(x2d, gain, w_in, cos, sin, hg_width, att_width):
    n, d = x2d.shape
    tm = TOK_TILE
    row = lambda i: (i, 0)
    fix = lambda i: (0, 0)
    return pl.pallas_call(
        _inproj_kernel,
        grid=(n // tm,),
        in_specs=[pl.BlockSpec((tm, d), row), pl.BlockSpec((1, d), fix),
                  pl.BlockSpec(w_in.shape, fix),
                  pl.BlockSpec((tm, LANES), row), pl.BlockSpec((tm, LANES), row)],
        out_specs=[pl.BlockSpec((tm, 4 * hg_width), row)] + [pl.BlockSpec((tm, att_width), row)] * 3,
        out_shape=[jax.ShapeDtypeStruct((n, 4 * hg_width), F32)]
        + [jax.ShapeDtypeStruct((n, att_width), F32)] * 3,
        compiler_params=_params(1),
        name="in_projection",
    )(x2d, gain.reshape(1, d), w_in, cos, sin)


def _level_table(t):
    ti = np.arange(t)[:, None]
    si = np.arange(t)[None, :]
    x = np.maximum(ti ^ si, 1)
    lvl = np.floor(np.log2(x)).astype(np.int32)
    diag = int(np.log2(t))
    return np.where(si < ti, lvl, np.where(si == ti, diag, -1)).astype(np.int32)


def _hgrn_kernel(q_ref, f_ref, i_ref, gt_ref, lb_ref, gain_ref, lvl_ref, y_ref, st_ref):
    t_rows = lvl_ref.shape[0]
    n_lev = t_rows.bit_length() - 1

    @pl.when(pl.program_id(1) == 0)
    def _():
        st_ref[...] = jnp.zeros_like(st_ref)

    row = lax.broadcasted_iota(I32, (t_rows, HG_DK), 0)
    lvl = lvl_ref[...]
    lb_all = lb_ref[...]
    for u, h in [(u, h) for u in range(q_ref.shape[0] // t_rows) for h in range(HG_HEADS)]:
        rs = slice(u * t_rows, (u + 1) * t_rows)
        sl = slice(h * HG_DK, (h + 1) * HG_DK)
        lb = lb_all[:, sl]
        f = lb + (1.0 - lb) * jax.nn.sigmoid(f_ref[rs, sl])
        g = jnp.log(f) * LOG2_E
        b = g
        s = 1
        while s < t_rows:
            b = b + jnp.where(row >= s, pltpu.roll(b, s, axis=0), 0.0)
            s *= 2
        kk = 1.0 - f
        q = q_ref[rs, sl]
        v16 = i_ref[rs, sl].astype(BF16)
        scores = jnp.where(lvl == n_lev, _dot_nt(q.astype(BF16), kk.astype(BF16)), 0.0)
        first = b - g
        last = b
        for j in range(n_lev):
            if j > 0:
                half = 1 << (j - 1)
                bit = ((row >> (j - 1)) & 1) == 1
                first = jnp.where(bit, pltpu.roll(first, half, axis=0), first)
                last = jnp.where(bit, last, pltpu.roll(last, t_rows - half, axis=0))
            qj = (q * jnp.exp2(b - first)).astype(BF16)
            kj = (kk * jnp.exp2(last - b)).astype(BF16)
            scores = jnp.where(lvl == j, _dot_nt(qj, kj), scores)
        st = st_ref[h]
        o = _dot(scores.astype(BF16), v16) + _dot_nt((q * jnp.exp2(b)).astype(BF16), st.astype(BF16))
        b_last = b[t_rows - 1:t_rows, :]
        kdec = (kk * jnp.exp2(b_last - b)).astype(BF16)
        st_ref[h] = jnp.exp2(b_last) * st + _dot_tn(v16, kdec)
        o = o * lax.rsqrt(jnp.mean(o * o, axis=-1, keepdims=True) + NORM_EPS) * gain_ref[:, sl]
        y_ref[rs, sl] = o * jax.nn.silu(gt_ref[rs, sl])


def hgrn_mixer(hproj, lb, out_gain, batch, seq):
    n, w4 = hproj.shape
    w = w4 // 4
    t = HG_STEP
    nb = seq // t
    col = lambda c: (lambda b, i: (b * nb + i, c))
    fix = lambda b, i: (0, 0)
    lvl = jnp.asarray(_level_table(HG_T))
    return pl.pallas_call(
        _hgrn_kernel,
        grid=(batch, nb),
        in_specs=[pl.BlockSpec((t, w), col(0)), pl.BlockSpec((t, w), col(1)),
                  pl.BlockSpec((t, w), col(2)), pl.BlockSpec((t, w), col(3)),
                  pl.BlockSpec((1, w), fix), pl.BlockSpec((1, w), fix),
                  pl.BlockSpec((HG_T, HG_T), fix)],
        out_specs=pl.BlockSpec((t, w), col(0)),
        out_shape=jax.ShapeDtypeStruct((n, w), F32),
        scratch_shapes=[pltpu.VMEM((HG_HEADS, HG_DK, HG_DK), F32)],
        compiler_params=_params(2),
        name="hgrn_mixer",
    )(hproj, hproj, hproj, hproj, lb.reshape(1, w), out_gain.reshape(1, w), lvl)


def _att_masks():
    blk = ATT_BLK
    rows = np.arange(blk)
    out = []
    for dil in ATT_DILATIONS:
        seg = blk * dil // ATT_CLASSES
        pos = (ATT_CLASSES // dil) * (rows % seg) + rows // seg
        kpos = np.concatenate([pos, blk + pos])
        dist = pos[:, None] + blk - kpos[None, :]
        out.append(((dist >= 0) & (dist <= blk)).astype(np.float32))
    return np.stack(out)


def _att_kernel(q_hbm, k_hbm, v_hbm, mask_ref, o_hbm,
                qbuf, kbuf, vbuf, acc, m_s, l_s, obuf, isem, osem):
    blk = ATT_BLK
    ncls = ATT_CLASSES
    npair = pl.num_programs(1)
    nsb = pl.num_programs(2)
    sb = pl.program_id(2)
    step = (pl.program_id(0) * npair + pl.program_id(1)) * nsb + sb
    n_steps = pl.num_programs(0) * npair * nsb
    lane = lax.broadcasted_iota(I32, (blk, LANES), 1)
    head0 = lane < ATT_DH
    col = lax.broadcasted_iota(I32, (blk, 2 * blk), 1)
    cur = step % 3
    prev = (step + 2) % 3

    def load_copies(s):
        row0 = pl.multiple_of(((s // (npair * nsb)) * nsb + s % nsb) * blk, blk)
        lane0 = pl.multiple_of(((s // nsb) % npair) * LANES, LANES)
        for r in range(ncls):
            dst = pl.ds(r * blk, blk)
            for hbm, buf, slot in ((q_hbm, qbuf, s % 2), (k_hbm, kbuf, s % 3), (v_hbm, vbuf, s % 3)):
                yield pltpu.make_async_copy(hbm.at[pl.ds(row0, blk), r, pl.ds(lane0, LANES)],
                                            buf.at[slot, dst, :], isem.at[s % 2])

    def store_copies(s):
        row0 = pl.multiple_of(((s // (npair * nsb)) * nsb + s % nsb) * blk, blk)
        lane0 = pl.multiple_of(((s // nsb) % npair) * LANES, LANES)
        for r in range(ncls):
            yield pltpu.make_async_copy(obuf.at[pl.ds(r * blk, blk), :],
                                        o_hbm.at[pl.ds(row0, blk), r, pl.ds(lane0, LANES)], osem.at[0])

    @pl.when(step == 0)
    def _():
        kbuf[2] = jnp.zeros(kbuf.shape[1:], kbuf.dtype)
        vbuf[2] = jnp.zeros(vbuf.shape[1:], vbuf.dtype)
        for cp in load_copies(step):
            cp.start()

    @pl.when(step + 1 < n_steps)
    def _():
        for cp in load_copies(step + 1):
            cp.start()

    for cp in load_copies(step):
        cp.wait()
    qs = step % 2

    def rows_of(buf, pieces, seg):
        return jnp.concatenate([buf[slot, pl.ds(o, seg), :] for slot, o in pieces], axis=0).astype(BF16)

    def block_attention(qb, kb, vb, valid):
        outs = []
        for hh in range(2):
            hm = head0 if hh == 0 else jnp.logical_not(head0)
            s = _dot_nt(jnp.where(hm, qb, jnp.zeros_like(qb)), kb)
            s = jnp.where(valid, s, NEG)
            m = jnp.max(s, axis=-1, keepdims=True)
            p = jnp.exp2(s - m)
            l = jnp.sum(p, axis=-1, keepdims=True)
            outs.append((_dot(p.astype(BF16), vb), m, l))
        (o0, m0, l0), (o1, m1, l1) = outs
        return (jnp.where(head0, o0, o1), jnp.where(head0, m0, m1), jnp.where(head0, l0, l1))

    def first_ok(is_first):
        return jnp.logical_or(col >= blk, jnp.logical_not(is_first))

    def pattern(p, dil):
        band = mask_ref[p] > 0.5
        nseg = ncls // dil
        seg = blk // nseg

        def body(idx, _):
            c = idx // nseg
            n = idx % nseg
            valid = jnp.logical_and(band, first_ok(jnp.logical_and(sb == 0, n == 0)))
            slot_p = jnp.where(n == 0, prev, cur)
            n_p = (n + nseg - 1) % nseg
            q_off = [pl.multiple_of((c + dil * k) * blk + seg * n, seg) for k in range(nseg)]
            p_off = [pl.multiple_of((c + dil * k) * blk + seg * n_p, seg) for k in range(nseg)]
            keys = [(slot_p, o) for o in p_off] + [(cur, o) for o in q_off]
            o, m, l = block_attention(rows_of(qbuf, [(qs, o) for o in q_off], seg),
                                      rows_of(kbuf, keys, seg), rows_of(vbuf, keys, seg), valid)
            for k, off in enumerate(q_off):
                rs = slice(k * seg, (k + 1) * seg)
                dst = pl.ds(off, seg)
                acc[p, dst, :] = o[rs]
                m_s[p, dst, :] = m[rs]
                l_s[p, dst, :] = l[rs]
            return 0

        lax.fori_loop(0, ncls, body, 0, unroll=True)

    for p, dil in enumerate(ATT_DILATIONS):
        pattern(p, dil)

    @pl.when(step > 0)
    def _():
        for cp in store_copies(step - 1):
            cp.wait()

    def combine(i, _):
        rows = pl.ds(pl.multiple_of(i * blk, blk), blk)
        ms = [m_s[p, rows, :] for p in range(len(ATT_DILATIONS))]
        top = functools.reduce(jnp.maximum, ms)
        ws = [jnp.exp2(m - top) for m in ms]
        num = sum(w * acc[p, rows, :] for p, w in enumerate(ws))
        den = sum(w * l_s[p, rows, :] for p, w in enumerate(ws))
        obuf[rows, :] = num / den
        return 0

    lax.fori_loop(0, ncls, combine, 0, unroll=2)
    for cp in store_copies(step):
        cp.start()

    @pl.when(step == n_steps - 1)
    def _():
        for cp in store_copies(step):
            cp.wait()


def dilated_attention(aq, ak, av, batch, seq):
    n, w = aq.shape
    nsb = seq // ATT_SUPER
    npair = w // LANES
    masks = jnp.asarray(_att_masks())
    view = lambda t: t.reshape(n // ATT_CLASSES, ATT_CLASSES, w)
    any_spec = pl.BlockSpec(memory_space=pl.ANY)
    out = pl.pallas_call(
        _att_kernel,
        grid=(batch, npair, nsb),
        in_specs=[any_spec, any_spec, any_spec, pl.BlockSpec(masks.shape, lambda b, hp, sb: (0, 0, 0))],
        out_specs=any_spec,
        out_shape=jax.ShapeDtypeStruct((n // ATT_CLASSES, ATT_CLASSES, w), F32),
        scratch_shapes=[pltpu.VMEM((2, ATT_SUPER, LANES), F32),
                        pltpu.VMEM((3, ATT_SUPER, LANES), F32),
                        pltpu.VMEM((3, ATT_SUPER, LANES), F32)]
        + [pltpu.VMEM((len(ATT_DILATIONS), ATT_SUPER, LANES), F32)] * 3
        + [pltpu.VMEM((ATT_SUPER, LANES), F32),
           pltpu.SemaphoreType.DMA((2,)), pltpu.SemaphoreType.DMA((1,))],
        compiler_params=_params(3),
        name="dilated_attention",
    )(view(aq), view(ak), view(av), masks)
    return out.reshape(n, w)


def _memkv_kernel(mem_ref, g_ref, w_ref, k_ref, v_ref):
    h = _rms(mem_ref[...], g_ref[...]).astype(BF16)
    d = k_ref.shape[1]
    k_ref[...] = _dot(h, w_ref[:, :d]).astype(BF16)
    v_ref[...] = _dot(h, w_ref[:, d:]).astype(BF16)


def memory_kv(mem2d, gain, wkv, batch):
    n, d = mem2d.shape
    m = n // batch
    row = lambda b: (b, 0)
    fix = lambda b: (0, 0)
    return pl.pallas_call(
        _memkv_kernel,
        grid=(batch,),
        in_specs=[pl.BlockSpec((m, d), row), pl.BlockSpec((1, d), fix), pl.BlockSpec(wkv.shape, fix)],
        out_specs=[pl.BlockSpec((m, d), row)] * 2,
        out_shape=[jax.ShapeDtypeStruct((n, d), BF16)] * 2,
        compiler_params=_params(1),
        name="memory_kv",
    )(mem2d, gain.reshape(1, d), wkv)


def _cross_kernel(x_ref, yh_ref, ya_ref, wout_ref, gx_ref, wq_ref, kx_ref, vx_ref, wo_ref,
                  gf_ref, wr_ref, xe_ref, bkt_ref, cnt_ref):
    d = x_ref.shape[1]
    hw = yh_ref.shape[1]
    x = (x_ref[...] + _dot(yh_ref[...].astype(BF16), wout_ref[:hw, :])
         + _dot(ya_ref[...].astype(BF16), wout_ref[hw:, :]))
    h = _rms(x, gx_ref[...]).astype(BF16)
    q = _dot(h, wq_ref[...])
    dh = d // X_HEADS
    heads = []
    for hh in range(X_HEADS):
        sl = slice(hh * dh, (hh + 1) * dh)
        s = _dot_nt(q[:, sl].astype(BF16), kx_ref[:, sl]) * (dh ** -0.5)
        s = s - jnp.max(s, axis=-1, keepdims=True)
        p = jnp.exp(s)
        p = p / jnp.sum(p, axis=-1, keepdims=True)
        heads.append(_dot(p.astype(BF16), vx_ref[:, sl]).astype(BF16))
    x = x + _dot(jnp.concatenate(heads, axis=1), wo_ref[...])
    xe_ref[:, :d] = x

    hf = _rms(x, gf_ref[...])
    h_hi, h_lo = _split_bf16(hf)
    w_hi, w_lo = _split_bf16(wr_ref[...])
    both = _dot(h_hi, jnp.concatenate([w_hi, w_lo], axis=1))
    logits = both[:, :LANES] + both[:, LANES:] + _dot(h_lo, w_hi)
    lane = lax.broadcasted_iota(I32, logits.shape, 1)
    big = 1 << 20

    def first_max(vals):
        top = jnp.max(vals, axis=-1, keepdims=True)
        return top, jnp.min(jnp.where(vals == top, lane, big), axis=-1, keepdims=True)

    is_g = lane < N_GROUPS
    g_top, g_sel = first_max(jnp.where(is_g, logits, NEG))
    g_w = 1.0 / jnp.sum(jnp.where(is_g, jnp.exp(logits - g_top), 0.0), axis=-1, keepdims=True)
    e_lo = N_GROUPS + EXPERTS_PER_GROUP * g_sel
    in_grp = jnp.logical_and(lane >= e_lo, lane < e_lo + EXPERTS_PER_GROUP)
    e_log = jnp.where(in_grp, logits, NEG)
    v1, i1 = first_max(e_log)
    v2, i2 = first_max(jnp.where(lane == i1, NEG, e_log))
    t = jnp.exp(v2 - v1)
    w1 = g_w / (1.0 + t)
    w2 = g_w * t / (1.0 + t)
    j1 = i1 - e_lo
    j2 = i2 - e_lo
    lo = jnp.minimum(j1, j2)
    hi = jnp.maximum(j1, j2)
    pair = jnp.where(lo == 0, hi - 1, jnp.where(lo == 1, hi + 1, N_PAIRS - 1))
    bucket = g_sel * N_PAIRS + pair
    w_lo = jnp.where(j1 < j2, w1, w2)
    w_hi = jnp.where(j1 < j2, w2, w1)
    xe_ref[:, d:] = jnp.where(lane == 0, w_lo, jnp.where(lane == 1, w_hi, 0.0))
    bkt_ref[...] = bucket
    cnt_ref[0] = jnp.sum((lane == bucket).astype(F32), axis=0, keepdims=True)


def cross_block(x2d, y_hg, y_att, w_out, g_cross, wq, kx, vx, wo, g_ffn, w_router, batch):
    n, d = x2d.shape
    tm = X_TILE
    nt = n // tm
    per_batch = nt // batch
    m = kx.shape[0] // batch
    row = lambda i: (i, 0)
    fix = lambda i: (0, 0)
    mem = lambda i: (i // per_batch, 0)
    return pl.pallas_call(
        _cross_kernel,
        grid=(nt,),
        in_specs=[pl.BlockSpec((tm, d), row), pl.BlockSpec((tm, y_hg.shape[1]), row),
                  pl.BlockSpec((tm, y_att.shape[1]), row), pl.BlockSpec(w_out.shape, fix),
                  pl.BlockSpec((1, d), fix), pl.BlockSpec(wq.shape, fix),
                  pl.BlockSpec((m, d), mem), pl.BlockSpec((m, d), mem), pl.BlockSpec(wo.shape, fix),
                  pl.BlockSpec((1, d), fix), pl.BlockSpec(w_router.shape, fix)],
        out_specs=[pl.BlockSpec((tm, d + LANES), row), pl.BlockSpec((tm, 1), row),
                   pl.BlockSpec((1, 1, LANES), lambda i: (i, 0, 0))],
        out_shape=[jax.ShapeDtypeStruct((n, d + LANES), F32), jax.ShapeDtypeStruct((n, 1), I32),
                   jax.ShapeDtypeStruct((nt, 1, LANES), F32)],
        compiler_params=_params(1),
        name="cross_block",
    )(x2d, y_hg, y_att, w_out, g_cross.reshape(1, d), wq, kx, vx, wo, g_ffn.reshape(1, d), w_router)


def _position_kernel(bkt_ref, base_ref, tri_ref, pos_ref):
    lane = lax.broadcasted_iota(I32, (bkt_ref.shape[0], LANES), 1)
    onehot = (lane == bkt_ref[...]).astype(F32)
    before = _dot(tri_ref[...], onehot.astype(BF16))
    pos = jnp.sum(onehot * (before + base_ref[0]), axis=-1, keepdims=True)
    pos_ref[...] = pos.astype(I32)


def sorted_positions(bucket, base):
    n = bucket.shape[0]
    nt = base.shape[0]
    tm = n // nt
    tri = jnp.asarray(np.tril(np.ones((tm, tm), np.float32), -1), BF16)
    return pl.pallas_call(
        _position_kernel,
        grid=(nt,),
        in_specs=[pl.BlockSpec((tm, 1), lambda i: (i, 0)),
                  pl.BlockSpec((1, 1, LANES), lambda i: (i, 0, 0)),
                  pl.BlockSpec((tm, tm), lambda i: (0, 0))],
        out_specs=pl.BlockSpec((tm, 1), lambda i: (i, 0)),
        out_shape=jax.ShapeDtypeStruct((n, 1), I32),
        compiler_params=_params(1),
        name="sorted_positions",
    )(bucket, base, tri)


def _moe_kernel(pos_ref, ea_ref, eb_ref, cnt_ref,
                x_hbm, g_ref, wga_ref, wgb_ref, wua_ref, wub_ref, wda_ref, wdb_ref, gfin_ref,
                out_hbm, src_ref, xbuf, obuf, gsem, ssem, *, final_norm):
    j = pl.program_id(0)
    nt = pl.num_programs(0)
    rows = xbuf.shape[1]
    d = obuf.shape[2]
    slot = j % 2
    other = 1 - slot

    def for_rows(n, fn):
        for g in range(rows // COPY_GROUP):
            @pl.when((g + 1) * COPY_GROUP <= n)
            def _():
                for r in range(g * COPY_GROUP, (g + 1) * COPY_GROUP):
                    fn(r, r % DMA_PRIORITIES)

        def single(r, _):
            fn(r, 0)
            return 0

        lax.fori_loop(n // COPY_GROUP * COPY_GROUP, n, single, 0)

    def wait_rows(n, bulk_copy, row_copy):
        n8 = pl.multiple_of(n // SUBLANES * SUBLANES, SUBLANES)

        @pl.when(n8 > 0)
        def _():
            bulk_copy(n8).wait()

        def single(r, _):
            row_copy.wait()
            return 0

        lax.fori_loop(n8, n, single, 0)

    def start_gather(tile, n, sl):
        def copy(r, priority):
            tok = src_ref[tile * rows + r]
            pltpu.make_async_copy(x_hbm.at[pl.ds(tok, 1)], xbuf.at[sl, pl.ds(r, 1)],
                                  gsem.at[sl]).start(priority=priority)
        for_rows(n, copy)

    def start_scatter(tile, n, sl):
        def copy(r, priority):
            tok = src_ref[tile * rows + r]
            pltpu.make_async_copy(obuf.at[sl, pl.ds(r, 1)], out_hbm.at[pl.ds(tok, 1)],
                                  ssem.at[sl]).start(priority=priority)
        for_rows(n, copy)

    def wait_gather(n, sl):
        wait_rows(n,
                  lambda m: pltpu.make_async_copy(x_hbm.at[pl.ds(0, m)], xbuf.at[sl, pl.ds(0, m)], gsem.at[sl]),
                  pltpu.make_async_copy(x_hbm.at[pl.ds(0, 1)], xbuf.at[sl, pl.ds(0, 1)], gsem.at[sl]))

    def wait_scatter(n, sl):
        wait_rows(n,
                  lambda m: pltpu.make_async_copy(obuf.at[sl, pl.ds(0, m)], out_hbm.at[pl.ds(0, m)], ssem.at[sl]),
                  pltpu.make_async_copy(obuf.at[sl, pl.ds(0, 1)], out_hbm.at[pl.ds(0, 1)], ssem.at[sl]))

    @pl.when(j == 0)
    def _():
        def fill(t, _):
            src_ref[pos_ref[t]] = t
            return 0

        lax.fori_loop(0, pos_ref.shape[0], fill, 0, unroll=8)
        xbuf[...] = jnp.zeros_like(xbuf)
        start_gather(0, cnt_ref[0], 0)

    used = cnt_ref[j] > 0
    nxt = jnp.minimum(j + 1, nt - 1)
    n_next = jnp.where(j + 1 < nt, cnt_ref[nxt], 0)
    n_prev = cnt_ref[jnp.maximum(j - 1, 0)]
    last_used = jnp.logical_and(used, n_next == 0)

    for static_slot in range(2):
        @pl.when(jnp.logical_and(used, slot == static_slot))
        def _():
            wait_gather(cnt_ref[j], static_slot)

            @pl.when(j >= 2)
            def _():
                wait_scatter(cnt_ref[jnp.maximum(j - 2, 0)], static_slot)

            start_gather(nxt, n_next, 1 - static_slot)

            @pl.when(j >= 1)
            def _():
                start_scatter(j - 1, n_prev, 1 - static_slot)

    @pl.when(cnt_ref[j] > 0)
    def _():
        xe = xbuf[slot]
        x = xe[:, :d]
        h = _rms(x, g_ref[...]).astype(BF16)
        lane = lax.broadcasted_iota(I32, (rows, LANES), 1)
        wts = xe[:, d:]
        w_a = jnp.sum(jnp.where(lane == 0, wts, 0.0), axis=-1, keepdims=True)
        w_b = jnp.sum(jnp.where(lane == 1, wts, 0.0), axis=-1, keepdims=True)
        act_a = (jax.nn.silu(_dot(h, wga_ref[...])) * _dot(h, wua_ref[...]) * w_a).astype(BF16)
        act_b = (jax.nn.silu(_dot(h, wgb_ref[...])) * _dot(h, wub_ref[...]) * w_b).astype(BF16)
        y = x + _dot(act_a, wda_ref[...]) + _dot(act_b, wdb_ref[...])
        if final_norm:
            y = _rms(y, gfin_ref[...])
        obuf[slot] = y

    @pl.when(last_used)
    def _():
        start_scatter(j, cnt_ref[j], slot)

        @pl.when(j >= 1)
        def _():
            wait_scatter(n_prev, other)

        wait_scatter(cnt_ref[j], slot)


def moe_ffn(x_ext, pos, tile_a, tile_b, tile_cnt, g_ffn, w_gate, w_up, w_down, g_final, final_norm):
    n, de = x_ext.shape
    d = de - LANES
    rows = MOE_TILE
    nt = tile_cnt.shape[0]
    ff = w_gate.shape[2]
    fix = lambda j, *_: (0, 0)
    exp_a = lambda j, pos, ea, eb, cnt: (ea[j], 0, 0)
    exp_b = lambda j, pos, ea, eb, cnt: (eb[j], 0, 0)
    up_spec = lambda im: pl.BlockSpec((None, d, ff), im)
    down_spec = lambda im: pl.BlockSpec((None, ff, d), im)
    grid_spec = pltpu.PrefetchScalarGridSpec(
        num_scalar_prefetch=4,
        grid=(nt,),
        in_specs=[pl.BlockSpec(memory_space=pl.ANY), pl.BlockSpec((1, d), fix),
                  up_spec(exp_a), up_spec(exp_b), up_spec(exp_a), up_spec(exp_b),
                  down_spec(exp_a), down_spec(exp_b), pl.BlockSpec((1, d), fix)],
        out_specs=pl.BlockSpec(memory_space=pl.ANY),
        scratch_shapes=[pltpu.SMEM((nt * rows,), I32),
                        pltpu.VMEM((2, rows, de), F32), pltpu.VMEM((2, rows, d), F32),
                        pltpu.SemaphoreType.DMA((2,)), pltpu.SemaphoreType.DMA((2,))],
    )
    return pl.pallas_call(
        functools.partial(_moe_kernel, final_norm=final_norm),
        grid_spec=grid_spec,
        out_shape=jax.ShapeDtypeStruct((n, d), F32),
        compiler_params=_params(1),
        name="moe_ffn",
    )(pos, tile_a, tile_b, tile_cnt, x_ext, g_ffn.reshape(1, d),
      w_gate, w_gate, w_up, w_up, w_down, w_down, g_final.reshape(1, d))


_PAIR_LO = np.array([0, 0, 0, 1, 1, 2], np.int32)
_PAIR_HI = np.array([1, 2, 3, 2, 3, 3], np.int32)


def _tile_plan(counts, n_tokens):
    rows = MOE_TILE
    counts = counts.astype(I32)
    total = jnp.sum(counts, axis=0)
    padded = (total + rows - 1) // rows * rows
    start = jnp.cumsum(padded) - padded
    base = start[None, :] + jnp.cumsum(counts, axis=0) - counts
    nt = n_tokens // rows + N_BUCKETS
    first_row = jnp.arange(nt, dtype=I32) * rows
    end = (start + padded)[:N_BUCKETS]
    bucket = jnp.sum((first_row[:, None] >= end[None, :]).astype(I32), axis=1)
    used = bucket < N_BUCKETS
    last_used = jnp.max(jnp.where(used, bucket, 0))
    b_eff = jnp.where(used, bucket, last_used)
    cnt = jnp.where(used, jnp.clip(total[b_eff] - (first_row - start[b_eff]), 0, rows), 0)
    grp = b_eff // N_PAIRS
    pair = b_eff % N_PAIRS
    exp_a = grp * EXPERTS_PER_GROUP + jnp.asarray(_PAIR_LO)[pair]
    exp_b = grp * EXPERTS_PER_GROUP + jnp.asarray(_PAIR_HI)[pair]
    return base.astype(F32), exp_a.astype(I32), exp_b.astype(I32), cnt.astype(I32)


def kernel(x, mem, positions, norm_mix, w_in, hg_lower_bounds, hg_out_norm, w_out, norm_cross, norm_mem,
           wq_x, wkv_x, wo_x, norm_ffn, w_router_group, w_router_expert, w_gate, w_up, w_down, norm_final):
    batch, seq, d = x.shape
    depth = w_in.shape[0]
    n = batch * seq
    hg_width = hg_lower_bounds.shape[1]
    att_width = (w_in.shape[2] - 4 * hg_width) // 3
    assert hg_width == HG_HEADS * HG_DK and seq % ATT_SUPER == 0 and n % TOK_TILE == 0

    lb_sm = jax.nn.softmax(hg_lower_bounds.astype(F32), axis=0)
    lbs = jnp.cumsum(lb_sm, axis=0) - lb_sm[0:1]
    cos, sin = rope_tables(positions)
    xs = x.reshape(n, d)
    mem2d = mem.reshape(-1, d)
    w_router = jnp.concatenate([w_router_group, w_router_expert], axis=-1)
    w_router = jnp.pad(w_router, ((0, 0), (0, 0), (0, LANES - w_router.shape[-1])))
    wg16 = w_gate.astype(BF16).reshape((-1,) + w_gate.shape[2:])
    wu16 = w_up.astype(BF16).reshape((-1,) + w_up.shape[2:])
    wd16 = w_down.astype(BF16).reshape((-1,) + w_down.shape[2:])

    for l in range(depth):
        hproj, aq, ak, av = in_projection(xs, norm_mix[l], w_in[l].astype(BF16), cos, sin, hg_width, att_width)
        y_hg = hgrn_mixer(hproj, lbs[l], hg_out_norm[l], batch, seq)
        y_att = dilated_attention(aq, ak, av, batch, seq)
        kx, vx = memory_kv(mem2d, norm_mem[l], wkv_x[l].astype(BF16), batch)
        x_ext, bucket, counts = cross_block(xs, y_hg, y_att, w_out[l].astype(BF16), norm_cross[l],
                                            wq_x[l].astype(BF16), kx, vx, wo_x[l].astype(BF16),
                                            norm_ffn[l], w_router[l], batch)
        base, exp_a, exp_b, cnt = _tile_plan(counts.reshape(n // POS_TILE, -1, LANES).sum(axis=1), n)
        pos = sorted_positions(bucket, base.reshape(base.shape[0], 1, LANES))
        xs = moe_ffn(x_ext, pos.reshape(n), exp_a + l * N_EXPERTS, exp_b + l * N_EXPERTS, cnt, norm_ffn[l],
                     wg16, wu16, wd16, norm_final, final_norm=(l == depth - 1))
    return xs.reshape(batch, seq, d)
```

```python
import functools

import numpy as np
import jax
import jax.numpy as jnp
from jax import lax
from jax.experimental import pallas as pl
from jax.experimental.pallas import tpu as pltpu

F32 = jnp.float32
BF16 = jnp.bfloat16
I32 = jnp.int32

NORM_EPS = 1e-6
ROPE_THETA = 10000.0
HG_HEADS = 4
HG_DK = 128
ATT_DH = 64
ATT_BLK = 128
ATT_DILATIONS = (1, 4, 16)
ATT_CLASSES = 16
ATT_SUPER = ATT_BLK * ATT_CLASSES
X_HEADS = 4
N_GROUPS = 4
EXPERTS_PER_GROUP = 4
N_EXPERTS = 16
N_PAIRS = 6
N_BUCKETS = N_GROUPS * N_PAIRS
EXPERT_FF = 512
LANES = 128
SUBLANES = 8
NEG = -1e30
LOG2_E = 1.4426950408889634

HG_T = 128
HG_STEP = 512
TOK_TILE = 512
X_TILE = 1024
POS_TILE = 1024
MOE_TILE = 256
DMA_PRIORITIES = 2
COPY_GROUP = 32
VMEM_LIMIT = 56 * 1024 * 1024


def _dot(a, b):
    return jnp.dot(a, b, preferred_element_type=F32)


def _dot_nt(a, b):
    return lax.dot_general(a, b, (((1,), (1,)), ((), ())), preferred_element_type=F32)


def _dot_tn(a, b):
    return lax.dot_general(a, b, (((0,), (0,)), ((), ())), preferred_element_type=F32)


def _rms(x, g):
    return x * lax.rsqrt(jnp.mean(x * x, axis=-1, keepdims=True) + NORM_EPS) * g


def _split_bf16(a):
    hi = a.astype(BF16)
    return hi, (a - hi.astype(F32)).astype(BF16)


def _params(n_axes):
    return pltpu.CompilerParams(dimension_semantics=("arbitrary",) * n_axes,
                                vmem_limit_bytes=VMEM_LIMIT)


def _rope_kernel(pos_ref, inv_ref, cos_ref, sin_ref):
    nfreq = ATT_DH // 2
    ngrp = LANES // nfreq
    q = pos_ref.shape[0] // ngrp
    lane = lax.broadcasted_iota(I32, (q, LANES), 1)
    grp = lane // nfreq
    pos = pos_ref[...].astype(F32)
    packed = pos[0:q]
    for k in range(1, ngrp):
        packed = jnp.where(grp == k, pos[k * q:(k + 1) * q], packed)
    ang = packed * inv_ref[...]
    c = jnp.cos(ang)
    s = jnp.sin(ang)
    neg = (lane % ATT_DH) < nfreq

    def spread(t, k):
        out = pltpu.roll(t, (-k * nfreq) % LANES, axis=1)
        for j in range(1, ngrp):
            out = jnp.where(grp == j, pltpu.roll(t, ((j - k) * nfreq) % LANES, axis=1), out)
        return out

    for k in range(ngrp):
        cos_ref[k * q:(k + 1) * q, :] = spread(c, k)
        sk = spread(s, k)
        sin_ref[k * q:(k + 1) * q, :] = jnp.where(neg, -sk, sk)


def rope_tables(positions):
    n = positions.size
    inv = ROPE_THETA ** (-jnp.arange(0, ATT_DH, 2, dtype=F32) / ATT_DH)
    inv = jnp.tile(inv, LANES // (ATT_DH // 2)).reshape(1, LANES)
    tm = TOK_TILE
    return pl.pallas_call(
        _rope_kernel,
        grid=(n // tm,),
        in_specs=[pl.BlockSpec((tm, 1), lambda i: (i, 0)),
                  pl.BlockSpec((1, LANES), lambda i: (0, 0))],
        out_specs=[pl.BlockSpec((tm, LANES), lambda i: (i, 0))] * 2,
        out_shape=[jax.ShapeDtypeStruct((n, LANES), F32)] * 2,
        compiler_params=_params(1),
        name="rope_tables",
    )(positions.reshape(n, 1), inv)


def _inproj_kernel(x_ref, g_ref, w_ref, cos_ref, sin_ref, hp_ref, q_ref, k_ref, v_ref):
    h = _rms(x_ref[...], g_ref[...]).astype(BF16)
    hgw = hp_ref.shape[1]
    aw = q_ref.shape[1]
    for c in range(hgw // aw):
        hp_ref[:, c * aw:(c + 1) * aw] = _dot(h, w_ref[:, c * aw:(c + 1) * aw])
    cos = cos_ref[...]
    sin = sin_ref[...]
    lane = lax.broadcasted_iota(I32, cos.shape, 1)
    first = (lane % ATT_DH) < ATT_DH // 2

    def rope(t, scale):
        for p in range(aw // LANES):
            tp = t[:, p * LANES:(p + 1) * LANES]
            rot = jnp.where(first, pltpu.roll(tp, LANES - ATT_DH // 2, axis=1),
                            pltpu.roll(tp, ATT_DH // 2, axis=1))
            yield (tp * cos + rot * sin) * scale

    aq = _dot(h, w_ref[:, hgw:hgw + aw])
    for p, blk in enumerate(rope(aq, ATT_DH ** -0.5 * LOG2_E)):
        q_ref[:, p * LANES:(p + 1) * LANES] = blk
    ak = _dot(h, w_ref[:, hgw + aw:hgw + 2 * aw])
    for p, blk in enumerate(rope(ak, 1.0)):
        k_ref[:, p * LANES:(p + 1) * LANES] = blk
    v_ref[...] = _dot(h, w_ref[:, hgw + 2 * aw:hgw + 3 * aw])


def in_projection(x2d, gain, w_in, cos, sin, hg_width, att_width):
    n, d = x2d.shape
    tm = TOK_TILE
    row = lambda i: (i, 0)
    fix = lambda i: (0, 0)
    return pl.pallas_call(
        _inproj_kernel,
        grid=(n // tm,),
        in_specs=[pl.BlockSpec((tm, d), row), pl.BlockSpec((1, d), fix),
                  pl.BlockSpec(w_in.shape, fix),
                  pl.BlockSpec((tm, LANES), row), pl.BlockSpec((tm, LANES), row)],
        out_specs=[pl.BlockSpec((tm, 4 * hg_width), row)] + [pl.BlockSpec((tm, att_width), row)] * 3,
        out_shape=[jax.ShapeDtypeStruct((n, 4 * hg_width), F32)]
        + [jax.ShapeDtypeStruct((n, att_width), F32)] * 3,
        compiler_params=_params(1),
        name="in_projection",
    )(x2d, gain.reshape(1, d), w_in, cos, sin)


def _level_table(t):
    ti = np.arange(t)[:, None]
    si = np.arange(t)[None, :]
    x = np.maximum(ti ^ si, 1)
    lvl = np.floor(np.log2(x)).astype(np.int32)
    diag = int(np.log2(t))
    return np.where(si < ti, lvl, np.where(si == ti, diag, -1)).astype(np.int32)


def _hgrn_kernel(q_ref, f_ref, i_ref, gt_ref, lb_ref, gain_ref, lvl_ref, y_ref, st_ref):
    t_rows = lvl_ref.shape[0]
    n_lev = t_rows.bit_length() - 1

    @pl.when(pl.program_id(1) == 0)
    def _():
        st_ref[...] = jnp.zeros_like(st_ref)

    row = lax.broadcasted_iota(I32, (t_rows, HG_DK), 0)
    lvl = lvl_ref[...]
    lb_all = lb_ref[...]
    for u, h in [(u, h) for u in range(q_ref.shape[0] // t_rows) for h in range(HG_HEADS)]:
        rs = slice(u * t_rows, (u + 1) * t_rows)
        sl = slice(h * HG_DK, (h + 1) * HG_DK)
        lb = lb_all[:, sl]
        f = lb + (1.0 - lb) * jax.nn.sigmoid(f_ref[rs, sl])
        g = jnp.log(f) * LOG2_E
        b = g
        s = 1
        while s < t_rows:
            b = b + jnp.where(row >= s, pltpu.roll(b, s, axis=0), 0.0)
            s *= 2
        kk = 1.0 - f
        q = q_ref[rs, sl]
        v16 = i_ref[rs, sl].astype(BF16)
        scores = jnp.where(lvl == n_lev, _dot_nt(q.astype(BF16), kk.astype(BF16)), 0.0)
        first = b - g
        last = b
        for j in range(n_lev):
            if j > 0:
                half = 1 << (j - 1)
                bit = ((row >> (j - 1)) & 1) == 1
                first = jnp.where(bit, pltpu.roll(first, half, axis=0), first)
                last = jnp.where(bit, last, pltpu.roll(last, t_rows - half, axis=0))
            qj = (q * jnp.exp2(b - first)).astype(BF16)
            kj = (kk * jnp.exp2(last - b)).astype(BF16)
            scores = jnp.where(lvl == j, _dot_nt(qj, kj), scores)
        st = st_ref[h]
        o = _dot(scores.astype(BF16), v16) + _dot_nt((q * jnp.exp2(b)).astype(BF16), st.astype(BF16))
        b_last = b[t_rows - 1:t_rows, :]
        kdec = (kk * jnp.exp2(b_last - b)).astype(BF16)
        st_ref[h] = jnp.exp2(b_last) * st + _dot_tn(v16, kdec)
        o = o * lax.rsqrt(jnp.mean(o * o, axis=-1, keepdims=True) + NORM_EPS) * gain_ref[:, sl]
        y_ref[rs, sl] = o * jax.nn.silu(gt_ref[rs, sl])


def hgrn_mixer(hproj, lb, out_gain, batch, seq):
    n, w4 = hproj.shape
    w = w4 // 4
    t = HG_STEP
    nb = seq // t
    col = lambda c: (lambda b, i: (b * nb + i, c))
    fix = lambda b, i: (0, 0)
    lvl = jnp.asarray(_level_table(HG_T))
    return pl.pallas_call(
        _hgrn_kernel,
        grid=(batch, nb),
        in_specs=[pl.BlockSpec((t, w), col(0)), pl.BlockSpec((t, w), col(1)),
                  pl.BlockSpec((t, w), col(2)), pl.BlockSpec((t, w), col(3)),
                  pl.BlockSpec((1, w), fix), pl.BlockSpec((1, w), fix),
                  pl.BlockSpec((HG_T, HG_T), fix)],
        out_specs=pl.BlockSpec((t, w), col(0)),
        out_shape=jax.ShapeDtypeStruct((n, w), F32),
        scratch_shapes=[pltpu.VMEM((HG_HEADS, HG_DK, HG_DK), F32)],
        compiler_params=_params(2),
        name="hgrn_mixer",
    )(hproj, hproj, hproj, hproj, lb.reshape(1, w), out_gain.reshape(1, w), lvl)


def _att_masks():
    blk = ATT_BLK
    rows = np.arange(blk)
    out = []
    for dil in ATT_DILATIONS:
        seg = blk * dil // ATT_CLASSES
        pos = (ATT_CLASSES // dil) * (rows % seg) + rows // seg
        kpos = np.concatenate([pos, blk + pos])
        dist = pos[:, None] + blk - kpos[None, :]
        out.append(((dist >= 0) & (dist <= blk)).astype(np.float32))
    return np.stack(out)


def _att_kernel(q_hbm, k_hbm, v_hbm, mask_ref, o_hbm,
                qbuf, kbuf, vbuf, acc, m_s, l_s, obuf, isem, osem):
    blk = ATT_BLK
    ncls = ATT_CLASSES
    npair = pl.num_programs(1)
    nsb = pl.num_programs(2)
    sb = pl.program_id(2)
    step = (pl.program_id(0) * npair + pl.program_id(1)) * nsb + sb
    n_steps = pl.num_programs(0) * npair * nsb
    lane = lax.broadcasted_iota(I32, (blk, LANES), 1)
    head0 = lane < ATT_DH
    col = lax.broadcasted_iota(I32, (blk, 2 * blk), 1)
    cur = step % 3
    prev = (step + 2) % 3

    def load_copies(s):
        row0 = pl.multiple_of(((s // (npair * nsb)) * nsb + s % nsb) * blk, blk)
        lane0 = pl.multiple_of(((s // nsb) % npair) * LANES, LANES)
        for r in range(ncls):
            dst = pl.ds(r * blk, blk)
            for hbm, buf, slot in ((q_hbm, qbuf, s % 2), (k_hbm, kbuf, s % 3), (v_hbm, vbuf, s % 3)):
                yield pltpu.make_async_copy(hbm.at[pl.ds(row0, blk), r, pl.ds(lane0, LANES)],
                                            buf.at[slot, dst, :], isem.at[s % 2])

    def store_copies(s):
        row0 = pl.multiple_of(((s // (npair * nsb)) * nsb + s % nsb) * blk, blk)
        lane0 = pl.multiple_of(((s // nsb) % npair) * LANES, LANES)
        for r in range(ncls):
            yield pltpu.make_async_copy(obuf.at[pl.ds(r * blk, blk), :],
                                        o_hbm.at[pl.ds(row0, blk), r, pl.ds(lane0, LANES)], osem.at[0])

    @pl.when(step == 0)
    def _():
        kbuf[2] = jnp.zeros(kbuf.shape[1:], kbuf.dtype)
        vbuf[2] = jnp.zeros(vbuf.shape[1:], vbuf.dtype)
        for cp in load_copies(step):
            cp.start()

    @pl.when(step + 1 < n_steps)
    def _():
        for cp in load_copies(step + 1):
            cp.start()

    for cp in load_copies(step):
        cp.wait()
    qs = step % 2

    def rows_of(buf, pieces, seg):
        return jnp.concatenate([buf[slot, pl.ds(o, seg), :] for slot, o in pieces], axis=0).astype(BF16)

    def block_attention(qb, kb, vb, valid):
        outs = []
        for hh in range(2):
            hm = head0 if hh == 0 else jnp.logical_not(head0)
            s = _dot_nt(jnp.where(hm, qb, jnp.zeros_like(qb)), kb)
            s = jnp.where(valid, s, NEG)
            m = jnp.max(s, axis=-1, keepdims=True)
            p = jnp.exp2(s - m)
            l = jnp.sum(p, axis=-1, keepdims=True)
            outs.append((_dot(p.astype(BF16), vb), m, l))
        (o0, m0, l0), (o1, m1, l1) = outs
        return (jnp.where(head0, o0, o1), jnp.where(head0, m0, m1), jnp.where(head0, l0, l1))

    def first_ok(is_first):
        return jnp.logical_or(col >= blk, jnp.logical_not(is_first))

    def pattern(p, dil):
        band = mask_ref[p] > 0.5
        nseg = ncls // dil
        seg = blk // nseg

        def body(idx, _):
            c = idx // nseg
            n = idx % nseg
            valid = jnp.logical_and(band, first_ok(jnp.logical_and(sb == 0, n == 0)))
            slot_p = jnp.where(n == 0, prev, cur)
            n_p = (n + nseg - 1) % nseg
            q_off = [pl.multiple_of((c + dil * k) * blk + seg * n, seg) for k in range(nseg)]
            p_off = [pl.multiple_of((c + dil * k) * blk + seg * n_p, seg) for k in range(nseg)]
            keys = [(slot_p, o) for o in p_off] + [(cur, o) for o in q_off]
            o, m, l = block_attention(rows_of(qbuf, [(qs, o) for o in q_off], seg),
                                      rows_of(kbuf, keys, seg), rows_of(vbuf, keys, seg), valid)
            for k, off in enumerate(q_off):
                rs = slice(k * seg, (k + 1) * seg)
                dst = pl.ds(off, seg)
                acc[p, dst, :] = o[rs]
                m_s[p, dst, :] = m[rs]
                l_s[p, dst, :] = l[rs]
            return 0

        lax.fori_loop(0, ncls, body, 0, unroll=True)

    for p, dil in enumerate(ATT_DILATIONS):
        pattern(p, dil)

    @pl.when(step > 0)
    def _():
        for cp in store_copies(step - 1):
            cp.wait()

    def combine(i, _):
        rows = pl.ds(pl.multiple_of(i * blk, blk), blk)
        ms = [m_s[p, rows, :] for p in range(len(ATT_DILATIONS))]
        top = functools.reduce(jnp.maximum, ms)
        ws = [jnp.exp2(m - top) for m in ms]
        num = sum(w * acc[p, rows, :] for p, w in enumerate(ws))
        den = sum(w * l_s[p, rows, :] for p, w in enumerate(ws))
        obuf[rows, :] = num / den
        return 0

    lax.fori_loop(0, ncls, combine, 0, unroll=2)
    for cp in store_copies(step):
        cp.start()

    @pl.when(step == n_steps - 1)
    def _():
        for cp in store_copies(step):
            cp.wait()


def dilated_attention(aq, ak, av, batch, seq):
    n, w = aq.shape
    nsb = seq // ATT_SUPER
    npair = w // LANES
    masks = jnp.asarray(_att_masks())
    view = lambda t: t.reshape(n // ATT_CLASSES, ATT_CLASSES, w)
    any_spec = pl.BlockSpec(memory_space=pl.ANY)
    out = pl.pallas_call(
        _att_kernel,
        grid=(batch, npair, nsb),
        in_specs=[any_spec, any_spec, any_spec, pl.BlockSpec(masks.shape, lambda b, hp, sb: (0, 0, 0))],
        out_specs=any_spec,
        out_shape=jax.ShapeDtypeStruct((n // ATT_CLASSES, ATT_CLASSES, w), F32),
        scratch_shapes=[pltpu.VMEM((2, ATT_SUPER, LANES), F32),
                        pltpu.VMEM((3, ATT_SUPER, LANES), F32),
                        pltpu.VMEM((3, ATT_SUPER, LANES), F32)]
        + [pltpu.VMEM((len(ATT_DILATIONS), ATT_SUPER, LANES), F32)] * 3
        + [pltpu.VMEM((ATT_SUPER, LANES), F32),
           pltpu.SemaphoreType.DMA((2,)), pltpu.SemaphoreType.DMA((1,))],
        compiler_params=_params(3),
        name="dilated_attention",
    )(view(aq), view(ak), view(av), masks)
    return out.reshape(n, w)


def _memkv_kernel(mem_ref, g_ref, w_ref, k_ref, v_ref):
    h = _rms(mem_ref[...], g_ref[...]).astype(BF16)
    d = k_ref.shape[1]
    k_ref[...] = _dot(h, w_ref[:, :d]).astype(BF16)
    v_ref[...] = _dot(h, w_ref[:, d:]).astype(BF16)


def memory_kv(mem2d, gain, wkv, batch):
    n, d = mem2d.shape
    m = n // batch
    row = lambda b: (b, 0)
    fix = lambda b: (0, 0)
    return pl.pallas_call(
        _memkv_kernel,
        grid=(batch,),
        in_specs=[pl.BlockSpec((m, d), row), pl.BlockSpec((1, d), fix), pl.BlockSpec(wkv.shape, fix)],
        out_specs=[pl.BlockSpec((m, d), row)] * 2,
        out_shape=[jax.ShapeDtypeStruct((n, d), BF16)] * 2,
        compiler_params=_params(1),
        name="memory_kv",
    )(mem2d, gain.reshape(1, d), wkv)


def _cross_kernel(x_ref, yh_ref, ya_ref, wout_ref, gx_ref, wq_ref, kx_ref, vx_ref, wo_ref,
                  gf_ref, wr_ref, xe_ref, bkt_ref, cnt_ref):
    d = x_ref.shape[1]
    hw = yh_ref.shape[1]
    x = (x_ref[...] + _dot(yh_ref[...].astype(BF16), wout_ref[:hw, :])
         + _dot(ya_ref[...].astype(BF16), wout_ref[hw:, :]))
    h = _rms(x, gx_ref[...]).astype(BF16)
    q = _dot(h, wq_ref[...])
    dh = d // X_HEADS
    heads = []
    for hh in range(X_HEADS):
        sl = slice(hh * dh, (hh + 1) * dh)
        s = _dot_nt(q[:, sl].astype(BF16), kx_ref[:, sl]) * (dh ** -0.5)
        s = s - jnp.max(s, axis=-1, keepdims=True)
        p = jnp.exp(s)
        p = p / jnp.sum(p, axis=-1, keepdims=True)
        heads.append(_dot(p.astype(BF16), vx_ref[:, sl]).astype(BF16))
    x = x + _dot(jnp.concatenate(heads, axis=1), wo_ref[...])
    xe_ref[:, :d] = x

    hf = _rms(x, gf_ref[...])
    h_hi, h_lo = _split_bf16(hf)
    w_hi, w_lo = _split_bf16(wr_ref[...])
    both = _dot(h_hi, jnp.concatenate([w_hi, w_lo], axis=1))
    logits = both[:, :LANES] + both[:, LANES:] + _dot(h_lo, w_hi)
    lane = lax.broadcasted_iota(I32, logits.shape, 1)
    big = 1 << 20

    def first_max(vals):
        top = jnp.max(vals, axis=-1, keepdims=True)
        return top, jnp.min(jnp.where(vals == top, lane, big), axis=-1, keepdims=True)

    is_g = lane < N_GROUPS
    g_top, g_sel = first_max(jnp.where(is_g, logits, NEG))
    g_w = 1.0 / jnp.sum(jnp.where(is_g, jnp.exp(logits - g_top), 0.0), axis=-1, keepdims=True)
    e_lo = N_GROUPS + EXPERTS_PER_GROUP * g_sel
    in_grp = jnp.logical_and(lane >= e_lo, lane < e_lo + EXPERTS_PER_GROUP)
    e_log = jnp.where(in_grp, logits, NEG)
    v1, i1 = first_max(e_log)
    v2, i2 = first_max(jnp.where(lane == i1, NEG, e_log))
    t = jnp.exp(v2 - v1)
    w1 = g_w / (1.0 + t)
    w2 = g_w * t / (1.0 + t)
    j1 = i1 - e_lo
    j2 = i2 - e_lo
    lo = jnp.minimum(j1, j2)
    hi = jnp.maximum(j1, j2)
    pair = jnp.where(lo == 0, hi - 1, jnp.where(lo == 1, hi + 1, N_PAIRS - 1))
    bucket = g_sel * N_PAIRS + pair
    w_lo = jnp.where(j1 < j2, w1, w2)
    w_hi = jnp.where(j1 < j2, w2, w1)
    xe_ref[:, d:] = jnp.where(lane == 0, w_lo, jnp.where(lane == 1, w_hi, 0.0))
    bkt_ref[...] = bucket
    cnt_ref[0] = jnp.sum((lane == bucket).astype(F32), axis=0, keepdims=True)


def cross_block(x2d, y_hg, y_att, w_out, g_cross, wq, kx, vx, wo, g_ffn, w_router, batch):
    n, d = x2d.shape
    tm = X_TILE
    nt = n // tm
    per_batch = nt // batch
    m = kx.shape[0] // batch
    row = lambda i: (i, 0)
    fix = lambda i: (0, 0)
    mem = lambda i: (i // per_batch, 0)
    return pl.pallas_call(
        _cross_kernel,
        grid=(nt,),
        in_specs=[pl.BlockSpec((tm, d), row), pl.BlockSpec((tm, y_hg.shape[1]), row),
                  pl.BlockSpec((tm, y_att.shape[1]), row), pl.BlockSpec(w_out.shape, fix),
                  pl.BlockSpec((1, d), fix), pl.BlockSpec(wq.shape, fix),
                  pl.BlockSpec((m, d), mem), pl.BlockSpec((m, d), mem), pl.BlockSpec(wo.shape, fix),
                  pl.BlockSpec((1, d), fix), pl.BlockSpec(w_router.shape, fix)],
        out_specs=[pl.BlockSpec((tm, d + LANES), row), pl.BlockSpec((tm, 1), row),
                   pl.BlockSpec((1, 1, LANES), lambda i: (i, 0, 0))],
        out_shape=[jax.ShapeDtypeStruct((n, d + LANES), F32), jax.ShapeDtypeStruct((n, 1), I32),
                   jax.ShapeDtypeStruct((nt, 1, LANES), F32)],
        compiler_params=_params(1),
        name="cross_block",
    )(x2d, y_hg, y_att, w_out, g_cross.reshape(1, d), wq, kx, vx, wo, g_ffn.reshape(1, d), w_router)


def _position_kernel(bkt_ref, base_ref, tri_ref, pos_ref):
    lane = lax.broadcasted_iota(I32, (bkt_ref.shape[0], LANES), 1)
    onehot = (lane == bkt_ref[...]).astype(F32)
    before = _dot(tri_ref[...], onehot.astype(BF16))
    pos = jnp.sum(onehot * (before + base_ref[0]), axis=-1, keepdims=True)
    pos_ref[...] = pos.astype(I32)


def sorted_positions(bucket, base):
    n = bucket.shape[0]
    nt = base.shape[0]
    tm = n // nt
    tri = jnp.asarray(np.tril(np.ones((tm, tm), np.float32), -1), BF16)
    return pl.pallas_call(
        _position_kernel,
        grid=(nt,),
        in_specs=[pl.BlockSpec((tm, 1), lambda i: (i, 0)),
                  pl.BlockSpec((1, 1, LANES), lambda i: (i, 0, 0)),
                  pl.BlockSpec((tm, tm), lambda i: (0, 0))],
        out_specs=pl.BlockSpec((tm, 1), lambda i: (i, 0)),
        out_shape=jax.ShapeDtypeStruct((n, 1), I32),
        compiler_params=_params(1),
        name="sorted_positions",
    )(bucket, base, tri)


def _moe_kernel(pos_ref, ea_ref, eb_ref, cnt_ref,
                x_hbm, g_ref, wga_ref, wgb_ref, wua_ref, wub_ref, wda_ref, wdb_ref, gfin_ref,
                out_hbm, src_ref, xbuf, obuf, gsem, ssem, *w16_refs, final_norm):
    j = pl.program_id(0)
    nt = pl.num_programs(0)
    rows = xbuf.shape[1]
    d = obuf.shape[2]
    slot = j % 2
    other = 1 - slot

    def for_rows(n, fn):
        for g in range(rows // COPY_GROUP):
            @pl.when((g + 1) * COPY_GROUP <= n)
            def _():
                for r in range(g * COPY_GROUP, (g + 1) * COPY_GROUP):
                    fn(r, r % DMA_PRIORITIES)

        def single(r, _):
            fn(r, 0)
            return 0

        lax.fori_loop(n // COPY_GROUP * COPY_GROUP, n, single, 0)

    def wait_rows(n, bulk_copy, row_copy):
        n8 = pl.multiple_of(n // SUBLANES * SUBLANES, SUBLANES)

        @pl.when(n8 > 0)
        def _():
            bulk_copy(n8).wait()

        def single(r, _):
            row_copy.wait()
            return 0

        lax.fori_loop(n8, n, single, 0)

    def start_gather(tile, n, sl):
        def copy(r, priority):
            tok = src_ref[tile * rows + r]
            pltpu.make_async_copy(x_hbm.at[pl.ds(tok, 1)], xbuf.at[sl, pl.ds(r, 1)],
                                  gsem.at[sl]).start(priority=priority)
        for_rows(n, copy)

    def start_scatter(tile, n, sl):
        def copy(r, priority):
            tok = src_ref[tile * rows + r]
            pltpu.make_async_copy(obuf.at[sl, pl.ds(r, 1)], out_hbm.at[pl.ds(tok, 1)],
                                  ssem.at[sl]).start(priority=priority)
        for_rows(n, copy)

    def wait_gather(n, sl):
        wait_rows(n,
                  lambda m: pltpu.make_async_copy(x_hbm.at[pl.ds(0, m)], xbuf.at[sl, pl.ds(0, m)], gsem.at[sl]),
                  pltpu.make_async_copy(x_hbm.at[pl.ds(0, 1)], xbuf.at[sl, pl.ds(0, 1)], gsem.at[sl]))

    def wait_scatter(n, sl):
        wait_rows(n,
                  lambda m: pltpu.make_async_copy(obuf.at[sl, pl.ds(0, m)], out_hbm.at[pl.ds(0, m)], ssem.at[sl]),
                  pltpu.make_async_copy(obuf.at[sl, pl.ds(0, 1)], out_hbm.at[pl.ds(0, 1)], ssem.at[sl]))

    @pl.when(j == 0)
    def _():
        def fill(t, _):
            src_ref[pos_ref[t]] = t
            return 0

        lax.fori_loop(0, pos_ref.shape[0], fill, 0, unroll=8)
        xbuf[...] = jnp.zeros_like(xbuf)
        start_gather(0, cnt_ref[0], 0)

    used = cnt_ref[j] > 0
    nxt = jnp.minimum(j + 1, nt - 1)
    n_next = jnp.where(j + 1 < nt, cnt_ref[nxt], 0)
    n_prev = cnt_ref[jnp.maximum(j - 1, 0)]
    last_used = jnp.logical_and(used, n_next == 0)

    for static_slot in range(2):
        @pl.when(jnp.logical_and(used, slot == static_slot))
        def _():
            wait_gather(cnt_ref[j], static_slot)

            @pl.when(j >= 2)
            def _():
                wait_scatter(cnt_ref[jnp.maximum(j - 2, 0)], static_slot)

            start_gather(nxt, n_next, 1 - static_slot)

            @pl.when(j >= 1)
            def _():
                start_scatter(j - 1, n_prev, 1 - static_slot)

    before = jnp.maximum(j - 1, 0)
    new_pair = jnp.logical_or(j == 0, jnp.logical_or(ea_ref[j] != ea_ref[before], eb_ref[j] != eb_ref[before]))

    @pl.when(jnp.logical_and(used, new_pair))
    def _():
        for w32, w16 in zip((wga_ref, wgb_ref, wua_ref, wub_ref, wda_ref, wdb_ref), w16_refs):
            w16[...] = w32[...].astype(BF16)

    @pl.when(cnt_ref[j] > 0)
    def _():
        wga16, wgb16, wua16, wub16, wda16, wdb16 = w16_refs
        xe = xbuf[slot]
        x = xe[:, :d]
        h = _rms(x, g_ref[...]).astype(BF16)
        lane = lax.broadcasted_iota(I32, (rows, LANES), 1)
        wts = xe[:, d:]
        w_a = jnp.sum(jnp.where(lane == 0, wts, 0.0), axis=-1, keepdims=True)
        w_b = jnp.sum(jnp.where(lane == 1, wts, 0.0), axis=-1, keepdims=True)
        act_a = (jax.nn.silu(_dot(h, wga16[...])) * _dot(h, wua16[...]) * w_a).astype(BF16)
        act_b = (jax.nn.silu(_dot(h, wgb16[...])) * _dot(h, wub16[...]) * w_b).astype(BF16)
        y = x + _dot(act_a, wda16[...]) + _dot(act_b, wdb16[...])
        if final_norm:
            y = _rms(y, gfin_ref[...])
        obuf[slot] = y

    @pl.when(last_used)
    def _():
        start_scatter(j, cnt_ref[j], slot)

        @pl.when(j >= 1)
        def _():
            wait_scatter(n_prev, other)

        wait_scatter(cnt_ref[j], slot)


def moe_ffn(x_ext, pos, tile_a, tile_b, tile_cnt, g_ffn, w_gate, w_up, w_down, g_final, final_norm):
    n, de = x_ext.shape
    d = de - LANES
    rows = MOE_TILE
    nt = tile_cnt.shape[0]
    ff = w_gate.shape[2]
    fix = lambda j, *_: (0, 0)
    exp_a = lambda j, pos, ea, eb, cnt: (ea[j], 0, 0)
    exp_b = lambda j, pos, ea, eb, cnt: (eb[j], 0, 0)
    up_spec = lambda im: pl.BlockSpec((None, d, ff), im)
    down_spec = lambda im: pl.BlockSpec((None, ff, d), im)
    grid_spec = pltpu.PrefetchScalarGridSpec(
        num_scalar_prefetch=4,
        grid=(nt,),
        in_specs=[pl.BlockSpec(memory_space=pl.ANY), pl.BlockSpec((1, d), fix),
                  up_spec(exp_a), up_spec(exp_b), up_spec(exp_a), up_spec(exp_b),
                  down_spec(exp_a), down_spec(exp_b), pl.BlockSpec((1, d), fix)],
        out_specs=pl.BlockSpec(memory_space=pl.ANY),
        scratch_shapes=[pltpu.SMEM((nt * rows,), I32),
                        pltpu.VMEM((2, rows, de), F32), pltpu.VMEM((2, rows, d), F32),
                        pltpu.SemaphoreType.DMA((2,)), pltpu.SemaphoreType.DMA((2,))]
        + [pltpu.VMEM((d, ff), BF16)] * 4 + [pltpu.VMEM((ff, d), BF16)] * 2,
    )
    return pl.pallas_call(
        functools.partial(_moe_kernel, final_norm=final_norm),
        grid_spec=grid_spec,
        out_shape=jax.ShapeDtypeStruct((n, d), F32),
        compiler_params=_params(1),
        name="moe_ffn",
    )(pos, tile_a, tile_b, tile_cnt, x_ext, g_ffn.reshape(1, d),
      w_gate, w_gate, w_up, w_up, w_down, w_down, g_final.reshape(1, d))


_PAIR_LO = np.array([0, 0, 0, 1, 1, 2], np.int32)
_PAIR_HI = np.array([1, 2, 3, 2, 3, 3], np.int32)


def _tile_plan(counts, n_tokens):
    rows = MOE_TILE
    counts = counts.astype(I32)
    total = jnp.sum(counts, axis=0)
    padded = (total + rows - 1) // rows * rows
    start = jnp.cumsum(padded) - padded
    base = start[None, :] + jnp.cumsum(counts, axis=0) - counts
    nt = n_tokens // rows + N_BUCKETS
    first_row = jnp.arange(nt, dtype=I32) * rows
    end = (start + padded)[:N_BUCKETS]
    bucket = jnp.sum((first_row[:, None] >= end[None, :]).astype(I32), axis=1)
    used = bucket < N_BUCKETS
    last_used = jnp.max(jnp.where(used, bucket, 0))
    b_eff = jnp.where(used, bucket, last_used)
    cnt = jnp.where(used, jnp.clip(total[b_eff] - (first_row - start[b_eff]), 0, rows), 0)
    grp = b_eff // N_PAIRS
    pair = b_eff % N_PAIRS
    exp_a = grp * EXPERTS_PER_GROUP + jnp.asarray(_PAIR_LO)[pair]
    exp_b = grp * EXPERTS_PER_GROUP + jnp.asarray(_PAIR_HI)[pair]
    return base.astype(F32), exp_a.astype(I32), exp_b.astype(I32), cnt.astype(I32)


def kernel(x, mem, positions, norm_mix, w_in, hg_lower_bounds, hg_out_norm, w_out, norm_cross, norm_mem,
           wq_x, wkv_x, wo_x, norm_ffn, w_router_group, w_router_expert, w_gate, w_up, w_down, norm_final):
    batch, seq, d = x.shape
    depth = w_in.shape[0]
    n = batch * seq
    hg_width = hg_lower_bounds.shape[1]
    att_width = (w_in.shape[2] - 4 * hg_width) // 3
    assert hg_width == HG_HEADS * HG_DK and seq % ATT_SUPER == 0 and n % TOK_TILE == 0

    lb_sm = jax.nn.softmax(hg_lower_bounds.astype(F32), axis=0)
    lbs = jnp.cumsum(lb_sm, axis=0) - lb_sm[0:1]
    cos, sin = rope_tables(positions)
    xs = x.reshape(n, d)
    mem2d = mem.reshape(-1, d)
    w_router = jnp.concatenate([w_router_group, w_router_expert], axis=-1)
    w_router = jnp.pad(w_router, ((0, 0), (0, 0), (0, LANES - w_router.shape[-1])))
    wg16 = w_gate.reshape((-1,) + w_gate.shape[2:])
    wu16 = w_up.reshape((-1,) + w_up.shape[2:])
    wd16 = w_down.reshape((-1,) + w_down.shape[2:])

    for l in range(depth):
        hproj, aq, ak, av = in_projection(xs, norm_mix[l], w_in[l].astype(BF16), cos, sin, hg_width, att_width)
        y_hg = hgrn_mixer(hproj, lbs[l], hg_out_norm[l], batch, seq)
        y_att = dilated_attention(aq, ak, av, batch, seq)
        kx, vx = memory_kv(mem2d, norm_mem[l], wkv_x[l].astype(BF16), batch)
        x_ext, bucket, counts = cross_block(xs, y_hg, y_att, w_out[l].astype(BF16), norm_cross[l],
                                            wq_x[l].astype(BF16), kx, vx, wo_x[l].astype(BF16),
                                            norm_ffn[l], w_router[l], batch)
        base, exp_a, exp_b, cnt = _tile_plan(counts.reshape(n // POS_TILE, -1, LANES).sum(axis=1), n)
        pos = sorted_positions(bucket, base.reshape(base.shape[0], 1, LANES))
        xs = moe_ffn(x_ext, pos.reshape(n), exp_a + l * N_EXPERTS, exp_b + l * N_EXPERTS, cnt, norm_ffn[l],
                     wg16, wu16, wd16, norm_final, final_norm=(l == depth - 1))
    return xs.reshape(batch, seq, d)
```
